```python
import math
import jax, jax.numpy as jnp
from jax import lax
import numpy as np

D_MODEL = 1024
BATCH = 4
SEQ = 8192
DEPTH = 1
DEC_BATCH = 8
DEC_SEQ = 8192
PAST_LEN = 128

HEAD_DIM = 64
N_DIFF_HEADS = 4
DIFF_V_DIM = 2 * HEAD_DIM
N_DIL_HEADS = 8
DIL_PATTERNS = ((128, 1), (512, 4), (2048, 16))
D_FF = 4 * D_MODEL
Q_BLOCK = 128
EPS = 1e-5
DIFF_QK_W = N_DIFF_HEADS * 2 * HEAD_DIM
DIFF_V_W = N_DIFF_HEADS * DIFF_V_DIM
DIL_W = N_DIL_HEADS * HEAD_DIM
IN_W = 2 * DIFF_QK_W + DIFF_V_W + 3 * DIL_W
MIX_W = DIFF_V_W + DIL_W

kernel_name = "hymba_diff_dilated_encoder"


def rmsnorm(x, g):
    xf = x.astype(jnp.float32)
    xf = xf * lax.rsqrt(jnp.mean(xf * xf, axis=-1, keepdims=True) + EPS)
    return (xf * g.astype(jnp.float32)).astype(x.dtype)


def alibi_slopes(n):
    return 2.0 ** (-8.0 * jnp.arange(1, n + 1, dtype=jnp.float32) / n)


def diff_attention(q, k, v, lam, slopes):
    B, S, H, _, dh = q.shape
    scale = dh ** -0.5
    nq = S // Q_BLOCK
    qb = q.reshape(B, nq, Q_BLOCK, H, 2, dh).transpose(1, 0, 2, 3, 4, 5)
    q0 = jnp.arange(nq, dtype=jnp.int32) * Q_BLOCK
    pos_k = jnp.arange(S, dtype=jnp.float32)

    def block(args):
        qblk, start = args
        s = jnp.einsum('bqhmd,bkhmd->bhmqk', qblk, k,
                       preferred_element_type=jnp.float32) * scale
        pos_q = (start + jnp.arange(Q_BLOCK, dtype=jnp.int32)).astype(jnp.float32)
        dist = jnp.abs(pos_q[:, None] - pos_k[None, :])
        s = s - slopes[:, None, None, None] * dist
        p = jax.nn.softmax(s, axis=-1)
        w = p[:, :, 0] - lam * p[:, :, 1]
        return jnp.einsum('bhqk,bkhe->bqhe', w.astype(v.dtype), v,
                          preferred_element_type=jnp.float32)

    o = lax.map(block, (qb, q0))
    return o.transpose(1, 0, 2, 3, 4).reshape(B, S, H, v.shape[-1])


def banded_dilated(q, k, v, dilation, radius, slopes):
    B, H, S, dh = q.shape
    d, r = dilation, radius
    L = S // d
    nb = -(-L // r)
    Lp = nb * r
    scale = dh ** -0.5

    def to_sub(t):
        return t.reshape(B, H, L, d, dh).transpose(0, 1, 3, 2, 4)

    qs = jnp.pad(to_sub(q), ((0, 0), (0, 0), (0, 0), (0, Lp - L), (0, 0)))
    pad_kv = ((0, 0), (0, 0), (0, 0), (r, Lp - L + r), (0, 0))
    ks = jnp.pad(to_sub(k), pad_kv).reshape(B, H, d, nb + 2, r, dh)
    vs = jnp.pad(to_sub(v), pad_kv).reshape(B, H, d, nb + 2, r, dh)
    qs = qs.reshape(B, H, d, nb, r, dh)
    kwin = jnp.concatenate([ks[:, :, :, :-2], ks[:, :, :, 1:-1], ks[:, :, :, 2:]], axis=4)
    vwin = jnp.concatenate([vs[:, :, :, :-2], vs[:, :, :, 1:-1], vs[:, :, :, 2:]], axis=4)

    blk = jnp.arange(nb, dtype=jnp.int32)[:, None] * r
    i_idx = blk + jnp.arange(r, dtype=jnp.int32)[None, :]
    j_idx = blk - r + jnp.arange(3 * r, dtype=jnp.int32)[None, :]
    rel = jnp.abs(i_idx[:, :, None] - j_idx[:, None, :])
    valid = (j_idx[:, None, :] >= 0) & (j_idx[:, None, :] < L) & (rel <= r)
    dist = (rel * d).astype(jnp.float32)

    s = jnp.einsum('bhgnqd,bhgnkd->bhgnqk', qs, kwin,
                   preferred_element_type=jnp.float32) * scale
    s = s - slopes[:, None, None, None, None] * dist
    s = jnp.where(valid, s, -jnp.inf)
    m = jnp.max(s, axis=-1, keepdims=True)
    e = jnp.exp(s - m)
    den = jnp.sum(e, axis=-1, keepdims=True)
    p = e / den
    lse = (m + jnp.log(den))[..., 0]
    o = jnp.einsum('bhgnqk,bhgnkd->bhgnqd', p.astype(v.dtype), vwin,
                   preferred_element_type=jnp.float32)
    o = o.reshape(B, H, d, Lp, dh)[:, :, :, :L].transpose(0, 1, 3, 2, 4).reshape(B, H, S, dh)
    lse = lse.reshape(B, H, d, Lp)[:, :, :, :L].transpose(0, 1, 3, 2).reshape(B, H, S)
    return o, lse


def dilated_mixture(q, k, v, slopes):
    B, S, H, dh = q.shape
    qt, kt, vt = (t.transpose(0, 2, 1, 3) for t in (q, k, v))
    outs, lses = [], []
    for window, dil in DIL_PATTERNS:
        o, lse = banded_dilated(qt, kt, vt, dil, window // (2 * dil), slopes)
        outs.append(o)
        lses.append(lse)
    wts = jax.nn.softmax(jnp.stack(lses, axis=0), axis=0)
    out = jnp.sum(wts[..., None] * jnp.stack(outs, axis=0), axis=0)
    return out.transpose(0, 2, 1, 3).reshape(B, S, H * dh)


def encoder_layer(x, g_mix, w_in, lam_qk, g_diff, g_dil, w_out, g_mlp, w_up, w_down, lam_init):
    B, S, _ = x.shape
    h = rmsnorm(x, g_mix)
    proj = h @ w_in
    splits = np.cumsum([DIFF_QK_W, DIFF_QK_W, DIFF_V_W, DIL_W, DIL_W]).tolist()
    qa, ka, va, qb, kb, vb = jnp.split(proj, splits, axis=-1)

    lq = lam_qk.astype(jnp.float32)
    lam = jnp.exp(jnp.dot(lq[0], lq[1])) - jnp.exp(jnp.dot(lq[2], lq[3])) + lam_init
    oa = diff_attention(qa.reshape(B, S, N_DIFF_HEADS, 2, HEAD_DIM),
                        ka.reshape(B, S, N_DIFF_HEADS, 2, HEAD_DIM),
                        va.reshape(B, S, N_DIFF_HEADS, DIFF_V_DIM),
                        lam, alibi_slopes(N_DIFF_HEADS))
    oa = oa * lax.rsqrt(jnp.mean(oa * oa, axis=-1, keepdims=True) + EPS)
    oa = (oa * g_diff.astype(jnp.float32) * (1.0 - lam_init)).reshape(B, S, DIFF_V_W)

    ob = dilated_mixture(qb.reshape(B, S, N_DIL_HEADS, HEAD_DIM),
                         kb.reshape(B, S, N_DIL_HEADS, HEAD_DIM),
                         vb.reshape(B, S, N_DIL_HEADS, HEAD_DIM),
                         alibi_slopes(N_DIL_HEADS))
    ob = rmsnorm(ob, g_dil)

    mix = jnp.concatenate([oa, ob], axis=-1).astype(x.dtype)
    x = x + mix @ w_out
    h = rmsnorm(x, g_mlp)
    x = x + jnp.square(jax.nn.relu(h @ w_up)) @ w_down
    return x


def trunk(x, g_mix, w_in, lam_qk, g_diff, g_dil, w_out, g_mlp, w_up, w_down, g_final):
    for l in range(DEPTH):
        lam_init = 0.8 - 0.6 * math.exp(-0.3 * l)
        x = encoder_layer(x, g_mix[l], w_in[l], lam_qk[l], g_diff[l], g_dil[l],
                          w_out[l], g_mlp[l], w_up[l], w_down[l], lam_init)
    return rmsnorm(x, g_final)


def setup_inputs(seed: int = 0) -> dict:
    key = jax.random.key(seed)
    ks = jax.random.split(key, 12)
    f32 = jnp.float32

    def gain(k, shape):
        return 1.0 + 0.02 * jax.random.normal(k, shape, f32)

    return {
        "x_prompt": jax.random.normal(ks[0], (BATCH, SEQ, D_MODEL), f32),
        "x_sample": jax.random.normal(ks[1], (DEC_BATCH, DEC_SEQ, D_MODEL), f32),
        "g_mix": gain(ks[2], (DEPTH, D_MODEL)),
        "w_in": jax.random.normal(ks[3], (DEPTH, D_MODEL, IN_W), f32) * D_MODEL ** -0.5,
        "lam_qk": 0.1 * jax.random.normal(ks[4], (DEPTH, 4, HEAD_DIM), f32),
        "g_diff": gain(ks[5], (DEPTH, N_DIFF_HEADS, DIFF_V_DIM)),
        "g_dil": gain(ks[6], (DEPTH, DIL_W)),
        "w_out": jax.random.normal(ks[7], (DEPTH, MIX_W, D_MODEL), f32) * MIX_W ** -0.5,
        "g_mlp": gain(ks[8], (DEPTH, D_MODEL)),
        "w_up": jax.random.normal(ks[9], (DEPTH, D_MODEL, D_FF), f32) * D_MODEL ** -0.5,
        "w_down": jax.random.normal(ks[10], (DEPTH, D_FF, D_MODEL), f32) * D_FF ** -0.5,
        "g_final": gain(ks[11], (D_MODEL,)),
    }


def reference(x_prompt, x_sample, g_mix, w_in, lam_qk, g_diff, g_dil, w_out,
              g_mlp, w_up, w_down, g_final):
    y_prompt = trunk(x_prompt, g_mix, w_in, lam_qk, g_diff, g_dil, w_out,
                     g_mlp, w_up, w_down, g_final)
    y_sample = trunk(x_sample, g_mix, w_in, lam_qk, g_diff, g_dil, w_out,
                     g_mlp, w_up, w_down, g_final)
    return (y_prompt, y_sample)
```

```python
import functools
import math

import jax
import jax.numpy as jnp
from jax import lax
from jax.experimental import pallas as pl
from jax.experimental.pallas import tpu as pltpu

D_MODEL = 1024
HEAD_DIM = 64
N_DIFF_HEADS = 4
N_DIL_HEADS = 8
N_DIL_PAIRS = N_DIL_HEADS // 2
DIL_PATTERNS = ((128, 1), (512, 4), (2048, 16))
DIL_RADIUS = 64
D_FF = 4 * D_MODEL
EPS = 1e-5
DIFF_W = 3 * N_DIFF_HEADS * 2 * HEAD_DIM
DIL_W = N_DIL_HEADS * HEAD_DIM
IN_W = DIFF_W + 3 * DIL_W
LANES = 128
PAIR_W = 3 * LANES
NEG = -1e30
VMEM_LIMIT = 56 * 1024 * 1024

BF16 = jnp.bfloat16
F32 = jnp.float32


def _rms(x, g):
    return x * lax.rsqrt(jnp.mean(x * x, axis=-1, keepdims=True) + EPS) * g


def _params(sem):
    return pltpu.CompilerParams(dimension_semantics=sem, vmem_limit_bytes=VMEM_LIMIT)


def _in_proj_kernel(x_ref, g_ref, w_ref, pa_ref, pb_ref):
    scale = HEAD_DIM ** -0.5
    h = _rms(x_ref[0], g_ref[...]).astype(BF16)
    qk_w = N_DIFF_HEADS * 2 * HEAD_DIM
    for c in range(DIFF_W // qk_w):
        p = jnp.dot(h, w_ref[:, c * qk_w:(c + 1) * qk_w], preferred_element_type=F32)
        if c == 0:
            p = p * scale
        pa_ref[0, :, c * qk_w:(c + 1) * qk_w] = p.astype(BF16)
    for c in range(3):
        p = jnp.dot(h, w_ref[:, DIFF_W + c * DIL_W:DIFF_W + (c + 1) * DIL_W],
                    preferred_element_type=F32)
        if c == 0:
            p = p * scale
        p = p.astype(BF16)
        for hp in range(N_DIL_PAIRS):
            pb_ref[0, hp, :, c * LANES:(c + 1) * LANES] = p[:, hp * LANES:(hp + 1) * LANES]


def _in_proj(x, g_mix, w_in, tm):
    B, S, D = x.shape
    return pl.pallas_call(
        _in_proj_kernel,
        grid=(B, S // tm),
        in_specs=[
            pl.BlockSpec((1, tm, D), lambda b, i: (b, i, 0)),
            pl.BlockSpec((1, D), lambda b, i: (0, 0)),
            pl.BlockSpec((D, IN_W), lambda b, i: (0, 0)),
        ],
        out_specs=[
            pl.BlockSpec((1, tm, DIFF_W), lambda b, i: (b, i, 0)),
            pl.BlockSpec((1, N_DIL_PAIRS, tm, PAIR_W), lambda b, i: (b, 0, i, 0)),
        ],
        out_shape=[
            jax.ShapeDtypeStruct((B, S, DIFF_W), BF16),
            jax.ShapeDtypeStruct((B, N_DIL_PAIRS, S, PAIR_W), BF16),
        ],
        compiler_params=_params(("parallel", "parallel")),
        name="in_proj",
    )(x, g_mix, w_in)


def _diff_attn_kernel(slopes_ref, lam_ref, g_ref, q_ref, k_ref, v_ref, o_ref,
                      q2_ref, m_ref, l_ref, acc_ref, *, bq, bk, lam_init):
    h = pl.program_id(1)
    qi = pl.program_id(2)
    seq = k_ref.shape[1]
    slope = slopes_ref[h]

    q = q_ref[0]
    lane = lax.broadcasted_iota(jnp.int32, q.shape, 1)
    zero = jnp.zeros_like(q)
    q2_ref[0:bq, :] = jnp.where(lane < HEAD_DIM, q, zero)
    q2_ref[bq:2 * bq, :] = jnp.where(lane >= HEAD_DIM, q, zero)
    m_ref[...] = jnp.full(m_ref.shape, -jnp.inf, F32)
    l_ref[...] = jnp.zeros(l_ref.shape, F32)
    acc_ref[...] = jnp.zeros(acc_ref.shape, F32)

    row = lax.broadcasted_iota(jnp.int32, (2 * bq, 1), 0)
    pos_q = (qi * bq + jnp.where(row >= bq, row - bq, row)).astype(F32)

    def body(j, carry):
        start = pl.multiple_of(j * bk, bk)
        k_t = k_ref[0, pl.ds(start, bk), :]
        v_t = v_ref[0, pl.ds(start, bk), :]
        s = lax.dot_general(q2_ref[...], k_t, (((1,), (1,)), ((), ())),
                            preferred_element_type=F32)
        pos_k = (j * bk + lax.broadcasted_iota(jnp.int32, (1, bk), 1)).astype(F32)
        s = s - slope * jnp.abs(pos_q - pos_k)
        m_old = m_ref[...]
        m_new = jnp.maximum(m_old, jnp.max(s, axis=1, keepdims=True))
        alpha = jnp.exp(m_old - m_new)
        p = jnp.exp(s - m_new)
        l_ref[...] = alpha * l_ref[...] + jnp.sum(p, axis=1, keepdims=True)
        acc_ref[...] = alpha * acc_ref[...] + jnp.dot(
            p.astype(BF16), v_t, preferred_element_type=F32)
        m_ref[...] = m_new
        return carry

    lax.fori_loop(0, seq // bk, body, 0)

    lq = lam_ref[...]
    lam = (jnp.exp(jnp.sum(lq[0:1] * lq[1:2], axis=1, keepdims=True))
           - jnp.exp(jnp.sum(lq[2:3] * lq[3:4], axis=1, keepdims=True)) + lam_init)
    o = acc_ref[...] / l_ref[...]
    od = o[0:bq] - lam * o[bq:2 * bq]
    od = od * lax.rsqrt(jnp.mean(od * od, axis=-1, keepdims=True) + EPS)
    od = od * g_ref[pl.ds(h, 1), :] * (1.0 - lam_init)
    o_ref[0] = od.astype(BF16)


def _diff_attn(pa, slopes, lam_qk, g_diff, lam_init, bq, bk):
    B, S, _ = pa.shape
    H = N_DIFF_HEADS
    kern = functools.partial(_diff_attn_kernel, bq=bq, bk=bk, lam_init=lam_init)
    return pl.pallas_call(
        kern,
        grid=(B, H, S // bq),
        in_specs=[
            pl.BlockSpec(memory_space=pltpu.SMEM),
            pl.BlockSpec((4, HEAD_DIM), lambda b, h, i: (0, 0)),
            pl.BlockSpec((H, LANES), lambda b, h, i: (0, 0)),
            pl.BlockSpec((1, bq, LANES), lambda b, h, i: (b, i, h)),
            pl.BlockSpec((1, S, LANES), lambda b, h, i: (b, 0, H + h)),
            pl.BlockSpec((1, S, LANES), lambda b, h, i: (b, 0, 2 * H + h)),
        ],
        out_specs=pl.BlockSpec((1, bq, LANES), lambda b, h, i: (b, i, h)),
        out_shape=jax.ShapeDtypeStruct((B, S, H * LANES), BF16),
        scratch_shapes=[
            pltpu.VMEM((2 * bq, LANES), BF16),
            pltpu.VMEM((2 * bq, 1), F32),
            pltpu.VMEM((2 * bq, 1), F32),
            pltpu.VMEM((2 * bq, LANES), F32),
        ],
        compiler_params=_params(("parallel", "parallel", "arbitrary")),
        name="diff_attn",
    )(slopes, lam_qk, g_diff, pa, pa, pa)


def _lane_col(tile, lane, idx):
    return jnp.sum(jnp.where(lane == idx, tile, 0.0), axis=1, keepdims=True)


def _dil_attn_kernel(*refs, dil, first, last):
    QB = LANES
    R = DIL_RADIUS
    KW = QB + 2 * R
    it = iter(refs)
    slopes_ref = next(it)
    q_ref, k_ref, v_ref = next(it), next(it), next(it)
    acc_in = st_in = acc_out = st_out = o_out = None
    if not first:
        acc_in, st_in = next(it), next(it)
    if last:
        o_out = next(it)
    else:
        acc_out, st_out = next(it), next(it)
    kpad, vpad = next(it), next(it)

    hp = pl.program_id(1)
    g = pl.program_id(2)
    L = q_ref.shape[2]
    slope_a = slopes_ref[2 * hp]
    slope_b = slopes_ref[2 * hp + 1]

    zpad = jnp.zeros((R, LANES), BF16)
    for pad, src in ((kpad, k_ref), (vpad, v_ref)):
        pad[0:R, :] = zpad
        pad[R:R + L, :] = src[0, 0]
        pad[R + L:R + L + R, :] = zpad

    r_i = lax.broadcasted_iota(jnp.int32, (QB, KW), 0)
    c_i = lax.broadcasted_iota(jnp.int32, (QB, KW), 1)
    rel = c_i - r_i - R
    band = (rel >= -R) & (rel <= R)
    dist = (jnp.abs(rel) * dil).astype(F32)
    bias2 = jnp.concatenate(
        [jnp.where(band, -slope_a * dist, NEG), jnp.where(band, -slope_b * dist, NEG)], axis=0)
    c2 = lax.broadcasted_iota(jnp.int32, (2 * QB, KW), 1)
    lane_q = lax.broadcasted_iota(jnp.int32, (QB, LANES), 1)
    head_a = lane_q < HEAD_DIM
    if st_in is not None or st_out is not None:
        st_w = (st_in if st_in is not None else st_out).shape[3]
        lane_s = lax.broadcasted_iota(jnp.int32, (QB, st_w), 1)

    if not first and not last:
        @pl.when(g == 0)
        def _():
            st_out[...] = st_in[...]

    def body(n, carry):
        i0 = pl.multiple_of(n * QB, QB)
        q = q_ref[0, 0, pl.ds(i0, QB), :]
        zero = jnp.zeros_like(q)
        q2 = jnp.concatenate([jnp.where(head_a, q, zero), jnp.where(head_a, zero, q)], axis=0)
        kw = kpad[pl.ds(i0, KW), :]
        vw = vpad[pl.ds(i0, KW), :]
        s = lax.dot_general(q2, kw, (((1,), (1,)), ((), ())), preferred_element_type=F32)
        s = s + bias2
        key = i0 - R + c2
        s = jnp.where((key >= 0) & (key < L), s, NEG)
        m_c = jnp.max(s, axis=1, keepdims=True)
        p = jnp.exp(s - m_c)
        l_c = jnp.sum(p, axis=1, keepdims=True)
        pv = jnp.dot(p.astype(BF16), vw, preferred_element_type=F32)
        acc_c = jnp.where(head_a, pv[0:QB], pv[QB:2 * QB])
        ma, mb = m_c[0:QB], m_c[QB:2 * QB]
        la, lb = l_c[0:QB], l_c[QB:2 * QB]
        if first:
            acc_n, ma_n, mb_n, la_n, lb_n = acc_c, ma, mb, la, lb
        else:
            st = st_in[0, 0, pl.ds(i0, QB), :]
            pma, pla = _lane_col(st, lane_s, 4 * g), _lane_col(st, lane_s, 4 * g + 1)
            pmb, plb = _lane_col(st, lane_s, 4 * g + 2), _lane_col(st, lane_s, 4 * g + 3)
            ma_n, mb_n = jnp.maximum(pma, ma), jnp.maximum(pmb, mb)
            ea_p, ea_c = jnp.exp(pma - ma_n), jnp.exp(ma - ma_n)
            eb_p, eb_c = jnp.exp(pmb - mb_n), jnp.exp(mb - mb_n)
            acc_p = acc_in[0, 0, pl.ds(i0, QB), :]
            acc_n = (acc_p * jnp.where(head_a, ea_p, eb_p)
                     + acc_c * jnp.where(head_a, ea_c, eb_c))
            la_n = pla * ea_p + la * ea_c
            lb_n = plb * eb_p + lb * eb_c
        if last:
            o_out[0, 0, pl.ds(i0, QB), :] = acc_n / jnp.where(head_a, la_n, lb_n)
        else:
            acc_out[0, 0, pl.ds(i0, QB), :] = acc_n
            if first:
                st_out[0, 0, pl.ds(i0, QB), :] = jnp.concatenate([ma_n, la_n, mb_n, lb_n], axis=1)
            else:
                t = st_out[0, 0, pl.ds(i0, QB), :]
                t = jnp.where(lane_s == 4 * g, ma_n, t)
                t = jnp.where(lane_s == 4 * g + 1, la_n, t)
                t = jnp.where(lane_s == 4 * g + 2, mb_n, t)
                t = jnp.where(lane_s == 4 * g + 3, lb_n, t)
                st_out[0, 0, pl.ds(i0, QB), :] = t
        return carry

    lax.fori_loop(0, L // LANES, body, 0)


def _dil_pass(pb, slopes, state, dil, first, last):
    B, P, S, _ = pb.shape
    L = S // dil
    pbv = pb.reshape(B, P, L, dil * PAIR_W)

    def blk(c):
        return pl.BlockSpec((1, 1, L, LANES), lambda b, p, g: (b, p, 0, 3 * g + c))

    acc_spec = pl.BlockSpec((1, 1, L, LANES), lambda b, p, g: (b, p, 0, g))
    st_spec = pl.BlockSpec((1, 1, L, 4 * dil), lambda b, p, g: (b, p, 0, 0))
    acc_shape = jax.ShapeDtypeStruct((B, P, L, dil * LANES), F32)
    st_shape = jax.ShapeDtypeStruct((B, P, L, 4 * dil), F32)
    in_specs = [pl.BlockSpec(memory_space=pltpu.SMEM), blk(0), blk(1), blk(2)]
    args = [slopes, pbv, pbv, pbv]
    if not first:
        acc, st = state
        in_specs += [acc_spec, st_spec]
        args += [acc.reshape(B, P, L, dil * LANES), st.reshape(B, P, L, 4 * dil)]
    if last:
        out_specs, out_shape = acc_spec, acc_shape
    else:
        out_specs, out_shape = [acc_spec, st_spec], [acc_shape, st_shape]
    out = pl.pallas_call(
        functools.partial(_dil_attn_kernel, dil=dil, first=first, last=last),
        grid=(B, P, dil),
        in_specs=in_specs,
        out_specs=out_specs,
        out_shape=out_shape,
        scratch_shapes=[pltpu.VMEM((L + 2 * DIL_RADIUS, LANES), BF16)] * 2,
        compiler_params=_params(("parallel", "parallel", "arbitrary")),
        name=f"dil_attn_d{dil}",
    )(*args)
    if last:
        return out.reshape(B, P, S, LANES)
    acc, st = out
    return acc.reshape(B, P, S, LANES), st.reshape(B, P, S, 4)


def _out_mlp_kernel(x_ref, oa_ref, ob_ref, gdil_ref, wout_ref, gmlp_ref, wup_ref,
                    wdown_ref, gfin_ref, y_ref, *, ff_chunk):
    ob = jnp.concatenate([ob_ref[0, p] for p in range(N_DIL_PAIRS)], axis=-1)
    ob = _rms(ob, gdil_ref[...])
    mix = jnp.concatenate([oa_ref[0], ob.astype(BF16)], axis=-1)
    x1 = x_ref[0] + jnp.dot(mix, wout_ref[...], preferred_element_type=F32)
    h = _rms(x1, gmlp_ref[...]).astype(BF16)
    y = x1
    for c in range(D_FF // ff_chunk):
        u = jnp.dot(h, wup_ref[:, c * ff_chunk:(c + 1) * ff_chunk], preferred_element_type=F32)
        u = jnp.square(jnp.maximum(u, 0.0)).astype(BF16)
        y = y + jnp.dot(u, wdown_ref[c * ff_chunk:(c + 1) * ff_chunk, :],
                        preferred_element_type=F32)
    y_ref[0] = _rms(y, gfin_ref[...])


def _out_mlp(x, oa, ob, g_dil, w_out, g_mlp, w_up, w_down, g_final, tm, ff_chunk):
    B, S, D = x.shape
    const = lambda shape: pl.BlockSpec(shape, lambda b, i: (0,) * len(shape),
                                       pipeline_mode=pl.Buffered(1))
    return pl.pallas_call(
        functools.partial(_out_mlp_kernel, ff_chunk=ff_chunk),
        grid=(B, S // tm),
        in_specs=[
            pl.BlockSpec((1, tm, D), lambda b, i: (b, i, 0)),
            pl.BlockSpec((1, tm, N_DIFF_HEADS * LANES), lambda b, i: (b, i, 0)),
            pl.BlockSpec((1, N_DIL_PAIRS, tm, LANES), lambda b, i: (b, 0, i, 0)),
            const((1, DIL_W)),
            const((2 * DIL_W, D)),
            const((1, D)),
            const((D, D_FF)),
            const((D_FF, D)),
            const((1, D)),
        ],
        out_specs=pl.BlockSpec((1, tm, D), lambda b, i: (b, i, 0)),
        out_shape=jax.ShapeDtypeStruct((B, S, D), F32),
        compiler_params=_params(("parallel", "parallel")),
        name="out_mlp",
    )(x, oa, ob, g_dil, w_out, g_mlp, w_up, w_down, g_final)


def _alibi_slopes(n):
    return 2.0 ** (-8.0 * jnp.arange(1, n + 1, dtype=F32) / n)


def _layer(x, g_mix, w_in, lam_qk, g_diff, g_dil, w_out, g_mlp, w_up, w_down, g_final,
           lam_init):
    pa, pb = _in_proj(x, g_mix, w_in, tm=512)
    oa = _diff_attn(pa, _alibi_slopes(N_DIFF_HEADS), lam_qk, g_diff, lam_init, bq=512, bk=512)
    dil_slopes = _alibi_slopes(N_DIL_HEADS)
    state = None
    for n, (window, dil) in enumerate(DIL_PATTERNS):
        assert window // (2 * dil) == DIL_RADIUS
        state = _dil_pass(pb, dil_slopes, state, dil, first=n == 0,
                          last=n == len(DIL_PATTERNS) - 1)
    return _out_mlp(x, oa, state, g_dil, w_out, g_mlp, w_up, w_down, g_final,
                    tm=512, ff_chunk=1024)


def kernel(x_prompt, x_sample, g_mix, w_in, lam_qk, g_diff, g_dil, w_out, g_mlp, w_up,
           w_down, g_final):
    assert g_mix.shape[0] == 1
    lam_init = 0.8 - 0.6 * math.exp(-0.3 * 0)
    weights = (
        g_mix[0][None, :], w_in[0].astype(BF16), lam_qk[0],
        g_diff[0], g_dil[0][None, :], w_out[0].astype(BF16), g_mlp[0][None, :],
        w_up[0].astype(BF16), w_down[0].astype(BF16), g_final[None, :],
    )
    return tuple(_layer(x, *weights, lam_init) for x in (x_prompt, x_sample))
```

```python
import functools
import math

import jax
import jax.numpy as jnp
from jax import lax
from jax.experimental import pallas as pl
from jax.experimental.pallas import tpu as pltpu

D_MODEL = 1024
HEAD_DIM = 64
N_DIFF_HEADS = 4
N_DIL_HEADS = 8
N_DIL_PAIRS = N_DIL_HEADS // 2
DIL_PATTERNS = ((128, 1), (512, 4), (2048, 16))
DIL_RADIUS = 64
D_FF = 4 * D_MODEL
EPS = 1e-5
DIFF_QW = N_DIFF_HEADS * 2 * HEAD_DIM
DIFF_W = 3 * DIFF_QW
DIL_W = N_DIL_HEADS * HEAD_DIM
IN_W = DIFF_W + 3 * DIL_W
LANES = 128
PAIR_W = 3 * LANES
NEG = -1e30
LOG2E = math.log2(math.e)
ONES_ROWS = 16
DIRECT_EXP_MAX_LOGIT = 40.0
VMEM_LIMIT = 56 * 1024 * 1024

BF16 = jnp.bfloat16
F32 = jnp.float32


def _rms(x, g):
    return x * lax.rsqrt(jnp.mean(x * x, axis=-1, keepdims=True) + EPS) * g


def _params(sem):
    return pltpu.CompilerParams(dimension_semantics=sem, vmem_limit_bytes=VMEM_LIMIT)


def _in_proj_kernel(x_ref, g_ref, w_ref, wqt_ref, wvt_ref, qt_ref, k_ref, vt_ref, pb_ref):
    scale = HEAD_DIM ** -0.5
    nt = (((1,), (1,)), ((), ()))
    h = _rms(x_ref[0], g_ref[...]).astype(BF16)
    qt = lax.dot_general(wqt_ref[...], h, nt, preferred_element_type=F32)
    qt_ref[0] = (qt * (scale * LOG2E)).astype(BF16)
    k_ref[0] = jnp.dot(h, w_ref[:, DIFF_QW:2 * DIFF_QW], preferred_element_type=F32).astype(BF16)
    vt = lax.dot_general(wvt_ref[...], h, nt, preferred_element_type=F32).astype(BF16)
    for hh in range(N_DIFF_HEADS):
        vt_ref[0, hh, 0] = vt[hh * LANES:(hh + 1) * LANES, :]
    for c in range(3):
        p = jnp.dot(h, w_ref[:, DIFF_W + c * DIL_W:DIFF_W + (c + 1) * DIL_W],
                    preferred_element_type=F32)
        if c == 0:
            p = p * scale
        p = p.astype(BF16)
        for hp in range(N_DIL_PAIRS):
            pb_ref[0, hp, :, c * LANES:(c + 1) * LANES] = p[:, hp * LANES:(hp + 1) * LANES]


def _in_proj(x, g_mix, w_in, wq_t, wv_t, tm):
    B, S, D = x.shape
    H = N_DIFF_HEADS
    return pl.pallas_call(
        _in_proj_kernel,
        grid=(B, S // tm),
        in_specs=[
            pl.BlockSpec((1, tm, D), lambda b, i: (b, i, 0)),
            pl.BlockSpec((1, D), lambda b, i: (0, 0)),
            pl.BlockSpec((D, IN_W), lambda b, i: (0, 0)),
            pl.BlockSpec((DIFF_QW, D), lambda b, i: (0, 0)),
            pl.BlockSpec((DIFF_QW, D), lambda b, i: (0, 0)),
        ],
        out_specs=[
            pl.BlockSpec((1, DIFF_QW, tm), lambda b, i: (b, 0, i)),
            pl.BlockSpec((1, tm, DIFF_QW), lambda b, i: (b, i, 0)),
            pl.BlockSpec((1, H, 1, LANES, tm), lambda b, i: (b, 0, i, 0, 0)),
            pl.BlockSpec((1, N_DIL_PAIRS, tm, PAIR_W), lambda b, i: (b, 0, i, 0)),
        ],
        out_shape=[
            jax.ShapeDtypeStruct((B, DIFF_QW, S), BF16),
            jax.ShapeDtypeStruct((B, S, DIFF_QW), BF16),
            jax.ShapeDtypeStruct((B, H, S // tm, LANES, tm), BF16),
            jax.ShapeDtypeStruct((B, N_DIL_PAIRS, S, PAIR_W), BF16),
        ],
        compiler_params=_params(("parallel", "parallel")),
        name="in_proj",
    )(x, g_mix, w_in, wq_t, wv_t)


def _diff_attn_kernel(slopes_ref, lam_ref, g_ref, qt_ref, k_ref, vt_ref, o_ref,
                      q2t_ref, m_ref, acc_ref, cb_ref, kmax_ref, *, bq, bk, lam_init):
    G = cb_ref.shape[1]
    h = pl.program_id(1)
    qi = pl.program_id(2)
    nkv = k_ref.shape[1] // bk
    slope2 = slopes_ref[h] * LOG2E

    qt = qt_ref[0]
    sub = lax.broadcasted_iota(jnp.int32, qt.shape, 0)
    zero = jnp.zeros_like(qt)
    q2t_ref[:, 0:bq] = jnp.where(sub < HEAD_DIM, qt, zero)
    q2t_ref[:, bq:2 * bq] = jnp.where(sub >= HEAD_DIM, qt, zero)
    m_ref[...] = jnp.full(m_ref.shape, -jnp.inf, F32)
    acc_ref[...] = jnp.zeros(acc_ref.shape, F32)
    key_off = lax.broadcasted_iota(jnp.int32, (bk, G), 0).astype(F32)
    cb_ref[...] = slope2 * key_off
    lane = lax.broadcasted_iota(jnp.int32, (1, 2 * bq), 1)
    pos_q = (qi * bq + jnp.where(lane >= bq, lane - bq, lane)).astype(F32)
    ones = jnp.ones((ONES_ROWS, bk), BF16)

    @pl.when(qi == 0)
    def _():
        def norm_tile(j, best):
            kf = k_ref[0, pl.ds(pl.multiple_of(j * bk, bk), bk), :].astype(F32)
            return jnp.maximum(best, jnp.max(jnp.sum(kf * kf, axis=1, keepdims=True)))
        kmax_ref[0] = lax.fori_loop(0, nkv, norm_tile, jnp.float32(0.0))

    q2f = q2t_ref[...].astype(F32)
    q_max = jnp.max(jnp.sum(q2f * q2f, axis=0, keepdims=True))
    direct = q_max * kmax_ref[0] <= DIRECT_EXP_MAX_LOGIT ** 2

    def tile(j, side, online):
        start = pl.multiple_of(j * bk, bk)
        k_t = k_ref[0, pl.ds(start, bk), :]
        vt1 = jnp.concatenate([vt_ref[0, 0, j], ones], axis=0)
        j0 = (j * bk).astype(F32)
        for gi in range(2 * bq // G):
            sl = slice(gi * G, (gi + 1) * G)
            s = jnp.dot(k_t, q2t_ref[:, sl], preferred_element_type=F32)
            pq = pos_q[:, sl]
            if side == 0:
                u = s - slope2 * jnp.abs(pq - (j0 + key_off))
                a = jnp.zeros_like(pq)
            elif side < 0:
                u = s + cb_ref[...]
                a = slope2 * (j0 - pq)
            else:
                u = s - cb_ref[...]
                a = slope2 * (pq - j0)
            if online:
                m_old = m_ref[:, sl]
                m_new = jnp.maximum(m_old, jnp.max(u, axis=0, keepdims=True) + a)
                alpha = jnp.exp2(m_old - m_new)
                p = jnp.exp2(u - (m_new - a)).astype(BF16)
                acc_ref[:, sl] = alpha * acc_ref[:, sl] + jnp.dot(
                    vt1, p, preferred_element_type=F32)
                m_ref[:, sl] = m_new
            else:
                p = jnp.exp2(u + a).astype(BF16)
                acc_ref[:, sl] += jnp.dot(vt1, p, preferred_element_type=F32)

    def all_tiles(online):
        def before(j, carry):
            tile(j, -1, online)
            return carry

        def after(j, carry):
            tile(j, 1, online)
            return carry

        lax.fori_loop(0, qi, before, 0)
        tile(qi, 0, online)
        lax.fori_loop(qi + 1, nkv, after, 0)

    @pl.when(direct)
    def _():
        all_tiles(False)

    @pl.when(jnp.logical_not(direct))
    def _():
        all_tiles(True)

    lq = lam_ref[...]
    lam = (jnp.exp(jnp.sum(lq[0:1] * lq[1:2], axis=1, keepdims=True))
           - jnp.exp(jnp.sum(lq[2:3] * lq[3:4], axis=1, keepdims=True)) + lam_init)
    acc = acc_ref[...]
    ot = acc[0:LANES] * (1.0 / acc[LANES:LANES + 1])
    od = (ot[:, 0:bq] - lam * ot[:, bq:2 * bq]).T
    od = od * lax.rsqrt(jnp.mean(od * od, axis=-1, keepdims=True) + EPS)
    od = od * g_ref[pl.ds(h, 1), :] * (1.0 - lam_init)
    o_ref[0] = od.astype(BF16)


def _diff_attn(qt, k, vt, slopes, lam_qk, g_diff, lam_init, bq):
    B, S, _ = k.shape
    H = N_DIFF_HEADS
    nkv, bk = vt.shape[2], vt.shape[4]
    assert bq == bk
    kern = functools.partial(_diff_attn_kernel, bq=bq, bk=bk, lam_init=lam_init)
    return pl.pallas_call(
        kern,
        grid=(B, H, S // bq),
        in_specs=[
            pl.BlockSpec(memory_space=pltpu.SMEM),
            pl.BlockSpec((4, HEAD_DIM), lambda b, h, i: (0, 0)),
            pl.BlockSpec((H, LANES), lambda b, h, i: (0, 0)),
            pl.BlockSpec((1, LANES, bq), lambda b, h, i: (b, h, i)),
            pl.BlockSpec((1, S, LANES), lambda b, h, i: (b, 0, h)),
            pl.BlockSpec((1, 1, nkv, LANES, bk), lambda b, h, i: (b, h, 0, 0, 0)),
        ],
        out_specs=pl.BlockSpec((1, bq, LANES), lambda b, h, i: (b, i, h)),
        out_shape=jax.ShapeDtypeStruct((B, S, H * LANES), BF16),
        scratch_shapes=[
            pltpu.VMEM((LANES, 2 * bq), BF16),
            pltpu.VMEM((1, 2 * bq), F32),
            pltpu.VMEM((LANES + ONES_ROWS, 2 * bq), F32),
            pltpu.VMEM((bk, 8 * LANES), F32),
            pltpu.SMEM((1,), F32),
        ],
        compiler_params=_params(("parallel", "parallel", "arbitrary")),
        name="diff_attn",
    )(slopes, lam_qk, g_diff, qt, k, vt)


def _lane_col(tile, lane, idx):
    return jnp.sum(jnp.where(lane == idx, tile, 0.0), axis=1, keepdims=True)


def _dil_attn_kernel(*refs, dil, first, last):
    QB = LANES
    R = DIL_RADIUS
    KW = QB + 2 * R
    it = iter(refs)
    slopes_ref = next(it)
    q_ref, k_ref, v_ref = next(it), next(it), next(it)
    acc_in = st_in = acc_out = st_out = o_out = None
    if not first:
        acc_in, st_in = next(it), next(it)
    if last:
        o_out = next(it)
    else:
        acc_out, st_out = next(it), next(it)
    kpad, vpad = next(it), next(it)

    hp = pl.program_id(1)
    g = pl.program_id(2)
    L = q_ref.shape[2]
    slope_a = slopes_ref[2 * hp]
    slope_b = slopes_ref[2 * hp + 1]

    zpad = jnp.zeros((R, LANES), BF16)
    for pad, src in ((kpad, k_ref), (vpad, v_ref)):
        pad[0:R, :] = zpad
        pad[R:R + L, :] = src[0, 0]
        pad[R + L:R + L + R, :] = zpad

    r_i = lax.broadcasted_iota(jnp.int32, (QB, KW), 0)
    c_i = lax.broadcasted_iota(jnp.int32, (QB, KW), 1)
    rel = c_i - r_i - R
    band = (rel >= -R) & (rel <= R)
    dist = (jnp.abs(rel) * dil).astype(F32)
    bias2 = jnp.concatenate(
        [jnp.where(band, -slope_a * dist, NEG), jnp.where(band, -slope_b * dist, NEG)], axis=0)
    c2 = lax.broadcasted_iota(jnp.int32, (2 * QB, KW), 1)
    lane_q = lax.broadcasted_iota(jnp.int32, (QB, LANES), 1)
    head_a = lane_q < HEAD_DIM
    if st_in is not None or st_out is not None:
        st_w = (st_in if st_in is not None else st_out).shape[3]
        lane_s = lax.broadcasted_iota(jnp.int32, (QB, st_w), 1)

    if not first and not last:
        @pl.when(g == 0)
        def _():
            st_out[...] = st_in[...]

    def body(n, carry):
        i0 = pl.multiple_of(n * QB, QB)
        q = q_ref[0, 0, pl.ds(i0, QB), :]
        zero = jnp.zeros_like(q)
        q2 = jnp.concatenate([jnp.where(head_a, q, zero), jnp.where(head_a, zero, q)], axis=0)
        kw = kpad[pl.ds(i0, KW), :]
        vw = vpad[pl.ds(i0, KW), :]
        s = lax.dot_general(q2, kw, (((1,), (1,)), ((), ())), preferred_element_type=F32)
        s = s + bias2
        key = i0 - R + c2
        s = jnp.where((key >= 0) & (key < L), s, NEG)
        m_c = jnp.max(s, axis=1, keepdims=True)
        p = jnp.exp(s - m_c)
        l_c = jnp.sum(p, axis=1, keepdims=True)
        pv = jnp.dot(p.astype(BF16), vw, preferred_element_type=F32)
        acc_c = jnp.where(head_a, pv[0:QB], pv[QB:2 * QB])
        ma, mb = m_c[0:QB], m_c[QB:2 * QB]
        la, lb = l_c[0:QB], l_c[QB:2 * QB]
        if first:
            acc_n, ma_n, mb_n, la_n, lb_n = acc_c, ma, mb, la, lb
        else:
            st = st_in[0, 0, pl.ds(i0, QB), :]
            pma, pla = _lane_col(st, lane_s, 4 * g), _lane_col(st, lane_s, 4 * g + 1)
            pmb, plb = _lane_col(st, lane_s, 4 * g + 2), _lane_col(st, lane_s, 4 * g + 3)
            ma_n, mb_n = jnp.maximum(pma, ma), jnp.maximum(pmb, mb)
            ea_p, ea_c = jnp.exp(pma - ma_n), jnp.exp(ma - ma_n)
            eb_p, eb_c = jnp.exp(pmb - mb_n), jnp.exp(mb - mb_n)
            acc_p = acc_in[0, 0, pl.ds(i0, QB), :]
            acc_n = (acc_p * jnp.where(head_a, ea_p, eb_p)
                     + acc_c * jnp.where(head_a, ea_c, eb_c))
            la_n = pla * ea_p + la * ea_c
            lb_n = plb * eb_p + lb * eb_c
        if last:
            o_out[0, 0, pl.ds(i0, QB), :] = acc_n / jnp.where(head_a, la_n, lb_n)
        else:
            acc_out[0, 0, pl.ds(i0, QB), :] = acc_n
            if first:
                st_out[0, 0, pl.ds(i0, QB), :] = jnp.concatenate([ma_n, la_n, mb_n, lb_n], axis=1)
            else:
                t = st_out[0, 0, pl.ds(i0, QB), :]
                t = jnp.where(lane_s == 4 * g, ma_n, t)
                t = jnp.where(lane_s == 4 * g + 1, la_n, t)
                t = jnp.where(lane_s == 4 * g + 2, mb_n, t)
                t = jnp.where(lane_s == 4 * g + 3, lb_n, t)
                st_out[0, 0, pl.ds(i0, QB), :] = t
        return carry

    lax.fori_loop(0, L // LANES, body, 0)


def _dil_pass(pb, slopes, state, dil, first, last):
    B, P, S, _ = pb.shape
    L = S // dil
    pbv = pb.reshape(B, P, L, dil * PAIR_W)

    def blk(c):
        return pl.BlockSpec((1, 1, L, LANES), lambda b, p, g: (b, p, 0, 3 * g + c))

    acc_spec = pl.BlockSpec((1, 1, L, LANES), lambda b, p, g: (b, p, 0, g))
    st_spec = pl.BlockSpec((1, 1, L, 4 * dil), lambda b, p, g: (b, p, 0, 0))
    acc_shape = jax.ShapeDtypeStruct((B, P, L, dil * LANES), F32)
    st_shape = jax.ShapeDtypeStruct((B, P, L, 4 * dil), F32)
    in_specs = [pl.BlockSpec(memory_space=pltpu.SMEM), blk(0), blk(1), blk(2)]
    args = [slopes, pbv, pbv, pbv]
    if not first:
        acc, st = state
        in_specs += [acc_spec, st_spec]
        args += [acc.reshape(B, P, L, dil * LANES), st.reshape(B, P, L, 4 * dil)]
    if last:
        out_specs, out_shape = acc_spec, acc_shape
    else:
        out_specs, out_shape = [acc_spec, st_spec], [acc_shape, st_shape]
    out = pl.pallas_call(
        functools.partial(_dil_attn_kernel, dil=dil, first=first, last=last),
        grid=(B, P, dil),
        in_specs=in_specs,
        out_specs=out_specs,
        out_shape=out_shape,
        scratch_shapes=[pltpu.VMEM((L + 2 * DIL_RADIUS, LANES), BF16)] * 2,
        compiler_params=_params(("parallel", "parallel", "arbitrary")),
        name=f"dil_attn_d{dil}",
    )(*args)
    if last:
        return out.reshape(B, P, S, LANES)
    acc, st = out
    return acc.reshape(B, P, S, LANES), st.reshape(B, P, S, 4)


def _out_mlp_kernel(x_ref, oa_ref, ob_ref, gdil_ref, wout_ref, gmlp_ref, wup_ref,
                    wdown_ref, gfin_ref, y_ref, *, ff_chunk):
    ob = jnp.concatenate([ob_ref[0, p] for p in range(N_DIL_PAIRS)], axis=-1)
    ob = _rms(ob, gdil_ref[...])
    mix = jnp.concatenate([oa_ref[0], ob.astype(BF16)], axis=-1)
    x1 = x_ref[0] + jnp.dot(mix, wout_ref[...], preferred_element_type=F32)
    h = _rms(x1, gmlp_ref[...]).astype(BF16)
    y = x1
    for c in range(D_FF // ff_chunk):
        u = jnp.dot(h, wup_ref[:, c * ff_chunk:(c + 1) * ff_chunk], preferred_element_type=F32)
        u = jnp.square(jnp.maximum(u, 0.0)).astype(BF16)
        y = y + jnp.dot(u, wdown_ref[c * ff_chunk:(c + 1) * ff_chunk, :],
                        preferred_element_type=F32)
    y_ref[0] = _rms(y, gfin_ref[...])


def _out_mlp(x, oa, ob, g_dil, w_out, g_mlp, w_up, w_down, g_final, tm, ff_chunk):
    B, S, D = x.shape
    const = lambda shape: pl.BlockSpec(shape, lambda b, i: (0,) * len(shape),
                                       pipeline_mode=pl.Buffered(1))
    return pl.pallas_call(
        functools.partial(_out_mlp_kernel, ff_chunk=ff_chunk),
        grid=(B, S // tm),
        in_specs=[
            pl.BlockSpec((1, tm, D), lambda b, i: (b, i, 0)),
            pl.BlockSpec((1, tm, N_DIFF_HEADS * LANES), lambda b, i: (b, i, 0)),
            pl.BlockSpec((1, N_DIL_PAIRS, tm, LANES), lambda b, i: (b, 0, i, 0)),
            const((1, DIL_W)),
            const((2 * DIL_W, D)),
            const((1, D)),
            const((D, D_FF)),
            const((D_FF, D)),
            const((1, D)),
        ],
        out_specs=pl.BlockSpec((1, tm, D), lambda b, i: (b, i, 0)),
        out_shape=jax.ShapeDtypeStruct((B, S, D), F32),
        compiler_params=_params(("parallel", "parallel")),
        name="out_mlp",
    )(x, oa, ob, g_dil, w_out, g_mlp, w_up, w_down, g_final)


def _alibi_slopes(n):
    return 2.0 ** (-8.0 * jnp.arange(1, n + 1, dtype=F32) / n)


def _layer(x, g_mix, w_in, wq_t, wv_t, lam_qk, g_diff, g_dil, w_out, g_mlp, w_up, w_down,
           g_final, lam_init):
    qt, k, vt, pb = _in_proj(x, g_mix, w_in, wq_t, wv_t, tm=512)
    oa = _diff_attn(qt, k, vt, _alibi_slopes(N_DIFF_HEADS), lam_qk, g_diff, lam_init, bq=512)
    dil_slopes = _alibi_slopes(N_DIL_HEADS)
    state = None
    for n, (window, dil) in enumerate(DIL_PATTERNS):
        assert window // (2 * dil) == DIL_RADIUS
        state = _dil_pass(pb, dil_slopes, state, dil, first=n == 0,
                          last=n == len(DIL_PATTERNS) - 1)
    return _out_mlp(x, oa, state, g_dil, w_out, g_mlp, w_up, w_down, g_final,
                    tm=512, ff_chunk=1024)


def kernel(x_prompt, x_sample, g_mix, w_in, lam_qk, g_diff, g_dil, w_out, g_mlp, w_up,
           w_down, g_final):
    assert g_mix.shape[0] == 1
    lam_init = 0.8 - 0.6 * math.exp(-0.3 * 0)
    w_in_b = w_in[0].astype(BF16)
    weights = (
        g_mix[0][None, :], w_in_b, w_in_b[:, 0:DIFF_QW].T, w_in_b[:, 2 * DIFF_QW:DIFF_W].T,
        lam_qk[0],
        g_diff[0], g_dil[0][None, :], w_out[0].astype(BF16), g_mlp[0][None, :],
        w_up[0].astype(BF16), w_down[0].astype(BF16), g_final[None, :],
    )
    return tuple(_layer(x, *weights, lam_init) for x in (x_prompt, x_sample))
```

```python
import functools
import math

import jax
import jax.numpy as jnp
from jax import lax
from jax.experimental import pallas as pl
from jax.experimental.pallas import tpu as pltpu

D_MODEL = 1024
HEAD_DIM = 64
N_DIFF_HEADS = 4
N_DIL_HEADS = 8
N_DIL_PAIRS = N_DIL_HEADS // 2
DIL_PATTERNS = ((128, 1), (512, 4), (2048, 16))
DIL_RADIUS = 64
D_FF = 4 * D_MODEL
EPS = 1e-5
DIFF_QW = N_DIFF_HEADS * 2 * HEAD_DIM
DIFF_W = 3 * DIFF_QW
DIL_W = N_DIL_HEADS * HEAD_DIM
IN_W = DIFF_W + 3 * DIL_W
LANES = 128
PAIR_W = 3 * LANES
NEG = -1e30
LOG2E = math.log2(math.e)
ONES_ROWS = 16
DIRECT_EXP_MAX_LOGIT = 60.0
VMEM_LIMIT = 56 * 1024 * 1024

BF16 = jnp.bfloat16
F32 = jnp.float32


def _rms(x, g):
    return x * lax.rsqrt(jnp.mean(x * x, axis=-1, keepdims=True) + EPS) * g


def _params(sem):
    return pltpu.CompilerParams(dimension_semantics=sem, vmem_limit_bytes=VMEM_LIMIT)


def _in_proj_kernel(x_ref, g_ref, w_ref, wqt_ref, wvt_ref, qt_ref, k_ref, vt_ref,
                    pb1_ref, pb4_ref, pb16_ref, nrm_ref, slab_ref, slab4_ref):
    scale = HEAD_DIM ** -0.5
    nt = (((1,), (1,)), ((), ()))
    tm = x_ref.shape[1]
    h = _rms(x_ref[0], g_ref[...]).astype(BF16)
    qt = lax.dot_general(wqt_ref[...], h, nt, preferred_element_type=F32)
    qt_ref[0] = (qt * (scale * LOG2E)).astype(BF16)
    k_ref[0] = jnp.dot(h, w_ref[:, DIFF_QW:2 * DIFF_QW], preferred_element_type=F32).astype(BF16)
    vt = lax.dot_general(wvt_ref[...], h, nt, preferred_element_type=F32).astype(BF16)
    for hh in range(N_DIFF_HEADS):
        vt_ref[0, hh, 0] = vt[hh * LANES:(hh + 1) * LANES, :]
    lane = lax.broadcasted_iota(jnp.int32, (1, LANES), 1)
    nrm = jnp.zeros((1, LANES), F32)
    for c in range(3):
        p = jnp.dot(h, w_ref[:, DIFF_W + c * DIL_W:DIFF_W + (c + 1) * DIL_W],
                    preferred_element_type=F32)
        if c == 0:
            p = p * (scale * LOG2E)
        cols = slice(c * LANES, (c + 1) * LANES)
        for hp in range(N_DIL_PAIRS):
            ph = p[:, hp * LANES:(hp + 1) * LANES]
            if c < 2:
                sq = ph * ph
                for e in range(2):
                    mine = (lane >= e * HEAD_DIM) & (lane < (e + 1) * HEAD_DIM)
                    big = jnp.max(jnp.sum(jnp.where(mine, sq, 0.0), axis=1, keepdims=True),
                                  axis=0, keepdims=True)
                    nrm = jnp.where(lane == 4 * hp + 2 * e + c, big, nrm)
            pb1_ref[0, hp, :, cols] = ph.astype(BF16)
            s = c * N_DIL_PAIRS + hp
            slab_ref[s] = ph
            for g in range(4):
                v4 = slab_ref[s, pl.ds(g, tm // 4, stride=4), :]
                pb4_ref[0, hp, g, :, cols] = v4.astype(BF16)
                slab4_ref[s, g] = v4
            for g in range(4):
                for c2 in range(4):
                    v16 = slab4_ref[s, g, pl.ds(c2, tm // 16, stride=4), :]
                    pb16_ref[0, hp, 4 * c2 + g, :, cols] = v16.astype(BF16)
    nrm_ref[0, 0] = nrm


def _in_proj(x, g_mix, w_in, wq_t, wv_t, tm):
    B, S, D = x.shape
    H = N_DIFF_HEADS
    P = N_DIL_PAIRS
    return pl.pallas_call(
        _in_proj_kernel,
        grid=(B, S // tm),
        in_specs=[
            pl.BlockSpec((1, tm, D), lambda b, i: (b, i, 0)),
            pl.BlockSpec((1, D), lambda b, i: (0, 0)),
            pl.BlockSpec((D, IN_W), lambda b, i: (0, 0)),
            pl.BlockSpec((DIFF_QW, D), lambda b, i: (0, 0)),
            pl.BlockSpec((DIFF_QW, D), lambda b, i: (0, 0)),
        ],
        out_specs=[
            pl.BlockSpec((1, DIFF_QW, tm), lambda b, i: (b, 0, i)),
            pl.BlockSpec((1, tm, DIFF_QW), lambda b, i: (b, i, 0)),
            pl.BlockSpec((1, H, 1, LANES, tm), lambda b, i: (b, 0, i, 0, 0)),
            pl.BlockSpec((1, P, tm, PAIR_W), lambda b, i: (b, 0, i, 0)),
            pl.BlockSpec((1, P, 4, tm // 4, PAIR_W), lambda b, i: (b, 0, 0, i, 0)),
            pl.BlockSpec((1, P, 16, tm // 16, PAIR_W), lambda b, i: (b, 0, 0, i, 0)),
            pl.BlockSpec((1, 1, 1, LANES), lambda b, i: (b, i, 0, 0)),
        ],
        out_shape=[
            jax.ShapeDtypeStruct((B, DIFF_QW, S), BF16),
            jax.ShapeDtypeStruct((B, S, DIFF_QW), BF16),
            jax.ShapeDtypeStruct((B, H, S // tm, LANES, tm), BF16),
            jax.ShapeDtypeStruct((B, P, S, PAIR_W), BF16),
            jax.ShapeDtypeStruct((B, P, 4, S // 4, PAIR_W), BF16),
            jax.ShapeDtypeStruct((B, P, 16, S // 16, PAIR_W), BF16),
            jax.ShapeDtypeStruct((B, S // tm, 1, LANES), F32),
        ],
        scratch_shapes=[
            pltpu.VMEM((3 * P, tm, LANES), F32),
            pltpu.VMEM((3 * P, 4, tm // 4, LANES), F32),
        ],
        compiler_params=_params(("parallel", "parallel")),
        name="in_proj",
    )(x, g_mix, w_in, wq_t, wv_t)


def _diff_attn_kernel(slopes_ref, lam_ref, g_ref, qt_ref, k_ref, vt_ref, o_ref,
                      q2t_ref, m_ref, acc_ref, cb_ref, kmax_ref, *, bq, bk, lam_init):
    G = cb_ref.shape[1]
    h = pl.program_id(1)
    qi = pl.program_id(2)
    nkv = k_ref.shape[1] // bk
    slope2 = slopes_ref[h] * LOG2E

    qt = qt_ref[0]
    sub = lax.broadcasted_iota(jnp.int32, qt.shape, 0)
    zero = jnp.zeros_like(qt)
    q2t_ref[:, 0:bq] = jnp.where(sub < HEAD_DIM, qt, zero)
    q2t_ref[:, bq:2 * bq] = jnp.where(sub >= HEAD_DIM, qt, zero)
    m_ref[...] = jnp.full(m_ref.shape, -jnp.inf, F32)
    acc_ref[...] = jnp.zeros(acc_ref.shape, F32)
    key_off = lax.broadcasted_iota(jnp.int32, (bk, G), 0).astype(F32)
    cb_ref[...] = slope2 * key_off
    lane = lax.broadcasted_iota(jnp.int32, (1, 2 * bq), 1)
    pos_q = (qi * bq + jnp.where(lane >= bq, lane - bq, lane)).astype(F32)
    ones = jnp.ones((ONES_ROWS, bk), BF16)

    @pl.when(qi == 0)
    def _():
        def norm_tile(j, best):
            kf = k_ref[0, pl.ds(pl.multiple_of(j * bk, bk), bk), :].astype(F32)
            return jnp.maximum(best, jnp.max(jnp.sum(kf * kf, axis=1, keepdims=True)))
        kmax_ref[0] = lax.fori_loop(0, nkv, norm_tile, jnp.float32(0.0))

    q2f = q2t_ref[...].astype(F32)
    q_max = jnp.max(jnp.sum(q2f * q2f, axis=0, keepdims=True))
    direct = q_max * kmax_ref[0] <= DIRECT_EXP_MAX_LOGIT ** 2

    def tile(j, side, online):
        start = pl.multiple_of(j * bk, bk)
        k_t = k_ref[0, pl.ds(start, bk), :]
        vt1 = jnp.concatenate([vt_ref[0, 0, j], ones], axis=0)
        j0 = (j * bk).astype(F32)
        for gi in range(2 * bq // G):
            sl = slice(gi * G, (gi + 1) * G)
            s = jnp.dot(k_t, q2t_ref[:, sl], preferred_element_type=F32)
            pq = pos_q[:, sl]
            if side == 0:
                u = s - slope2 * jnp.abs(pq - (j0 + key_off))
                a = jnp.zeros_like(pq)
            elif side < 0:
                u = s + cb_ref[...]
                a = slope2 * (j0 - pq)
            else:
                u = s - cb_ref[...]
                a = slope2 * (pq - j0)
            if online:
                m_old = m_ref[:, sl]
                m_new = jnp.maximum(m_old, jnp.max(u, axis=0, keepdims=True) + a)
                alpha = jnp.exp2(m_old - m_new)
                p = jnp.exp2(u - (m_new - a)).astype(BF16)
                acc_ref[:, sl] = alpha * acc_ref[:, sl] + jnp.dot(
                    vt1, p, preferred_element_type=F32)
                m_ref[:, sl] = m_new
            else:
                p = jnp.exp2(u + a).astype(BF16)
                acc_ref[:, sl] += jnp.dot(vt1, p, preferred_element_type=F32)

    def all_tiles(online):
        def before(j, carry):
            tile(j, -1, online)
            return carry

        def after(j, carry):
            tile(j, 1, online)
            return carry

        lax.fori_loop(0, qi, before, 0)
        tile(qi, 0, online)
        lax.fori_loop(qi + 1, nkv, after, 0)

    @pl.when(direct)
    def _():
        all_tiles(False)

    @pl.when(jnp.logical_not(direct))
    def _():
        all_tiles(True)

    lq = lam_ref[...]
    lam = (jnp.exp(jnp.sum(lq[0:1] * lq[1:2], axis=1, keepdims=True))
           - jnp.exp(jnp.sum(lq[2:3] * lq[3:4], axis=1, keepdims=True)) + lam_init)
    acc = acc_ref[...]
    ot = acc[0:LANES] * (1.0 / acc[LANES:LANES + 1])
    od = (ot[:, 0:bq] - lam * ot[:, bq:2 * bq]).T
    od = od * lax.rsqrt(jnp.mean(od * od, axis=-1, keepdims=True) + EPS)
    od = od * g_ref[pl.ds(h, 1), :] * (1.0 - lam_init)
    o_ref[0] = od.astype(BF16)


def _diff_attn(qt, k, vt, slopes, lam_qk, g_diff, lam_init, bq):
    B, S, _ = k.shape
    H = N_DIFF_HEADS
    nkv, bk = vt.shape[2], vt.shape[4]
    assert bq == bk
    kern = functools.partial(_diff_attn_kernel, bq=bq, bk=bk, lam_init=lam_init)
    return pl.pallas_call(
        kern,
        grid=(B, H, S // bq),
        in_specs=[
            pl.BlockSpec(memory_space=pltpu.SMEM),
            pl.BlockSpec((4, HEAD_DIM), lambda b, h, i: (0, 0)),
            pl.BlockSpec((H, LANES), lambda b, h, i: (0, 0)),
            pl.BlockSpec((1, LANES, bq), lambda b, h, i: (b, h, i)),
            pl.BlockSpec((1, S, LANES), lambda b, h, i: (b, 0, h)),
            pl.BlockSpec((1, 1, nkv, LANES, bk), lambda b, h, i: (b, h, 0, 0, 0)),
        ],
        out_specs=pl.BlockSpec((1, bq, LANES), lambda b, h, i: (b, i, h)),
        out_shape=jax.ShapeDtypeStruct((B, S, H * LANES), BF16),
        scratch_shapes=[
            pltpu.VMEM((LANES, 2 * bq), BF16),
            pltpu.VMEM((1, 2 * bq), F32),
            pltpu.VMEM((LANES + ONES_ROWS, 2 * bq), F32),
            pltpu.VMEM((bk, 8 * LANES), F32),
            pltpu.SMEM((1,), F32),
        ],
        compiler_params=_params(("parallel", "parallel", "arbitrary")),
        name="diff_attn",
    )(slopes, lam_qk, g_diff, qt, k, vt)


def _dil_bias(dil, slope_a, slope_b):
    QB, R = LANES, DIL_RADIUS
    KW = QB + 2 * R
    r_i = lax.broadcasted_iota(jnp.int32, (QB, KW), 0)
    c_i = lax.broadcasted_iota(jnp.int32, (QB, KW), 1)
    rel = c_i - r_i - R
    band = (rel >= -R) & (rel <= R)
    dist = (jnp.abs(rel) * dil).astype(F32)
    return jnp.concatenate([jnp.where(band, -(slope_a * LOG2E) * dist, NEG),
                            jnp.where(band, -(slope_b * LOG2E) * dist, NEG)], axis=0)


def _dil_segment(q_at, k_src, v_src, L, dil, tok0, bias2, first, last, online,
                 acc_ref, l_ref, m_ref, o_ref, kpad, vpad):
    QB, R = LANES, DIL_RADIUS
    KW = QB + 2 * R
    unroll = 1 if online else min(4, L // QB)
    assert L % (QB * unroll) == 0
    zpad = jnp.zeros((R, LANES), BF16)
    kpad[0:R, :] = zpad
    kpad[R:R + L, :] = k_src
    kpad[R + L:R + L + R, :] = zpad
    vpad[0:R, :] = jnp.zeros((R, 2 * LANES), BF16)
    vpad[R:R + L, 0:LANES] = v_src
    vpad[R:R + L, LANES:2 * LANES] = jnp.ones((L, LANES), BF16)
    vpad[R + L:R + L + R, :] = jnp.zeros((R, 2 * LANES), BF16)

    lane_q = lax.broadcasted_iota(jnp.int32, (QB, LANES), 1)
    head_a = lane_q < HEAD_DIM
    c2 = lax.broadcasted_iota(jnp.int32, (2 * QB, KW), 1)

    def rows(i0):
        if dil == 1:
            return pl.ds(i0, QB)
        return pl.ds(tok0 + dil * i0, QB, stride=dil)

    def scores(i0):
        q = q_at(i0)
        zero = jnp.zeros_like(q)
        q2 = jnp.concatenate([jnp.where(head_a, q, zero), jnp.where(head_a, zero, q)], axis=0)
        s = lax.dot_general(q2, kpad[pl.ds(i0, KW), :], (((1,), (1,)), ((), ())),
                            preferred_element_type=F32)
        return s + bias2

    def finish(i0, s):
        if online:
            key = i0 - R + c2
            s = jnp.where((key >= 0) & (key < L), s, NEG)
            m_c = jnp.max(s, axis=1, keepdims=True)
            s = s - m_c
        pv = jnp.dot(jnp.exp2(s).astype(BF16), vpad[pl.ds(i0, KW), :],
                     preferred_element_type=F32)
        acc_n = jnp.where(head_a, pv[0:QB, 0:LANES], pv[QB:2 * QB, 0:LANES])
        l_n = jnp.where(head_a, pv[0:QB, LANES:2 * LANES], pv[QB:2 * QB, LANES:2 * LANES])
        r = rows(i0)
        if online:
            m_n = jnp.where(head_a, jnp.broadcast_to(m_c[0:QB], (QB, LANES)),
                            jnp.broadcast_to(m_c[QB:2 * QB], (QB, LANES)))
            if not first:
                m_p = m_ref[r, :]
                m_c_rep, m_n = m_n, jnp.maximum(m_p, m_n)
                e_p, e_c = jnp.exp2(m_p - m_n), jnp.exp2(m_c_rep - m_n)
                acc_n = acc_ref[r, :] * e_p + acc_n * e_c
                l_n = l_ref[r, :] * e_p + l_n * e_c
            if not last:
                m_ref[r, :] = m_n
        elif not first:
            acc_n = acc_ref[r, :] + acc_n
            l_n = l_ref[r, :] + l_n
        if last:
            o_ref[0, 0, r, :] = acc_n / l_n
        else:
            acc_ref[r, :] = acc_n
            l_ref[r, :] = l_n

    def body(n, carry):
        starts = [pl.multiple_of((n * unroll + u) * QB, QB) for u in range(unroll)]
        ss = [scores(i0) for i0 in starts]
        for i0, s in zip(starts, ss):
            finish(i0, s)
        return carry

    lax.fori_loop(0, L // (QB * unroll), body, 0)


def _dil_kernel(slopes_ref, nrm_ref, pb1_ref, pb4_ref, pb16_ref, o_ref,
                     acc_ref, l_ref, m_ref, kpad, vpad, *, group16):
    hp = pl.program_id(1)
    t = pl.program_id(2)
    S = pb1_ref.shape[2]
    n16 = 16 // group16
    slope_a = slopes_ref[2 * hp]
    slope_b = slopes_ref[2 * hp + 1]

    big = jnp.max(nrm_ref[0, :, 0, :], axis=0, keepdims=True)
    lane = lax.broadcasted_iota(jnp.int32, big.shape, 1)
    def pick(idx):
        return jnp.max(jnp.where(lane == idx, big, 0.0))

    direct = ((pick(4 * hp) * pick(4 * hp + 1) <= DIRECT_EXP_MAX_LOGIT ** 2)
              & (pick(4 * hp + 2) * pick(4 * hp + 3) <= DIRECT_EXP_MAX_LOGIT ** 2))

    def pattern(online):
        args = (acc_ref, l_ref, m_ref, o_ref, kpad, vpad)

        @pl.when(t < n16)
        def _():
            bias2 = _dil_bias(16, slope_a, slope_b)

            def residue(r, carry):
                _dil_segment(lambda i0: pb16_ref[0, 0, r, pl.ds(i0, LANES), 0:LANES],
                             pb16_ref[0, 0, r, :, LANES:2 * LANES],
                             pb16_ref[0, 0, r, :, 2 * LANES:3 * LANES],
                             S // 16, 16, t * group16 + r, bias2, True, False, online, *args)
                return carry

            lax.fori_loop(0, group16, residue, 0)

        @pl.when((t >= n16) & (t < n16 + 4))
        def _():
            _dil_segment(lambda i0: pb4_ref[0, 0, 0, pl.ds(i0, LANES), 0:LANES],
                         pb4_ref[0, 0, 0, :, LANES:2 * LANES],
                         pb4_ref[0, 0, 0, :, 2 * LANES:3 * LANES],
                         S // 4, 4, t - n16, _dil_bias(4, slope_a, slope_b),
                         False, False, online, *args)

        @pl.when(t == n16 + 4)
        def _():
            _dil_segment(lambda i0: pb1_ref[0, 0, pl.ds(i0, LANES), 0:LANES],
                         pb1_ref[0, 0, :, LANES:2 * LANES],
                         pb1_ref[0, 0, :, 2 * LANES:3 * LANES],
                         S, 1, 0, _dil_bias(1, slope_a, slope_b),
                         False, True, online, *args)

    @pl.when(direct)
    def _():
        pattern(False)

    @pl.when(jnp.logical_not(direct))
    def _():
        pattern(True)


def _dil_attn(pb1, pb4, pb16, nrm, slopes, group16=4):
    B, P, S, _ = pb1.shape
    n16 = 16 // group16
    nt = nrm.shape[1]
    return pl.pallas_call(
        functools.partial(_dil_kernel, group16=group16),
        grid=(B, P, n16 + 4 + 1),
        in_specs=[
            pl.BlockSpec(memory_space=pltpu.SMEM),
            pl.BlockSpec((1, nt, 1, LANES), lambda b, p, t: (b, 0, 0, 0)),
            pl.BlockSpec((1, 1, S, PAIR_W), lambda b, p, t: (b, p, 0, 0)),
            pl.BlockSpec((1, 1, 1, S // 4, PAIR_W),
                         lambda b, p, t: (b, p, jnp.clip(t - n16, 0, 3), 0, 0)),
            pl.BlockSpec((1, 1, group16, S // 16, PAIR_W),
                         lambda b, p, t: (b, p, jnp.minimum(t, n16 - 1), 0, 0)),
        ],
        out_specs=pl.BlockSpec((1, 1, S, LANES), lambda b, p, t: (b, p, 0, 0)),
        out_shape=jax.ShapeDtypeStruct((B, P, S, LANES), F32),
        scratch_shapes=[
            pltpu.VMEM((S, LANES), F32),
            pltpu.VMEM((S, LANES), F32),
            pltpu.VMEM((S, LANES), F32),
            pltpu.VMEM((S + 2 * DIL_RADIUS, LANES), BF16),
            pltpu.VMEM((S + 2 * DIL_RADIUS, 2 * LANES), BF16),
        ],
        compiler_params=_params(("parallel", "parallel", "arbitrary")),
        name="dil_attn",
    )(slopes, nrm, pb1, pb4, pb16)


def _lane_col(tile, lane, idx):
    return jnp.sum(jnp.where(lane == idx, tile, 0.0), axis=1, keepdims=True)


def _dil_attn_kernel(*refs, dil, first, last):
    QB = LANES
    R = DIL_RADIUS
    KW = QB + 2 * R
    it = iter(refs)
    slopes_ref = next(it)
    q_ref, k_ref, v_ref = next(it), next(it), next(it)
    acc_in = st_in = acc_out = st_out = o_out = None
    if not first:
        acc_in, st_in = next(it), next(it)
    if last:
        o_out = next(it)
    else:
        acc_out, st_out = next(it), next(it)
    kpad, vpad = next(it), next(it)

    hp = pl.program_id(1)
    g = pl.program_id(2)
    L = q_ref.shape[2]
    slope_a = slopes_ref[2 * hp]
    slope_b = slopes_ref[2 * hp + 1]

    zpad = jnp.zeros((R, LANES), BF16)
    for pad, src in ((kpad, k_ref), (vpad, v_ref)):
        pad[0:R, :] = zpad
        pad[R:R + L, :] = src[0, 0]
        pad[R + L:R + L + R, :] = zpad

    r_i = lax.broadcasted_iota(jnp.int32, (QB, KW), 0)
    c_i = lax.broadcasted_iota(jnp.int32, (QB, KW), 1)
    rel = c_i - r_i - R
    band = (rel >= -R) & (rel <= R)
    dist = (jnp.abs(rel) * dil).astype(F32)
    bias2 = jnp.concatenate(
        [jnp.where(band, -slope_a * dist, NEG), jnp.where(band, -slope_b * dist, NEG)], axis=0)
    c2 = lax.broadcasted_iota(jnp.int32, (2 * QB, KW), 1)
    lane_q = lax.broadcasted_iota(jnp.int32, (QB, LANES), 1)
    head_a = lane_q < HEAD_DIM
    if st_in is not None or st_out is not None:
        st_w = (st_in if st_in is not None else st_out).shape[3]
        lane_s = lax.broadcasted_iota(jnp.int32, (QB, st_w), 1)

    if not first and not last:
        @pl.when(g == 0)
        def _():
            st_out[...] = st_in[...]

    def body(n, carry):
        i0 = pl.multiple_of(n * QB, QB)
        q = q_ref[0, 0, pl.ds(i0, QB), :]
        zero = jnp.zeros_like(q)
        q2 = jnp.concatenate([jnp.where(head_a, q, zero), jnp.where(head_a, zero, q)], axis=0)
        kw = kpad[pl.ds(i0, KW), :]
        vw = vpad[pl.ds(i0, KW), :]
        s = lax.dot_general(q2, kw, (((1,), (1,)), ((), ())), preferred_element_type=F32)
        s = s + bias2
        key = i0 - R + c2
        s = jnp.where((key >= 0) & (key < L), s, NEG)
        m_c = jnp.max(s, axis=1, keepdims=True)
        p = jnp.exp(s - m_c)
        l_c = jnp.sum(p, axis=1, keepdims=True)
        pv = jnp.dot(p.astype(BF16), vw, preferred_element_type=F32)
        acc_c = jnp.where(head_a, pv[0:QB], pv[QB:2 * QB])
        ma, mb = m_c[0:QB], m_c[QB:2 * QB]
        la, lb = l_c[0:QB], l_c[QB:2 * QB]
        if first:
            acc_n, ma_n, mb_n, la_n, lb_n = acc_c, ma, mb, la, lb
        else:
            st = st_in[0, 0, pl.ds(i0, QB), :]
            pma, pla = _lane_col(st, lane_s, 4 * g), _lane_col(st, lane_s, 4 * g + 1)
            pmb, plb = _lane_col(st, lane_s, 4 * g + 2), _lane_col(st, lane_s, 4 * g + 3)
            ma_n, mb_n = jnp.maximum(pma, ma), jnp.maximum(pmb, mb)
            ea_p, ea_c = jnp.exp(pma - ma_n), jnp.exp(ma - ma_n)
            eb_p, eb_c = jnp.exp(pmb - mb_n), jnp.exp(mb - mb_n)
            acc_p = acc_in[0, 0, pl.ds(i0, QB), :]
            acc_n = (acc_p * jnp.where(head_a, ea_p, eb_p)
                     + acc_c * jnp.where(head_a, ea_c, eb_c))
            la_n = pla * ea_p + la * ea_c
            lb_n = plb * eb_p + lb * eb_c
        if last:
            o_out[0, 0, pl.ds(i0, QB), :] = acc_n / jnp.where(head_a, la_n, lb_n)
        else:
            acc_out[0, 0, pl.ds(i0, QB), :] = acc_n
            if first:
                st_out[0, 0, pl.ds(i0, QB), :] = jnp.concatenate([ma_n, la_n, mb_n, lb_n], axis=1)
            else:
                t = st_out[0, 0, pl.ds(i0, QB), :]
                t = jnp.where(lane_s == 4 * g, ma_n, t)
                t = jnp.where(lane_s == 4 * g + 1, la_n, t)
                t = jnp.where(lane_s == 4 * g + 2, mb_n, t)
                t = jnp.where(lane_s == 4 * g + 3, lb_n, t)
                st_out[0, 0, pl.ds(i0, QB), :] = t
        return carry

    lax.fori_loop(0, L // LANES, body, 0)


def _dil_pass(pb, slopes, state, dil, first, last):
    B, P, S, _ = pb.shape
    L = S // dil
    pbv = pb.reshape(B, P, L, dil * PAIR_W)

    def blk(c):
        return pl.BlockSpec((1, 1, L, LANES), lambda b, p, g: (b, p, 0, 3 * g + c))

    acc_spec = pl.BlockSpec((1, 1, L, LANES), lambda b, p, g: (b, p, 0, g))
    st_spec = pl.BlockSpec((1, 1, L, 4 * dil), lambda b, p, g: (b, p, 0, 0))
    acc_shape = jax.ShapeDtypeStruct((B, P, L, dil * LANES), F32)
    st_shape = jax.ShapeDtypeStruct((B, P, L, 4 * dil), F32)
    in_specs = [pl.BlockSpec(memory_space=pltpu.SMEM), blk(0), blk(1), blk(2)]
    args = [slopes, pbv, pbv, pbv]
    if not first:
        acc, st = state
        in_specs += [acc_spec, st_spec]
        args += [acc.reshape(B, P, L, dil * LANES), st.reshape(B, P, L, 4 * dil)]
    if last:
        out_specs, out_shape = acc_spec, acc_shape
    else:
        out_specs, out_shape = [acc_spec, st_spec], [acc_shape, st_shape]
    out = pl.pallas_call(
        functools.partial(_dil_attn_kernel, dil=dil, first=first, last=last),
        grid=(B, P, dil),
        in_specs=in_specs,
        out_specs=out_specs,
        out_shape=out_shape,
        scratch_shapes=[pltpu.VMEM((L + 2 * DIL_RADIUS, LANES), BF16)] * 2,
        compiler_params=_params(("parallel", "parallel", "arbitrary")),
        name=f"dil_attn_d{dil}",
    )(*args)
    if last:
        return out.reshape(B, P, S, LANES)
    acc, st = out
    return acc.reshape(B, P, S, LANES), st.reshape(B, P, S, 4)


def _out_mlp_kernel(x_ref, oa_ref, ob_ref, gdil_ref, wout_ref, gmlp_ref, wup_ref,
                    wdown_ref, gfin_ref, y_ref, *, ff_chunk):
    ob = jnp.concatenate([ob_ref[0, p] for p in range(N_DIL_PAIRS)], axis=-1)
    ob = _rms(ob, gdil_ref[...])
    mix = jnp.concatenate([oa_ref[0], ob.astype(BF16)], axis=-1)
    x1 = x_ref[0] + jnp.dot(mix, wout_ref[...], preferred_element_type=F32)
    h = _rms(x1, gmlp_ref[...]).astype(BF16)
    y = x1
    for c in range(D_FF // ff_chunk):
        u = jnp.dot(h, wup_ref[:, c * ff_chunk:(c + 1) * ff_chunk], preferred_element_type=F32)
        u = jnp.square(jnp.maximum(u, 0.0)).astype(BF16)
        y = y + jnp.dot(u, wdown_ref[c * ff_chunk:(c + 1) * ff_chunk, :],
                        preferred_element_type=F32)
    y_ref[0] = _rms(y, gfin_ref[...])


def _out_mlp(x, oa, ob, g_dil, w_out, g_mlp, w_up, w_down, g_final, tm, ff_chunk):
    B, S, D = x.shape
    const = lambda shape: pl.BlockSpec(shape, lambda b, i: (0,) * len(shape),
                                       pipeline_mode=pl.Buffered(1))
    return pl.pallas_call(
        functools.partial(_out_mlp_kernel, ff_chunk=ff_chunk),
        grid=(B, S // tm),
        in_specs=[
            pl.BlockSpec((1, tm, D), lambda b, i: (b, i, 0)),
            pl.BlockSpec((1, tm, N_DIFF_HEADS * LANES), lambda b, i: (b, i, 0)),
            pl.BlockSpec((1, N_DIL_PAIRS, tm, LANES), lambda b, i: (b, 0, i, 0)),
            const((1, DIL_W)),
            const((2 * DIL_W, D)),
            const((1, D)),
            const((D, D_FF)),
            const((D_FF, D)),
            const((1, D)),
        ],
        out_specs=pl.BlockSpec((1, tm, D), lambda b, i: (b, i, 0)),
        out_shape=jax.ShapeDtypeStruct((B, S, D), F32),
        compiler_params=_params(("parallel", "parallel")),
        name="out_mlp",
    )(x, oa, ob, g_dil, w_out, g_mlp, w_up, w_down, g_final)


def _alibi_slopes(n):
    return 2.0 ** (-8.0 * jnp.arange(1, n + 1, dtype=F32) / n)


def _layer(x, g_mix, w_in, wq_t, wv_t, lam_qk, g_diff, g_dil, w_out, g_mlp, w_up, w_down,
           g_final, lam_init):
    qt, k, vt, pb1, pb4, pb16, nrm = _in_proj(x, g_mix, w_in, wq_t, wv_t, tm=512)
    oa = _diff_attn(qt, k, vt, _alibi_slopes(N_DIFF_HEADS), lam_qk, g_diff, lam_init, bq=512)
    assert DIL_PATTERNS == ((2 * DIL_RADIUS, 1), (8 * DIL_RADIUS, 4), (32 * DIL_RADIUS, 16))
    ob = _dil_attn(pb1, pb4, pb16, nrm, _alibi_slopes(N_DIL_HEADS))
    return _out_mlp(x, oa, ob, g_dil, w_out, g_mlp, w_up, w_down, g_final,
                    tm=512, ff_chunk=1024)


def kernel(x_prompt, x_sample, g_mix, w_in, lam_qk, g_diff, g_dil, w_out, g_mlp, w_up,
           w_down, g_final):
    assert g_mix.shape[0] == 1
    lam_init = 0.8 - 0.6 * math.exp(-0.3 * 0)
    w_in_b = w_in[0].astype(BF16)
    weights = (
        g_mix[0][None, :], w_in_b, w_in_b[:, 0:DIFF_QW].T, w_in_b[:, 2 * DIFF_QW:DIFF_W].T,
        lam_qk[0],
        g_diff[0], g_dil[0][None, :], w_out[0].astype(BF16), g_mlp[0][None, :],
        w_up[0].astype(BF16), w_down[0].astype(BF16), g_final[None, :],
    )
    return tuple(_layer(x, *weights, lam_init) for x in (x_prompt, x_sample))
```

```python
import functools
import math

import jax
import jax.numpy as jnp
from jax import lax
from jax.experimental import pallas as pl
from jax.experimental.pallas import tpu as pltpu

D_MODEL = 1024
HEAD_DIM = 64
N_DIFF_HEADS = 4
N_DIL_HEADS = 8
N_DIL_PAIRS = N_DIL_HEADS // 2
DIL_PATTERNS = ((128, 1), (512, 4), (2048, 16))
DIL_RADIUS = 64
D_FF = 4 * D_MODEL
EPS = 1e-5
DIFF_QW = N_DIFF_HEADS * 2 * HEAD_DIM
DIFF_W = 3 * DIFF_QW
DIL_W = N_DIL_HEADS * HEAD_DIM
IN_W = DIFF_W + 3 * DIL_W
LANES = 128
PAIR_W = 3 * LANES
NEG = -1e30
LOG2E = math.log2(math.e)
ONES_ROWS = 16
DIRECT_EXP_MAX_LOGIT = 60.0
VMEM_LIMIT = 56 * 1024 * 1024

BF16 = jnp.bfloat16
F32 = jnp.float32


def _rms(x, g):
    return x * lax.rsqrt(jnp.mean(x * x, axis=-1, keepdims=True) + EPS) * g


def _params(sem):
    return pltpu.CompilerParams(dimension_semantics=sem, vmem_limit_bytes=VMEM_LIMIT)


def _in_proj_kernel(x_ref, g_ref, w_ref, wqt_ref, wvt_ref, qt_ref, k_ref, vt_ref,
                    pb1_ref, pb4_ref, pb16_ref, nrm_ref, slab_ref, slab4_ref):
    scale = HEAD_DIM ** -0.5
    nt = (((1,), (1,)), ((), ()))
    tm = x_ref.shape[1]
    h = _rms(x_ref[0], g_ref[...]).astype(BF16)
    qt = lax.dot_general(wqt_ref[...], h, nt, preferred_element_type=F32)
    qt_ref[0] = (qt * (scale * LOG2E)).astype(BF16)
    k_ref[0] = jnp.dot(h, w_ref[:, DIFF_QW:2 * DIFF_QW], preferred_element_type=F32).astype(BF16)
    vt = lax.dot_general(wvt_ref[...], h, nt, preferred_element_type=F32).astype(BF16)
    for hh in range(N_DIFF_HEADS):
        vt_ref[0, hh, 0] = vt[hh * LANES:(hh + 1) * LANES, :]
    lane = lax.broadcasted_iota(jnp.int32, (1, LANES), 1)
    nrm = jnp.zeros((1, LANES), F32)
    for c in range(3):
        p = jnp.dot(h, w_ref[:, DIFF_W + c * DIL_W:DIFF_W + (c + 1) * DIL_W],
                    preferred_element_type=F32)
        if c == 0:
            p = p * (scale * LOG2E)
        cols = slice(c * LANES, (c + 1) * LANES)
        for hp in range(N_DIL_PAIRS):
            ph = p[:, hp * LANES:(hp + 1) * LANES]
            if c < 2:
                sq = ph * ph
                for e in range(2):
                    mine = (lane >= e * HEAD_DIM) & (lane < (e + 1) * HEAD_DIM)
                    big = jnp.max(jnp.sum(jnp.where(mine, sq, 0.0), axis=1, keepdims=True),
                                  axis=0, keepdims=True)
                    nrm = jnp.where(lane == 4 * hp + 2 * e + c, big, nrm)
            pb1_ref[0, hp, :, cols] = ph.astype(BF16)
            s = c * N_DIL_PAIRS + hp
            slab_ref[s] = ph
            for g in range(4):
                v4 = slab_ref[s, pl.ds(g, tm // 4, stride=4), :]
                pb4_ref[0, hp, g, :, cols] = v4.astype(BF16)
                slab4_ref[s, g] = v4
            for g in range(4):
                for c2 in range(4):
                    v16 = slab4_ref[s, g, pl.ds(c2, tm // 16, stride=4), :]
                    pb16_ref[0, hp, 4 * c2 + g, :, cols] = v16.astype(BF16)
    nrm_ref[0, 0] = nrm


def _in_proj(x, g_mix, w_in, wq_t, wv_t, tm):
    B, S, D = x.shape
    H = N_DIFF_HEADS
    P = N_DIL_PAIRS
    return pl.pallas_call(
        _in_proj_kernel,
        grid=(B, S // tm),
        in_specs=[
            pl.BlockSpec((1, tm, D), lambda b, i: (b, i, 0)),
            pl.BlockSpec((1, D), lambda b, i: (0, 0)),
            pl.BlockSpec((D, IN_W), lambda b, i: (0, 0)),
            pl.BlockSpec((DIFF_QW, D), lambda b, i: (0, 0)),
            pl.BlockSpec((DIFF_QW, D), lambda b, i: (0, 0)),
        ],
        out_specs=[
            pl.BlockSpec((1, DIFF_QW, tm), lambda b, i: (b, 0, i)),
            pl.BlockSpec((1, tm, DIFF_QW), lambda b, i: (b, i, 0)),
            pl.BlockSpec((1, H, 1, LANES, tm), lambda b, i: (b, 0, i, 0, 0)),
            pl.BlockSpec((1, P, tm, PAIR_W), lambda b, i: (b, 0, i, 0)),
            pl.BlockSpec((1, P, 4, tm // 4, PAIR_W), lambda b, i: (b, 0, 0, i, 0)),
            pl.BlockSpec((1, P, 16, tm // 16, PAIR_W), lambda b, i: (b, 0, 0, i, 0)),
            pl.BlockSpec((1, 1, 1, LANES), lambda b, i: (b, i, 0, 0)),
        ],
        out_shape=[
            jax.ShapeDtypeStruct((B, DIFF_QW, S), BF16),
            jax.ShapeDtypeStruct((B, S, DIFF_QW), BF16),
            jax.ShapeDtypeStruct((B, H, S // tm, LANES, tm), BF16),
            jax.ShapeDtypeStruct((B, P, S, PAIR_W), BF16),
            jax.ShapeDtypeStruct((B, P, 4, S // 4, PAIR_W), BF16),
            jax.ShapeDtypeStruct((B, P, 16, S // 16, PAIR_W), BF16),
            jax.ShapeDtypeStruct((B, S // tm, 1, LANES), F32),
        ],
        scratch_shapes=[
            pltpu.VMEM((3 * P, tm, LANES), F32),
            pltpu.VMEM((3 * P, 4, tm // 4, LANES), F32),
        ],
        compiler_params=_params(("parallel", "parallel")),
        name="in_proj",
    )(x, g_mix, w_in, wq_t, wv_t)


def _diff_attn_kernel(slopes_ref, lam_ref, g_ref, qt_ref, k_ref, vt_ref, o_ref,
                      q2t_ref, m_ref, acc_ref, cb_ref, kmax_ref, *, bq, bk, lam_init):
    G = cb_ref.shape[1]
    h = pl.program_id(1)
    qi = pl.program_id(2)
    nkv = k_ref.shape[1] // bk
    slope2 = slopes_ref[h] * LOG2E

    qt = qt_ref[0]
    sub = lax.broadcasted_iota(jnp.int32, qt.shape, 0)
    zero = jnp.zeros_like(qt)
    q2t_ref[:, 0:bq] = jnp.where(sub < HEAD_DIM, qt, zero)
    q2t_ref[:, bq:2 * bq] = jnp.where(sub >= HEAD_DIM, qt, zero)
    m_ref[...] = jnp.full(m_ref.shape, -jnp.inf, F32)
    acc_ref[...] = jnp.zeros(acc_ref.shape, F32)
    key_off = lax.broadcasted_iota(jnp.int32, (bk, G), 0).astype(F32)
    cb_ref[...] = slope2 * key_off
    lane = lax.broadcasted_iota(jnp.int32, (1, 2 * bq), 1)
    pos_q = (qi * bq + jnp.where(lane >= bq, lane - bq, lane)).astype(F32)
    ones = jnp.ones((ONES_ROWS, bk), BF16)

    @pl.when(qi == 0)
    def _():
        def norm_tile(j, best):
            kf = k_ref[0, pl.ds(pl.multiple_of(j * bk, bk), bk), :].astype(F32)
            return jnp.maximum(best, jnp.max(jnp.sum(kf * kf, axis=1, keepdims=True)))
        kmax_ref[0] = lax.fori_loop(0, nkv, norm_tile, jnp.float32(0.0))

    q2f = q2t_ref[...].astype(F32)
    q_max = jnp.max(jnp.sum(q2f * q2f, axis=0, keepdims=True))
    direct = q_max * kmax_ref[0] <= DIRECT_EXP_MAX_LOGIT ** 2

    def tile(j, side, online):
        start = pl.multiple_of(j * bk, bk)
        k_t = k_ref[0, pl.ds(start, bk), :]
        vt1 = jnp.concatenate([vt_ref[0, 0, j], ones], axis=0)
        j0 = (j * bk).astype(F32)
        groups = [slice(gi * G, (gi + 1) * G) for gi in range(2 * bq // G)]
        scores = [jnp.dot(k_t, q2t_ref[:, sl], preferred_element_type=F32) for sl in groups]
        for sl, s in zip(groups, scores):
            pq = pos_q[:, sl]
            if side == 0:
                u = s - slope2 * jnp.abs(pq - (j0 + key_off))
                a = jnp.zeros_like(pq)
            elif side < 0:
                u = s + cb_ref[...]
                a = slope2 * (j0 - pq)
            else:
                u = s - cb_ref[...]
                a = slope2 * (pq - j0)
            if online:
                m_old = m_ref[:, sl]
                m_new = jnp.maximum(m_old, jnp.max(u, axis=0, keepdims=True) + a)
                alpha = jnp.exp2(m_old - m_new)
                p = jnp.exp2(u - (m_new - a)).astype(BF16)
                acc_ref[:, sl] = alpha * acc_ref[:, sl] + jnp.dot(
                    vt1, p, preferred_element_type=F32)
                m_ref[:, sl] = m_new
            else:
                p = jnp.exp2(u + a).astype(BF16)
                acc_ref[:, sl] += jnp.dot(vt1, p, preferred_element_type=F32)

    def all_tiles(online):
        def before(j, carry):
            tile(j, -1, online)
            return carry

        def after(j, carry):
            tile(j, 1, online)
            return carry

        ndiag = bq // bk
        lax.fori_loop(0, qi * ndiag, before, 0)
        for d in range(ndiag):
            tile(qi * ndiag + d, 0, online)
        lax.fori_loop((qi + 1) * ndiag, nkv, after, 0)

    @pl.when(direct)
    def _():
        all_tiles(False)

    @pl.when(jnp.logical_not(direct))
    def _():
        all_tiles(True)

    lq = lam_ref[...]
    lam = (jnp.exp(jnp.sum(lq[0:1] * lq[1:2], axis=1, keepdims=True))
           - jnp.exp(jnp.sum(lq[2:3] * lq[3:4], axis=1, keepdims=True)) + lam_init)
    acc = acc_ref[...]
    ot = acc[0:LANES] * (1.0 / acc[LANES:LANES + 1])
    od = (ot[:, 0:bq] - lam * ot[:, bq:2 * bq]).T
    od = od * lax.rsqrt(jnp.mean(od * od, axis=-1, keepdims=True) + EPS)
    od = od * g_ref[pl.ds(h, 1), :] * (1.0 - lam_init)
    o_ref[0] = od.astype(BF16)


def _diff_attn(qt, k, vt, slopes, lam_qk, g_diff, lam_init, bq):
    B, S, _ = k.shape
    H = N_DIFF_HEADS
    nkv, bk = vt.shape[2], vt.shape[4]
    assert bq % bk == 0
    kern = functools.partial(_diff_attn_kernel, bq=bq, bk=bk, lam_init=lam_init)
    return pl.pallas_call(
        kern,
        grid=(B, H, S // bq),
        in_specs=[
            pl.BlockSpec(memory_space=pltpu.SMEM),
            pl.BlockSpec((4, HEAD_DIM), lambda b, h, i: (0, 0)),
            pl.BlockSpec((H, LANES), lambda b, h, i: (0, 0)),
            pl.BlockSpec((1, LANES, bq), lambda b, h, i: (b, h, i)),
            pl.BlockSpec((1, S, LANES), lambda b, h, i: (b, 0, h)),
            pl.BlockSpec((1, 1, nkv, LANES, bk), lambda b, h, i: (b, h, 0, 0, 0)),
        ],
        out_specs=pl.BlockSpec((1, bq, LANES), lambda b, h, i: (b, i, h)),
        out_shape=jax.ShapeDtypeStruct((B, S, H * LANES), BF16),
        scratch_shapes=[
            pltpu.VMEM((LANES, 2 * bq), BF16),
            pltpu.VMEM((1, 2 * bq), F32),
            pltpu.VMEM((LANES + ONES_ROWS, 2 * bq), F32),
            pltpu.VMEM((bk, 8 * LANES), F32),
            pltpu.SMEM((1,), F32),
        ],
        compiler_params=_params(("parallel", "parallel", "arbitrary")),
        name="diff_attn",
    )(slopes, lam_qk, g_diff, qt, k, vt)


def _dil_bias(dil, slope_a, slope_b):
    QB, R = LANES, DIL_RADIUS
    KW = QB + 2 * R
    r_i = lax.broadcasted_iota(jnp.int32, (QB, KW), 0)
    c_i = lax.broadcasted_iota(jnp.int32, (QB, KW), 1)
    rel = c_i - r_i - R
    band = (rel >= -R) & (rel <= R)
    dist = (jnp.abs(rel) * dil).astype(F32)
    return jnp.concatenate([jnp.where(band, -(slope_a * LOG2E) * dist, NEG),
                            jnp.where(band, -(slope_b * LOG2E) * dist, NEG)], axis=0)


def _dil_segment(q_at, k_src, v_src, L, dil, tok0, bias2, first, last, online,
                 acc_ref, l_ref, m_ref, o_ref, kpad, vpad):
    QB, R = LANES, DIL_RADIUS
    KW = QB + 2 * R
    unroll = 1 if online else min(4, L // QB)
    assert L % (QB * unroll) == 0
    zpad = jnp.zeros((R, LANES), BF16)
    kpad[0:R, :] = zpad
    kpad[R:R + L, :] = k_src
    kpad[R + L:R + L + R, :] = zpad
    vpad[0:R, :] = jnp.zeros((R, 2 * LANES), BF16)
    vpad[R:R + L, 0:LANES] = v_src
    vpad[R:R + L, LANES:2 * LANES] = jnp.ones((L, LANES), BF16)
    vpad[R + L:R + L + R, :] = jnp.zeros((R, 2 * LANES), BF16)

    lane_q = lax.broadcasted_iota(jnp.int32, (QB, LANES), 1)
    head_a = lane_q < HEAD_DIM
    c2 = lax.broadcasted_iota(jnp.int32, (2 * QB, KW), 1)

    def rows(i0):
        if dil == 1:
            return pl.ds(i0, QB)
        return pl.ds(tok0 + dil * i0, QB, stride=dil)

    def scores(i0):
        q = q_at(i0)
        zero = jnp.zeros_like(q)
        q2 = jnp.concatenate([jnp.where(head_a, q, zero), jnp.where(head_a, zero, q)], axis=0)
        s = lax.dot_general(q2, kpad[pl.ds(i0, KW), :], (((1,), (1,)), ((), ())),
                            preferred_element_type=F32)
        return s + bias2

    def finish(i0, s):
        if online:
            key = i0 - R + c2
            s = jnp.where((key >= 0) & (key < L), s, NEG)
            m_c = jnp.max(s, axis=1, keepdims=True)
            s = s - m_c
        pv = jnp.dot(jnp.exp2(s).astype(BF16), vpad[pl.ds(i0, KW), :],
                     preferred_element_type=F32)
        acc_n = jnp.where(head_a, pv[0:QB, 0:LANES], pv[QB:2 * QB, 0:LANES])
        l_n = jnp.where(head_a, pv[0:QB, LANES:2 * LANES], pv[QB:2 * QB, LANES:2 * LANES])
        r = rows(i0)
        if online:
            m_n = jnp.where(head_a, jnp.broadcast_to(m_c[0:QB], (QB, LANES)),
                            jnp.broadcast_to(m_c[QB:2 * QB], (QB, LANES)))
            if not first:
                m_p = m_ref[r, :]
                m_c_rep, m_n = m_n, jnp.maximum(m_p, m_n)
                e_p, e_c = jnp.exp2(m_p - m_n), jnp.exp2(m_c_rep - m_n)
                acc_n = acc_ref[r, :] * e_p + acc_n * e_c
                l_n = l_ref[r, :] * e_p + l_n * e_c
            if not last:
                m_ref[r, :] = m_n
        elif not first:
            acc_n = acc_ref[r, :] + acc_n
            l_n = l_ref[r, :] + l_n
        if last:
            o_ref[0, 0, r, :] = acc_n / l_n
        else:
            acc_ref[r, :] = acc_n
            l_ref[r, :] = l_n

    def body(n, carry):
        starts = [pl.multiple_of((n * unroll + u) * QB, QB) for u in range(unroll)]
        ss = [scores(i0) for i0 in starts]
        for i0, s in zip(starts, ss):
            finish(i0, s)
        return carry

    lax.fori_loop(0, L // (QB * unroll), body, 0)


def _dil_kernel(slopes_ref, nrm_ref, pb1_ref, pb4_ref, pb16_ref, o_ref,
                     acc_ref, l_ref, m_ref, kpad, vpad, *, group16):
    hp = pl.program_id(1)
    t = pl.program_id(2)
    S = pb1_ref.shape[2]
    n16 = 16 // group16
    slope_a = slopes_ref[2 * hp]
    slope_b = slopes_ref[2 * hp + 1]

    big = jnp.max(nrm_ref[0, :, 0, :], axis=0, keepdims=True)
    lane = lax.broadcasted_iota(jnp.int32, big.shape, 1)
    def pick(idx):
        return jnp.max(jnp.where(lane == idx, big, 0.0))

    direct = ((pick(4 * hp) * pick(4 * hp + 1) <= DIRECT_EXP_MAX_LOGIT ** 2)
              & (pick(4 * hp + 2) * pick(4 * hp + 3) <= DIRECT_EXP_MAX_LOGIT ** 2))

    def pattern(online):
        args = (acc_ref, l_ref, m_ref, o_ref, kpad, vpad)

        @pl.when(t < n16)
        def _():
            bias2 = _dil_bias(16, slope_a, slope_b)

            def residue(r, carry):
                _dil_segment(lambda i0: pb16_ref[0, 0, r, pl.ds(i0, LANES), 0:LANES],
                             pb16_ref[0, 0, r, :, LANES:2 * LANES],
                             pb16_ref[0, 0, r, :, 2 * LANES:3 * LANES],
                             S // 16, 16, t * group16 + r, bias2, True, False, online, *args)
                return carry

            lax.fori_loop(0, group16, residue, 0)

        @pl.when((t >= n16) & (t < n16 + 4))
        def _():
            _dil_segment(lambda i0: pb4_ref[0, 0, 0, pl.ds(i0, LANES), 0:LANES],
                         pb4_ref[0, 0, 0, :, LANES:2 * LANES],
                         pb4_ref[0, 0, 0, :, 2 * LANES:3 * LANES],
                         S // 4, 4, t - n16, _dil_bias(4, slope_a, slope_b),
                         False, False, online, *args)

        @pl.when(t == n16 + 4)
        def _():
            _dil_segment(lambda i0: pb1_ref[0, 0, pl.ds(i0, LANES), 0:LANES],
                         pb1_ref[0, 0, :, LANES:2 * LANES],
                         pb1_ref[0, 0, :, 2 * LANES:3 * LANES],
                         S, 1, 0, _dil_bias(1, slope_a, slope_b),
                         False, True, online, *args)

    @pl.when(direct)
    def _():
        pattern(False)

    @pl.when(jnp.logical_not(direct))
    def _():
        pattern(True)


def _dil_attn(pb1, pb4, pb16, nrm, slopes, group16=4):
    B, P, S, _ = pb1.shape
    n16 = 16 // group16
    nt = nrm.shape[1]
    return pl.pallas_call(
        functools.partial(_dil_kernel, group16=group16),
        grid=(B, P, n16 + 4 + 1),
        in_specs=[
            pl.BlockSpec(memory_space=pltpu.SMEM),
            pl.BlockSpec((1, nt, 1, LANES), lambda b, p, t: (b, 0, 0, 0)),
            pl.BlockSpec((1, 1, S, PAIR_W), lambda b, p, t: (b, p, 0, 0)),
            pl.BlockSpec((1, 1, 1, S // 4, PAIR_W),
                         lambda b, p, t: (b, p, jnp.clip(t - n16, 0, 3), 0, 0)),
            pl.BlockSpec((1, 1, group16, S // 16, PAIR_W),
                         lambda b, p, t: (b, p, jnp.minimum(t, n16 - 1), 0, 0)),
        ],
        out_specs=pl.BlockSpec((1, 1, S, LANES), lambda b, p, t: (b, p, 0, 0)),
        out_shape=jax.ShapeDtypeStruct((B, P, S, LANES), F32),
        scratch_shapes=[
            pltpu.VMEM((S, LANES), F32),
            pltpu.VMEM((S, LANES), F32),
            pltpu.VMEM((S, LANES), F32),
            pltpu.VMEM((S + 2 * DIL_RADIUS, LANES), BF16),
            pltpu.VMEM((S + 2 * DIL_RADIUS, 2 * LANES), BF16),
        ],
        compiler_params=_params(("parallel", "parallel", "arbitrary")),
        name="dil_attn",
    )(slopes, nrm, pb1, pb4, pb16)


def _out_mlp_kernel(x_ref, oa_ref, ob_ref, gdil_ref, wout_ref, gmlp_ref, wup_ref,
                    wdown_ref, gfin_ref, y_ref, *, ff_chunk):
    ob = jnp.concatenate([ob_ref[0, p] for p in range(N_DIL_PAIRS)], axis=-1)
    ob = _rms(ob, gdil_ref[...])
    mix = jnp.concatenate([oa_ref[0], ob.astype(BF16)], axis=-1)
    x1 = x_ref[0] + jnp.dot(mix, wout_ref[...], preferred_element_type=F32)
    h = _rms(x1, gmlp_ref[...]).astype(BF16)
    y = x1
    for c in range(D_FF // ff_chunk):
        u = jnp.dot(h, wup_ref[:, c * ff_chunk:(c + 1) * ff_chunk], preferred_element_type=F32)
        u = jnp.square(jnp.maximum(u, 0.0)).astype(BF16)
        y = y + jnp.dot(u, wdown_ref[c * ff_chunk:(c + 1) * ff_chunk, :],
                        preferred_element_type=F32)
    y_ref[0] = _rms(y, gfin_ref[...])


def _out_mlp(x, oa, ob, g_dil, w_out, g_mlp, w_up, w_down, g_final, tm, ff_chunk):
    B, S, D = x.shape
    const = lambda shape: pl.BlockSpec(shape, lambda b, i: (0,) * len(shape),
                                       pipeline_mode=pl.Buffered(1))
    return pl.pallas_call(
        functools.partial(_out_mlp_kernel, ff_chunk=ff_chunk),
        grid=(B, S // tm),
        in_specs=[
            pl.BlockSpec((1, tm, D), lambda b, i: (b, i, 0)),
            pl.BlockSpec((1, tm, N_DIFF_HEADS * LANES), lambda b, i: (b, i, 0)),
            pl.BlockSpec((1, N_DIL_PAIRS, tm, LANES), lambda b, i: (b, 0, i, 0)),
            const((1, DIL_W)),
            const((2 * DIL_W, D)),
            const((1, D)),
            const((D, D_FF)),
            const((D_FF, D)),
            const((1, D)),
        ],
        out_specs=pl.BlockSpec((1, tm, D), lambda b, i: (b, i, 0)),
        out_shape=jax.ShapeDtypeStruct((B, S, D), F32),
        compiler_params=_params(("parallel", "parallel")),
        name="out_mlp",
    )(x, oa, ob, g_dil, w_out, g_mlp, w_up, w_down, g_final)


def _alibi_slopes(n):
    return 2.0 ** (-8.0 * jnp.arange(1, n + 1, dtype=F32) / n)


def _layer(x, g_mix, w_in, wq_t, wv_t, lam_qk, g_diff, g_dil, w_out, g_mlp, w_up, w_down,
           g_final, lam_init):
    qt, k, vt, pb1, pb4, pb16, nrm = _in_proj(x, g_mix, w_in, wq_t, wv_t, tm=512)
    oa = _diff_attn(qt, k, vt, _alibi_slopes(N_DIFF_HEADS), lam_qk, g_diff, lam_init, bq=1024)
    assert DIL_PATTERNS == ((2 * DIL_RADIUS, 1), (8 * DIL_RADIUS, 4), (32 * DIL_RADIUS, 16))
    ob = _dil_attn(pb1, pb4, pb16, nrm, _alibi_slopes(N_DIL_HEADS))
    return _out_mlp(x, oa, ob, g_dil, w_out, g_mlp, w_up, w_down, g_final,
                    tm=512, ff_chunk=1024)


def kernel(x_prompt, x_sample, g_mix, w_in, lam_qk, g_diff, g_dil, w_out, g_mlp, w_up,
           w_down, g_final):
    assert g_mix.shape[0] == 1
    lam_init = 0.8 - 0.6 * math.exp(-0.3 * 0)
    w_in_b = w_in[0].astype(BF16)
    weights = (
        g_mix[0][None, :], w_in_b, w_in_b[:, 0:DIFF_QW].T, w_in_b[:, 2 * DIFF_QW:DIFF_W].T,
        lam_qk[0],
        g_diff[0], g_dil[0][None, :], w_out[0].astype(BF16), g_mlp[0][None, :],
        w_up[0].astype(BF16), w_down[0].astype(BF16), g_final[None, :],
    )
    return tuple(_layer(x, *weights, lam_init) for x in (x_prompt, x_sample))
```

```python
import functools
import math

import jax
import jax.numpy as jnp
from jax import lax
from jax.experimental import pallas as pl
from jax.experimental.pallas import tpu as pltpu

D_MODEL = 1024
HEAD_DIM = 64
N_DIFF_HEADS = 4
N_DIL_HEADS = 8
N_DIL_PAIRS = N_DIL_HEADS // 2
DIL_PATTERNS = ((128, 1), (512, 4), (2048, 16))
DIL_RADIUS = 64
D_FF = 4 * D_MODEL
EPS = 1e-5
DIFF_QW = N_DIFF_HEADS * 2 * HEAD_DIM
DIFF_W = 3 * DIFF_QW
DIL_W = N_DIL_HEADS * HEAD_DIM
IN_W = DIFF_W + 3 * DIL_W
LANES = 128
PAIR_W = 3 * LANES
NEG = -1e30
LOG2E = math.log2(math.e)
ONES_ROWS = 16
DIRECT_EXP_MAX_LOGIT = 60.0
NRM_DIFF_K = 4 * N_DIL_PAIRS
NRM_DIFF_Q = NRM_DIFF_K + N_DIFF_HEADS
VMEM_LIMIT = 56 * 1024 * 1024

BF16 = jnp.bfloat16
F32 = jnp.float32


def _rms(x, g):
    return x * lax.rsqrt(jnp.mean(x * x, axis=-1, keepdims=True) + EPS) * g


def _params(sem):
    return pltpu.CompilerParams(dimension_semantics=sem, vmem_limit_bytes=VMEM_LIMIT)


def _in_proj_kernel(x_ref, g_ref, w_ref, wqt_ref, wvt_ref, qt_ref, k_ref, vt_ref,
                    pb1_ref, pb4_ref, pb16_ref, nrm_ref, slab_ref, slab4_ref):
    scale = HEAD_DIM ** -0.5
    nt = (((1,), (1,)), ((), ()))
    tm = x_ref.shape[1]
    lane = lax.broadcasted_iota(jnp.int32, (1, LANES), 1)
    nrm = jnp.zeros((1, LANES), F32)
    h = _rms(x_ref[0], g_ref[...]).astype(BF16)
    qt = lax.dot_general(wqt_ref[...], h, nt, preferred_element_type=F32) * (scale * LOG2E)
    qt_ref[0] = qt.astype(BF16)
    for hm in range(2 * N_DIFF_HEADS):
        qm = qt[hm * HEAD_DIM:(hm + 1) * HEAD_DIM, :]
        big = jnp.max(jnp.sum(qm * qm, axis=0, keepdims=True), axis=1, keepdims=True)
        nrm = jnp.where(lane == NRM_DIFF_Q + hm, big, nrm)
    kd = jnp.dot(h, w_ref[:, DIFF_QW:2 * DIFF_QW], preferred_element_type=F32)
    k_ref[0] = kd.astype(BF16)
    for hh in range(N_DIFF_HEADS):
        kh = kd[:, hh * LANES:(hh + 1) * LANES]
        big = jnp.max(jnp.sum(kh * kh, axis=1, keepdims=True), axis=0, keepdims=True)
        nrm = jnp.where(lane == NRM_DIFF_K + hh, big, nrm)
    vt = lax.dot_general(wvt_ref[...], h, nt, preferred_element_type=F32).astype(BF16)
    for hh in range(N_DIFF_HEADS):
        vt_ref[0, hh, 0] = vt[hh * LANES:(hh + 1) * LANES, :]
    for c in range(3):
        p = jnp.dot(h, w_ref[:, DIFF_W + c * DIL_W:DIFF_W + (c + 1) * DIL_W],
                    preferred_element_type=F32)
        if c == 0:
            p = p * (scale * LOG2E)
        cols = slice(c * LANES, (c + 1) * LANES)
        for hp in range(N_DIL_PAIRS):
            ph = p[:, hp * LANES:(hp + 1) * LANES]
            if c < 2:
                sq = ph * ph
                for e in range(2):
                    mine = (lane >= e * HEAD_DIM) & (lane < (e + 1) * HEAD_DIM)
                    big = jnp.max(jnp.sum(jnp.where(mine, sq, 0.0), axis=1, keepdims=True),
                                  axis=0, keepdims=True)
                    nrm = jnp.where(lane == 4 * hp + 2 * e + c, big, nrm)
            pb1_ref[0, hp, :, cols] = ph.astype(BF16)
            s = c * N_DIL_PAIRS + hp
            slab_ref[s] = ph
            for g in range(4):
                v4 = slab_ref[s, pl.ds(g, tm // 4, stride=4), :]
                pb4_ref[0, hp, g, :, cols] = v4.astype(BF16)
                slab4_ref[s, g] = v4
            for g in range(4):
                for c2 in range(4):
                    v16 = slab4_ref[s, g, pl.ds(c2, tm // 16, stride=4), :]
                    pb16_ref[0, hp, 4 * c2 + g, :, cols] = v16.astype(BF16)
    nrm_ref[0, 0] = nrm


def _in_proj(x, g_mix, w_in, wq_t, wv_t, tm):
    B, S, D = x.shape
    H = N_DIFF_HEADS
    P = N_DIL_PAIRS
    return pl.pallas_call(
        _in_proj_kernel,
        grid=(B, S // tm),
        in_specs=[
            pl.BlockSpec((1, tm, D), lambda b, i: (b, i, 0)),
            pl.BlockSpec((1, D), lambda b, i: (0, 0)),
            pl.BlockSpec((D, IN_W), lambda b, i: (0, 0)),
            pl.BlockSpec((DIFF_QW, D), lambda b, i: (0, 0)),
            pl.BlockSpec((DIFF_QW, D), lambda b, i: (0, 0)),
        ],
        out_specs=[
            pl.BlockSpec((1, DIFF_QW, tm), lambda b, i: (b, 0, i)),
            pl.BlockSpec((1, tm, DIFF_QW), lambda b, i: (b, i, 0)),
            pl.BlockSpec((1, H, 1, LANES, tm), lambda b, i: (b, 0, i, 0, 0)),
            pl.BlockSpec((1, P, tm, PAIR_W), lambda b, i: (b, 0, i, 0)),
            pl.BlockSpec((1, P, 4, tm // 4, PAIR_W), lambda b, i: (b, 0, 0, i, 0)),
            pl.BlockSpec((1, P, 16, tm // 16, PAIR_W), lambda b, i: (b, 0, 0, i, 0)),
            pl.BlockSpec((1, 1, 1, LANES), lambda b, i: (b, i, 0, 0)),
        ],
        out_shape=[
            jax.ShapeDtypeStruct((B, DIFF_QW, S), BF16),
            jax.ShapeDtypeStruct((B, S, DIFF_QW), BF16),
            jax.ShapeDtypeStruct((B, H, S // tm, LANES, tm), BF16),
            jax.ShapeDtypeStruct((B, P, S, PAIR_W), BF16),
            jax.ShapeDtypeStruct((B, P, 4, S // 4, PAIR_W), BF16),
            jax.ShapeDtypeStruct((B, P, 16, S // 16, PAIR_W), BF16),
            jax.ShapeDtypeStruct((B, S // tm, 1, LANES), F32),
        ],
        scratch_shapes=[
            pltpu.VMEM((3 * P, tm, LANES), F32),
            pltpu.VMEM((3 * P, 4, tm // 4, LANES), F32),
        ],
        compiler_params=_params(("parallel", "parallel")),
        name="in_proj",
    )(x, g_mix, w_in, wq_t, wv_t)


def _diff_attn_kernel(slopes_ref, lam_ref, g_ref, nrm_ref, qt_ref, k_ref, vt_ref, o_ref,
                      q2t_ref, m_ref, acc_ref, cb_ref, *, bq, bk, lam_init):
    G = cb_ref.shape[1]
    h = pl.program_id(1)
    qi = pl.program_id(2)
    nkv = k_ref.shape[1] // bk
    slope2 = slopes_ref[h] * LOG2E

    qt = qt_ref[0]
    sub = lax.broadcasted_iota(jnp.int32, qt.shape, 0)
    zero = jnp.zeros_like(qt)
    q2t_ref[:, 0:bq] = jnp.where(sub < HEAD_DIM, qt, zero)
    q2t_ref[:, bq:2 * bq] = jnp.where(sub >= HEAD_DIM, qt, zero)
    m_ref[...] = jnp.full(m_ref.shape, -jnp.inf, F32)
    acc_ref[...] = jnp.zeros(acc_ref.shape, F32)
    key_off = lax.broadcasted_iota(jnp.int32, (bk, G), 0).astype(F32)
    cb_ref[...] = slope2 * key_off
    lane = lax.broadcasted_iota(jnp.int32, (1, 2 * bq), 1)
    pos_q = (qi * bq + jnp.where(lane >= bq, lane - bq, lane)).astype(F32)
    ones = jnp.ones((ONES_ROWS, bk), BF16)

    q_rows = nrm_ref.shape[1] * bq // k_ref.shape[1]
    k_big = jnp.max(nrm_ref[0, :, 0, :], axis=0, keepdims=True)
    q_big = jnp.max(nrm_ref[0, pl.ds(qi * q_rows, q_rows), 0, :], axis=0, keepdims=True)
    nlane = lax.broadcasted_iota(jnp.int32, k_big.shape, 1)
    k_max = jnp.max(jnp.where(nlane == NRM_DIFF_K + h, k_big, 0.0))
    q_mine = (nlane == NRM_DIFF_Q + 2 * h) | (nlane == NRM_DIFF_Q + 2 * h + 1)
    q_max = jnp.max(jnp.where(q_mine, q_big, 0.0))
    direct = q_max * k_max <= DIRECT_EXP_MAX_LOGIT ** 2

    def online_tile(j, side):
        start = pl.multiple_of(j * bk, bk)
        k_t = k_ref[0, pl.ds(start, bk), :]
        vt1 = jnp.concatenate([vt_ref[0, 0, j], ones], axis=0)
        j0 = (j * bk).astype(F32)
        groups = [slice(gi * G, (gi + 1) * G) for gi in range(2 * bq // G)]
        scores = [jnp.dot(k_t, q2t_ref[:, sl], preferred_element_type=F32) for sl in groups]
        for sl, s in zip(groups, scores):
            pq = pos_q[:, sl]
            if side == 0:
                u = s - slope2 * jnp.abs(pq - (j0 + key_off))
                a = jnp.zeros_like(pq)
            elif side < 0:
                u = s + cb_ref[...]
                a = slope2 * (j0 - pq)
            else:
                u = s - cb_ref[...]
                a = slope2 * (pq - j0)
            m_old = m_ref[:, sl]
            m_new = jnp.maximum(m_old, jnp.max(u, axis=0, keepdims=True) + a)
            alpha = jnp.exp2(m_old - m_new)
            p = jnp.exp2(u - (m_new - a)).astype(BF16)
            acc_ref[:, sl] = alpha * acc_ref[:, sl] + jnp.dot(
                vt1, p, preferred_element_type=F32)
            m_ref[:, sl] = m_new

    def direct_pair(j, side):
        groups = [slice(gi * G, (gi + 1) * G) for gi in range(2 * bq // G)]

        def operands(jj):
            k_t = k_ref[0, pl.ds(pl.multiple_of(jj * bk, bk), bk), :]
            return k_t, jnp.concatenate([vt_ref[0, 0, jj], ones], axis=0), (jj * bk).astype(F32)

        def score(k_t, sl):
            return jnp.dot(k_t, q2t_ref[:, sl], preferred_element_type=F32)

        def pv(vt1, j0, sl, s):
            pq = pos_q[:, sl]
            if side == 0:
                e = s - slope2 * jnp.abs(pq - (j0 + key_off))
            elif side < 0:
                e = s + cb_ref[...] + slope2 * (j0 - pq)
            else:
                e = s - cb_ref[...] + slope2 * (pq - j0)
            return jnp.dot(vt1, jnp.exp2(e).astype(BF16), preferred_element_type=F32)

        k_a, vt_a, j0_a = operands(j)
        k_b, vt_b, j0_b = operands(j + 1)
        s_a = [score(k_a, sl) for sl in groups]
        pv_a, s_b = [], []
        for sl, s in zip(groups, s_a):
            pv_a.append(pv(vt_a, j0_a, sl, s))
            s_b.append(score(k_b, sl))
        pv_b = [pv(vt_b, j0_b, sl, s) for sl, s in zip(groups, s_b)]
        for sl, x, y in zip(groups, pv_a, pv_b):
            acc_ref[:, sl] += x + y

    def all_tiles(online):
        step = 1 if online else 2

        def one(j, side):
            if online:
                online_tile(j, side)
            else:
                direct_pair(j, side)

        def before(t, carry):
            one(t * step, -1)
            return carry

        def after(t, carry):
            one(t * step, 1)
            return carry

        ndiag = bq // bk
        assert ndiag % step == 0 and nkv % step == 0
        lax.fori_loop(0, qi * ndiag // step, before, 0)
        for d in range(0, ndiag, step):
            one(qi * ndiag + d, 0)
        lax.fori_loop((qi + 1) * ndiag // step, nkv // step, after, 0)

    @pl.when(direct)
    def _():
        all_tiles(False)

    @pl.when(jnp.logical_not(direct))
    def _():
        all_tiles(True)

    lq = lam_ref[...]
    lam = (jnp.exp(jnp.sum(lq[0:1] * lq[1:2], axis=1, keepdims=True))
           - jnp.exp(jnp.sum(lq[2:3] * lq[3:4], axis=1, keepdims=True)) + lam_init)
    acc = acc_ref[...]
    ot = acc[0:LANES] * (1.0 / acc[LANES:LANES + 1])
    od = (ot[:, 0:bq] - lam * ot[:, bq:2 * bq]).T
    od = od * lax.rsqrt(jnp.mean(od * od, axis=-1, keepdims=True) + EPS)
    od = od * g_ref[pl.ds(h, 1), :] * (1.0 - lam_init)
    o_ref[0] = od.astype(BF16)


def _diff_attn(qt, k, vt, nrm, slopes, lam_qk, g_diff, lam_init, bq):
    B, S, _ = k.shape
    H = N_DIFF_HEADS
    nkv, bk = vt.shape[2], vt.shape[4]
    assert bq % bk == 0
    kern = functools.partial(_diff_attn_kernel, bq=bq, bk=bk, lam_init=lam_init)
    return pl.pallas_call(
        kern,
        grid=(B, H, S // bq),
        in_specs=[
            pl.BlockSpec(memory_space=pltpu.SMEM),
            pl.BlockSpec((4, HEAD_DIM), lambda b, h, i: (0, 0)),
            pl.BlockSpec((H, LANES), lambda b, h, i: (0, 0)),
            pl.BlockSpec((1, nrm.shape[1], 1, LANES), lambda b, h, i: (b, 0, 0, 0)),
            pl.BlockSpec((1, LANES, bq), lambda b, h, i: (b, h, i)),
            pl.BlockSpec((1, S, LANES), lambda b, h, i: (b, 0, h)),
            pl.BlockSpec((1, 1, nkv, LANES, bk), lambda b, h, i: (b, h, 0, 0, 0)),
        ],
        out_specs=pl.BlockSpec((1, bq, LANES), lambda b, h, i: (b, i, h)),
        out_shape=jax.ShapeDtypeStruct((B, S, H * LANES), BF16),
        scratch_shapes=[
            pltpu.VMEM((LANES, 2 * bq), BF16),
            pltpu.VMEM((1, 2 * bq), F32),
            pltpu.VMEM((LANES + ONES_ROWS, 2 * bq), F32),
            pltpu.VMEM((bk, 8 * LANES), F32),
        ],
        compiler_params=_params(("parallel", "parallel", "arbitrary")),
        name="diff_attn",
    )(slopes, lam_qk, g_diff, nrm, qt, k, vt)


def _dil_bias(dil, slope_a, slope_b):
    QB, R = LANES, DIL_RADIUS
    KW = QB + 2 * R
    r_i = lax.broadcasted_iota(jnp.int32, (QB, KW), 0)
    c_i = lax.broadcasted_iota(jnp.int32, (QB, KW), 1)
    rel = c_i - r_i - R
    band = (rel >= -R) & (rel <= R)
    dist = (jnp.abs(rel) * dil).astype(F32)
    return jnp.concatenate([jnp.where(band, -(slope_a * LOG2E) * dist, NEG),
                            jnp.where(band, -(slope_b * LOG2E) * dist, NEG)], axis=0)


def _dil_segment(q_at, k_src, v_src, L, dil, tok0, bias2, first, last, online,
                 acc_ref, l_ref, m_ref, o_ref, kpad, vpad):
    QB, R = LANES, DIL_RADIUS
    KW = QB + 2 * R
    unroll = 1 if online else min(4, L // QB)
    assert L % (QB * unroll) == 0
    zpad = jnp.zeros((R, LANES), BF16)
    kpad[0:R, :] = zpad
    kpad[R:R + L, :] = k_src
    kpad[R + L:R + L + R, :] = zpad
    vpad[0:R, :] = jnp.zeros((R, 2 * LANES), BF16)
    vpad[R:R + L, 0:LANES] = v_src
    vpad[R:R + L, LANES:2 * LANES] = jnp.ones((L, LANES), BF16)
    vpad[R + L:R + L + R, :] = jnp.zeros((R, 2 * LANES), BF16)

    lane_q = lax.broadcasted_iota(jnp.int32, (QB, LANES), 1)
    head_a = lane_q < HEAD_DIM
    c2 = lax.broadcasted_iota(jnp.int32, (2 * QB, KW), 1)

    def rows(i0):
        if dil == 1:
            return pl.ds(i0, QB)
        return pl.ds(tok0 + dil * i0, QB, stride=dil)

    def scores(i0):
        q = q_at(i0)
        zero = jnp.zeros_like(q)
        q2 = jnp.concatenate([jnp.where(head_a, q, zero), jnp.where(head_a, zero, q)], axis=0)
        s = lax.dot_general(q2, kpad[pl.ds(i0, KW), :], (((1,), (1,)), ((), ())),
                            preferred_element_type=F32)
        return s + bias2

    def finish(i0, s):
        if online:
            key = i0 - R + c2
            s = jnp.where((key >= 0) & (key < L), s, NEG)
            m_c = jnp.max(s, axis=1, keepdims=True)
            s = s - m_c
        pv = jnp.dot(jnp.exp2(s).astype(BF16), vpad[pl.ds(i0, KW), :],
                     preferred_element_type=F32)
        acc_n = jnp.where(head_a, pv[0:QB, 0:LANES], pv[QB:2 * QB, 0:LANES])
        l_n = jnp.where(head_a, pv[0:QB, LANES:2 * LANES], pv[QB:2 * QB, LANES:2 * LANES])
        r = rows(i0)
        if online:
            m_n = jnp.where(head_a, jnp.broadcast_to(m_c[0:QB], (QB, LANES)),
                            jnp.broadcast_to(m_c[QB:2 * QB], (QB, LANES)))
            if not first:
                m_p = m_ref[r, :]
                m_c_rep, m_n = m_n, jnp.maximum(m_p, m_n)
                e_p, e_c = jnp.exp2(m_p - m_n), jnp.exp2(m_c_rep - m_n)
                acc_n = acc_ref[r, :] * e_p + acc_n * e_c
                l_n = l_ref[r, :] * e_p + l_n * e_c
            if not last:
                m_ref[r, :] = m_n
        elif not first:
            acc_n = acc_ref[r, :] + acc_n
            l_n = l_ref[r, :] + l_n
        if last:
            o_ref[0, 0, r, :] = acc_n / l_n
        else:
            acc_ref[r, :] = acc_n
            l_ref[r, :] = l_n

    def body(n, carry):
        starts = [pl.multiple_of((n * unroll + u) * QB, QB) for u in range(unroll)]
        ss = [scores(i0) for i0 in starts]
        for i0, s in zip(starts, ss):
            finish(i0, s)
        return carry

    lax.fori_loop(0, L // (QB * unroll), body, 0)


def _dil_kernel(slopes_ref, nrm_ref, pb1_ref, pb4_ref, pb16_ref, o_ref,
                     acc_ref, l_ref, m_ref, kpad, vpad, *, group16):
    hp = pl.program_id(1)
    t = pl.program_id(2)
    S = pb1_ref.shape[2]
    n16 = 16 // group16
    slope_a = slopes_ref[2 * hp]
    slope_b = slopes_ref[2 * hp + 1]

    big = jnp.max(nrm_ref[0, :, 0, :], axis=0, keepdims=True)
    lane = lax.broadcasted_iota(jnp.int32, big.shape, 1)
    def pick(idx):
        return jnp.max(jnp.where(lane == idx, big, 0.0))

    direct = ((pick(4 * hp) * pick(4 * hp + 1) <= DIRECT_EXP_MAX_LOGIT ** 2)
              & (pick(4 * hp + 2) * pick(4 * hp + 3) <= DIRECT_EXP_MAX_LOGIT ** 2))

    def pattern(online):
        args = (acc_ref, l_ref, m_ref, o_ref, kpad, vpad)

        @pl.when(t < n16)
        def _():
            bias2 = _dil_bias(16, slope_a, slope_b)

            def residue(r, carry):
                _dil_segment(lambda i0: pb16_ref[0, 0, r, pl.ds(i0, LANES), 0:LANES],
                             pb16_ref[0, 0, r, :, LANES:2 * LANES],
                             pb16_ref[0, 0, r, :, 2 * LANES:3 * LANES],
                             S // 16, 16, t * group16 + r, bias2, True, False, online, *args)
                return carry

            lax.fori_loop(0, group16, residue, 0)

        @pl.when((t >= n16) & (t < n16 + 4))
        def _():
            _dil_segment(lambda i0: pb4_ref[0, 0, 0, pl.ds(i0, LANES), 0:LANES],
                         pb4_ref[0, 0, 0, :, LANES:2 * LANES],
                         pb4_ref[0, 0, 0, :, 2 * LANES:3 * LANES],
                         S // 4, 4, t - n16, _dil_bias(4, slope_a, slope_b),
                         False, False, online, *args)

        @pl.when(t == n16 + 4)
        def _():
            _dil_segment(lambda i0: pb1_ref[0, 0, pl.ds(i0, LANES), 0:LANES],
                         pb1_ref[0, 0, :, LANES:2 * LANES],
                         pb1_ref[0, 0, :, 2 * LANES:3 * LANES],
                         S, 1, 0, _dil_bias(1, slope_a, slope_b),
                         False, True, online, *args)

    @pl.when(direct)
    def _():
        pattern(False)

    @pl.when(jnp.logical_not(direct))
    def _():
        pattern(True)


def _dil_attn(pb1, pb4, pb16, nrm, slopes, group16=4):
    B, P, S, _ = pb1.shape
    n16 = 16 // group16
    nt = nrm.shape[1]
    return pl.pallas_call(
        functools.partial(_dil_kernel, group16=group16),
        grid=(B, P, n16 + 4 + 1),
        in_specs=[
            pl.BlockSpec(memory_space=pltpu.SMEM),
            pl.BlockSpec((1, nt, 1, LANES), lambda b, p, t: (b, 0, 0, 0)),
            pl.BlockSpec((1, 1, S, PAIR_W), lambda b, p, t: (b, p, 0, 0)),
            pl.BlockSpec((1, 1, 1, S // 4, PAIR_W),
                         lambda b, p, t: (b, p, jnp.clip(t - n16, 0, 3), 0, 0)),
            pl.BlockSpec((1, 1, group16, S // 16, PAIR_W),
                         lambda b, p, t: (b, p, jnp.minimum(t, n16 - 1), 0, 0)),
        ],
        out_specs=pl.BlockSpec((1, 1, S, LANES), lambda b, p, t: (b, p, 0, 0)),
        out_shape=jax.ShapeDtypeStruct((B, P, S, LANES), F32),
        scratch_shapes=[
            pltpu.VMEM((S, LANES), F32),
            pltpu.VMEM((S, LANES), F32),
            pltpu.VMEM((S, LANES), F32),
            pltpu.VMEM((S + 2 * DIL_RADIUS, LANES), BF16),
            pltpu.VMEM((S + 2 * DIL_RADIUS, 2 * LANES), BF16),
        ],
        compiler_params=_params(("parallel", "parallel", "arbitrary")),
        name="dil_attn",
    )(slopes, nrm, pb1, pb4, pb16)


def _out_mlp_kernel(x_ref, oa_ref, ob_ref, gdil_ref, wout_ref, gmlp_ref, wup_ref,
                    wdown_ref, gfin_ref, y_ref, *, ff_chunk):
    ob = jnp.concatenate([ob_ref[0, p] for p in range(N_DIL_PAIRS)], axis=-1)
    ob = _rms(ob, gdil_ref[...])
    mix = jnp.concatenate([oa_ref[0], ob.astype(BF16)], axis=-1)
    x1 = x_ref[0] + jnp.dot(mix, wout_ref[...], preferred_element_type=F32)
    h = _rms(x1, gmlp_ref[...]).astype(BF16)
    y = x1
    for c in range(D_FF // ff_chunk):
        u = jnp.dot(h, wup_ref[:, c * ff_chunk:(c + 1) * ff_chunk], preferred_element_type=F32)
        u = jnp.square(jnp.maximum(u, 0.0)).astype(BF16)
        y = y + jnp.dot(u, wdown_ref[c * ff_chunk:(c + 1) * ff_chunk, :],
                        preferred_element_type=F32)
    y_ref[0] = _rms(y, gfin_ref[...])


def _out_mlp(x, oa, ob, g_dil, w_out, g_mlp, w_up, w_down, g_final, tm, ff_chunk):
    B, S, D = x.shape
    const = lambda shape: pl.BlockSpec(shape, lambda b, i: (0,) * len(shape),
                                       pipeline_mode=pl.Buffered(1))
    return pl.pallas_call(
        functools.partial(_out_mlp_kernel, ff_chunk=ff_chunk),
        grid=(B, S // tm),
        in_specs=[
            pl.BlockSpec((1, tm, D), lambda b, i: (b, i, 0)),
            pl.BlockSpec((1, tm, N_DIFF_HEADS * LANES), lambda b, i: (b, i, 0)),
            pl.BlockSpec((1, N_DIL_PAIRS, tm, LANES), lambda b, i: (b, 0, i, 0)),
            const((1, DIL_W)),
            const((2 * DIL_W, D)),
            const((1, D)),
            const((D, D_FF)),
            const((D_FF, D)),
            const((1, D)),
        ],
        out_specs=pl.BlockSpec((1, tm, D), lambda b, i: (b, i, 0)),
        out_shape=jax.ShapeDtypeStruct((B, S, D), F32),
        compiler_params=_params(("parallel", "parallel")),
        name="out_mlp",
    )(x, oa, ob, g_dil, w_out, g_mlp, w_up, w_down, g_final)


def _alibi_slopes(n):
    return 2.0 ** (-8.0 * jnp.arange(1, n + 1, dtype=F32) / n)


def _layer(x, g_mix, w_in, wq_t, wv_t, lam_qk, g_diff, g_dil, w_out, g_mlp, w_up, w_down,
           g_final, lam_init):
    qt, k, vt, pb1, pb4, pb16, nrm = _in_proj(x, g_mix, w_in, wq_t, wv_t, tm=512)
    oa = _diff_attn(qt, k, vt, nrm, _alibi_slopes(N_DIFF_HEADS), lam_qk, g_diff, lam_init,
                    bq=1024)
    assert DIL_PATTERNS == ((2 * DIL_RADIUS, 1), (8 * DIL_RADIUS, 4), (32 * DIL_RADIUS, 16))
    ob = _dil_attn(pb1, pb4, pb16, nrm, _alibi_slopes(N_DIL_HEADS))
    return _out_mlp(x, oa, ob, g_dil, w_out, g_mlp, w_up, w_down, g_final,
                    tm=512, ff_chunk=1024)


def kernel(x_prompt, x_sample, g_mix, w_in, lam_qk, g_diff, g_dil, w_out, g_mlp, w_up,
           w_down, g_final):
    assert g_mix.shape[0] == 1
    lam_init = 0.8 - 0.6 * math.exp(-0.3 * 0)
    w_in_b = w_in[0].astype(BF16)
    weights = (
        g_mix[0][None, :], w_in_b, w_in_b[:, 0:DIFF_QW].T, w_in_b[:, 2 * DIFF_QW:DIFF_W].T,
        lam_qk[0],
        g_diff[0], g_dil[0][None, :], w_out[0].astype(BF16), g_mlp[0][None, :],
        w_up[0].astype(BF16), w_down[0].astype(BF16), g_final[None, :],
    )
    return tuple(_layer(x, *weights, lam_init) for x in (x_prompt, x_sample))
```

```python
import functools
import math

import jax
import jax.numpy as jnp
from jax import lax
from jax.experimental import pallas as pl
from jax.experimental.pallas import tpu as pltpu

D_MODEL = 1024
HEAD_DIM = 64
N_DIFF_HEADS = 4
N_DIL_HEADS = 8
N_DIL_PAIRS = N_DIL_HEADS // 2
DIL_PATTERNS = ((128, 1), (512, 4), (2048, 16))
DIL_RADIUS = 64
D_FF = 4 * D_MODEL
EPS = 1e-5
DIFF_QW = N_DIFF_HEADS * 2 * HEAD_DIM
DIFF_W = 3 * DIFF_QW
DIL_W = N_DIL_HEADS * HEAD_DIM
IN_W = DIFF_W + 3 * DIL_W
LANES = 128
PAIR_W = 3 * LANES
NEG = -1e30
LOG2E = math.log2(math.e)
ONES_ROWS = 16
DIRECT_EXP_MAX_LOGIT = 60.0
DIL_MAX_BLOCKS = 16
NRM_DIFF_K = 4 * N_DIL_PAIRS
NRM_DIFF_Q = NRM_DIFF_K + N_DIFF_HEADS
VMEM_LIMIT = 56 * 1024 * 1024

BF16 = jnp.bfloat16
F32 = jnp.float32


def _rms(x, g):
    return x * lax.rsqrt(jnp.mean(x * x, axis=-1, keepdims=True) + EPS) * g


def _params(sem):
    return pltpu.CompilerParams(dimension_semantics=sem, vmem_limit_bytes=VMEM_LIMIT)


def _in_proj_kernel(x_ref, g_ref, w_ref, wqt_ref, wvt_ref, qt_ref, k_ref, vt_ref,
                    pb1_ref, pb4_ref, pb16_ref, nrm_ref, slab_ref, slab4_ref):
    scale = HEAD_DIM ** -0.5
    nt = (((1,), (1,)), ((), ()))
    tm = x_ref.shape[1]
    lane = lax.broadcasted_iota(jnp.int32, (1, LANES), 1)
    nrm = jnp.zeros((1, LANES), F32)
    h = _rms(x_ref[0], g_ref[...]).astype(BF16)
    for c in range(3):
        p = jnp.dot(h, w_ref[:, DIFF_W + c * DIL_W:DIFF_W + (c + 1) * DIL_W],
                    preferred_element_type=F32)
        if c == 0:
            p = p * (scale * LOG2E)
        cols = slice(c * LANES, (c + 1) * LANES)
        for hp in range(N_DIL_PAIRS):
            ph = p[:, hp * LANES:(hp + 1) * LANES]
            if c < 2:
                sq = ph * ph
                for e in range(2):
                    mine = (lane >= e * HEAD_DIM) & (lane < (e + 1) * HEAD_DIM)
                    big = jnp.max(jnp.sum(jnp.where(mine, sq, 0.0), axis=1, keepdims=True),
                                  axis=0, keepdims=True)
                    nrm = jnp.where(lane == 4 * hp + 2 * e + c, big, nrm)
            pb1_ref[0, hp, :, cols] = ph.astype(BF16)
            s = c * N_DIL_PAIRS + hp
            slab_ref[s] = ph
            for g in range(4):
                v4 = slab_ref[s, pl.ds(g, tm // 4, stride=4), :]
                pb4_ref[0, hp, g, :, cols] = v4.astype(BF16)
                slab4_ref[s, g] = v4
            for g in range(4):
                for c2 in range(4):
                    v16 = slab4_ref[s, g, pl.ds(c2, tm // 16, stride=4), :]
                    pb16_ref[0, hp, 4 * c2 + g, :, cols] = v16.astype(BF16)
    kd = jnp.dot(h, w_ref[:, DIFF_QW:2 * DIFF_QW], preferred_element_type=F32)
    k_ref[0] = kd.astype(BF16)
    for hh in range(N_DIFF_HEADS):
        kh = kd[:, hh * LANES:(hh + 1) * LANES]
        big = jnp.max(jnp.sum(kh * kh, axis=1, keepdims=True), axis=0, keepdims=True)
        nrm = jnp.where(lane == NRM_DIFF_K + hh, big, nrm)
    qt = lax.dot_general(wqt_ref[...], h, nt, preferred_element_type=F32) * (scale * LOG2E)
    qt_ref[0] = qt.astype(BF16)
    for hm in range(2 * N_DIFF_HEADS):
        qm = qt[hm * HEAD_DIM:(hm + 1) * HEAD_DIM, :]
        big = jnp.max(jnp.sum(qm * qm, axis=0, keepdims=True), axis=1, keepdims=True)
        nrm = jnp.where(lane == NRM_DIFF_Q + hm, big, nrm)
    nrm_ref[0, 0] = nrm
    vt = lax.dot_general(wvt_ref[...], h, nt, preferred_element_type=F32).astype(BF16)
    for hh in range(N_DIFF_HEADS):
        vt_ref[0, hh, 0] = vt[hh * LANES:(hh + 1) * LANES, :]


def _in_proj(x, g_mix, w_in, wq_t, wv_t, tm):
    B, S, D = x.shape
    H = N_DIFF_HEADS
    P = N_DIL_PAIRS
    return pl.pallas_call(
        _in_proj_kernel,
        grid=(B, S // tm),
        in_specs=[
            pl.BlockSpec((1, tm, D), lambda b, i: (b, i, 0)),
            pl.BlockSpec((1, D), lambda b, i: (0, 0)),
            pl.BlockSpec((D, IN_W), lambda b, i: (0, 0)),
            pl.BlockSpec((DIFF_QW, D), lambda b, i: (0, 0)),
            pl.BlockSpec((DIFF_QW, D), lambda b, i: (0, 0)),
        ],
        out_specs=[
            pl.BlockSpec((1, DIFF_QW, tm), lambda b, i: (b, 0, i)),
            pl.BlockSpec((1, tm, DIFF_QW), lambda b, i: (b, i, 0)),
            pl.BlockSpec((1, H, 1, LANES, tm), lambda b, i: (b, 0, i, 0, 0)),
            pl.BlockSpec((1, P, tm, PAIR_W), lambda b, i: (b, 0, i, 0)),
            pl.BlockSpec((1, P, 4, tm // 4, PAIR_W), lambda b, i: (b, 0, 0, i, 0)),
            pl.BlockSpec((1, P, 16, tm // 16, PAIR_W), lambda b, i: (b, 0, 0, i, 0)),
            pl.BlockSpec((1, 1, 1, LANES), lambda b, i: (b, i, 0, 0)),
        ],
        out_shape=[
            jax.ShapeDtypeStruct((B, DIFF_QW, S), BF16),
            jax.ShapeDtypeStruct((B, S, DIFF_QW), BF16),
            jax.ShapeDtypeStruct((B, H, S // tm, LANES, tm), BF16),
            jax.ShapeDtypeStruct((B, P, S, PAIR_W), BF16),
            jax.ShapeDtypeStruct((B, P, 4, S // 4, PAIR_W), BF16),
            jax.ShapeDtypeStruct((B, P, 16, S // 16, PAIR_W), BF16),
            jax.ShapeDtypeStruct((B, S // tm, 1, LANES), F32),
        ],
        scratch_shapes=[
            pltpu.VMEM((3 * P, tm, LANES), F32),
            pltpu.VMEM((3 * P, 4, tm // 4, LANES), F32),
        ],
        compiler_params=_params(("parallel", "parallel")),
        name="in_proj",
    )(x, g_mix, w_in, wq_t, wv_t)


def _diff_attn_kernel(slopes_ref, lam_ref, g_ref, nrm_ref, qt_ref, k_ref, vt_ref, o_ref,
                      q2t_ref, m_ref, acc_ref, cb_ref, *, bq, bk, lam_init):
    G = cb_ref.shape[1]
    h = pl.program_id(1)
    qi = pl.program_id(2)
    nkv = k_ref.shape[1] // bk
    slope2 = slopes_ref[h] * LOG2E

    qt = qt_ref[0]
    sub = lax.broadcasted_iota(jnp.int32, qt.shape, 0)
    zero = jnp.zeros_like(qt)
    q2t_ref[:, 0:bq] = jnp.where(sub < HEAD_DIM, qt, zero)
    q2t_ref[:, bq:2 * bq] = jnp.where(sub >= HEAD_DIM, qt, zero)
    m_ref[...] = jnp.full(m_ref.shape, -jnp.inf, F32)
    acc_ref[...] = jnp.zeros(acc_ref.shape, F32)
    key_off = lax.broadcasted_iota(jnp.int32, (bk, G), 0).astype(F32)
    cb_ref[...] = slope2 * key_off
    lane = lax.broadcasted_iota(jnp.int32, (1, 2 * bq), 1)
    pos_q = (qi * bq + jnp.where(lane >= bq, lane - bq, lane)).astype(F32)
    ones = jnp.ones((ONES_ROWS, bk), BF16)

    q_rows = nrm_ref.shape[1] * bq // k_ref.shape[1]
    k_big = jnp.max(nrm_ref[0, :, 0, :], axis=0, keepdims=True)
    q_big = jnp.max(nrm_ref[0, pl.ds(qi * q_rows, q_rows), 0, :], axis=0, keepdims=True)
    nlane = lax.broadcasted_iota(jnp.int32, k_big.shape, 1)
    k_max = jnp.max(jnp.where(nlane == NRM_DIFF_K + h, k_big, 0.0))
    q_mine = (nlane == NRM_DIFF_Q + 2 * h) | (nlane == NRM_DIFF_Q + 2 * h + 1)
    q_max = jnp.max(jnp.where(q_mine, q_big, 0.0))
    direct = q_max * k_max <= DIRECT_EXP_MAX_LOGIT ** 2

    def online_tile(j, side):
        start = pl.multiple_of(j * bk, bk)
        k_t = k_ref[0, pl.ds(start, bk), :]
        vt1 = jnp.concatenate([vt_ref[0, 0, j], ones], axis=0)
        j0 = (j * bk).astype(F32)
        groups = [slice(gi * G, (gi + 1) * G) for gi in range(2 * bq // G)]
        scores = [jnp.dot(k_t, q2t_ref[:, sl], preferred_element_type=F32) for sl in groups]
        for sl, s in zip(groups, scores):
            pq = pos_q[:, sl]
            if side == 0:
                u = s - slope2 * jnp.abs(pq - (j0 + key_off))
                a = jnp.zeros_like(pq)
            elif side < 0:
                u = s + cb_ref[...]
                a = slope2 * (j0 - pq)
            else:
                u = s - cb_ref[...]
                a = slope2 * (pq - j0)
            m_old = m_ref[:, sl]
            m_new = jnp.maximum(m_old, jnp.max(u, axis=0, keepdims=True) + a)
            alpha = jnp.exp2(m_old - m_new)
            p = jnp.exp2(u - (m_new - a)).astype(BF16)
            acc_ref[:, sl] = alpha * acc_ref[:, sl] + jnp.dot(
                vt1, p, preferred_element_type=F32)
            m_ref[:, sl] = m_new

    def direct_pair(j, side):
        groups = [slice(gi * G, (gi + 1) * G) for gi in range(2 * bq // G)]

        def operands(jj):
            k_t = k_ref[0, pl.ds(pl.multiple_of(jj * bk, bk), bk), :]
            return k_t, jnp.concatenate([vt_ref[0, 0, jj], ones], axis=0), (jj * bk).astype(F32)

        def score(k_t, sl):
            return jnp.dot(k_t, q2t_ref[:, sl], preferred_element_type=F32)

        def pv(vt1, j0, sl, s):
            pq = pos_q[:, sl]
            if side == 0:
                e = s - slope2 * jnp.abs(pq - (j0 + key_off))
            elif side < 0:
                e = s + cb_ref[...] + slope2 * (j0 - pq)
            else:
                e = s - cb_ref[...] + slope2 * (pq - j0)
            return jnp.dot(vt1, jnp.exp2(e).astype(BF16), preferred_element_type=F32)

        k_a, vt_a, j0_a = operands(j)
        k_b, vt_b, j0_b = operands(j + 1)
        s_a = [score(k_a, sl) for sl in groups]
        pv_a, s_b = [], []
        for sl, s in zip(groups, s_a):
            pv_a.append(pv(vt_a, j0_a, sl, s))
            s_b.append(score(k_b, sl))
        pv_b = [pv(vt_b, j0_b, sl, s) for sl, s in zip(groups, s_b)]
        for sl, x, y in zip(groups, pv_a, pv_b):
            acc_ref[:, sl] += x + y

    def all_tiles(online):
        step = 1 if online else 2

        def one(j, side):
            if online:
                online_tile(j, side)
            else:
                direct_pair(j, side)

        def before(t, carry):
            one(t * step, -1)
            return carry

        def after(t, carry):
            one(t * step, 1)
            return carry

        ndiag = bq // bk
        assert ndiag % step == 0 and nkv % step == 0
        lax.fori_loop(0, qi * ndiag // step, before, 0)
        for d in range(0, ndiag, step):
            one(qi * ndiag + d, 0)
        lax.fori_loop((qi + 1) * ndiag // step, nkv // step, after, 0)

    @pl.when(direct)
    def _():
        all_tiles(False)

    @pl.when(jnp.logical_not(direct))
    def _():
        all_tiles(True)

    lq = lam_ref[...]
    lam = (jnp.exp(jnp.sum(lq[0:1] * lq[1:2], axis=1, keepdims=True))
           - jnp.exp(jnp.sum(lq[2:3] * lq[3:4], axis=1, keepdims=True)) + lam_init)
    acc = acc_ref[...]
    ot = acc[0:LANES] * (1.0 / acc[LANES:LANES + 1])
    od = (ot[:, 0:bq] - lam * ot[:, bq:2 * bq]).T
    od = od * lax.rsqrt(jnp.mean(od * od, axis=-1, keepdims=True) + EPS)
    od = od * g_ref[pl.ds(h, 1), :] * (1.0 - lam_init)
    o_ref[0] = od.astype(BF16)


def _diff_attn(qt, k, vt, nrm, slopes, lam_qk, g_diff, lam_init, bq):
    B, S, _ = k.shape
    H = N_DIFF_HEADS
    nkv, bk = vt.shape[2], vt.shape[4]
    assert bq % bk == 0
    kern = functools.partial(_diff_attn_kernel, bq=bq, bk=bk, lam_init=lam_init)
    return pl.pallas_call(
        kern,
        grid=(B, H, S // bq),
        in_specs=[
            pl.BlockSpec(memory_space=pltpu.SMEM),
            pl.BlockSpec((4, HEAD_DIM), lambda b, h, i: (0, 0)),
            pl.BlockSpec((H, LANES), lambda b, h, i: (0, 0)),
            pl.BlockSpec((1, nrm.shape[1], 1, LANES), lambda b, h, i: (b, 0, 0, 0)),
            pl.BlockSpec((1, LANES, bq), lambda b, h, i: (b, h, i)),
            pl.BlockSpec((1, S, LANES), lambda b, h, i: (b, 0, h)),
            pl.BlockSpec((1, 1, nkv, LANES, bk), lambda b, h, i: (b, h, 0, 0, 0)),
        ],
        out_specs=pl.BlockSpec((1, bq, LANES), lambda b, h, i: (b, i, h)),
        out_shape=jax.ShapeDtypeStruct((B, S, H * LANES), BF16),
        scratch_shapes=[
            pltpu.VMEM((LANES, 2 * bq), BF16),
            pltpu.VMEM((1, 2 * bq), F32),
            pltpu.VMEM((LANES + ONES_ROWS, 2 * bq), F32),
            pltpu.VMEM((bk, 8 * LANES), F32),
        ],
        compiler_params=_params(("parallel", "parallel", "arbitrary")),
        name="diff_attn",
    )(slopes, lam_qk, g_diff, nrm, qt, k, vt)


def _dil_bias(dil, slope_a, slope_b):
    QB, R = LANES, DIL_RADIUS
    KW = QB + 2 * R
    r_i = lax.broadcasted_iota(jnp.int32, (QB, KW), 0)
    c_i = lax.broadcasted_iota(jnp.int32, (QB, KW), 1)
    rel = c_i - r_i - R
    band = (rel >= -R) & (rel <= R)
    dist = (jnp.abs(rel) * dil).astype(F32)
    return jnp.concatenate([jnp.where(band, -(slope_a * LOG2E) * dist, NEG),
                            jnp.where(band, -(slope_b * LOG2E) * dist, NEG)], axis=0)


def _dil_segment(q_at, k_at, v_at, nres, L, dil, tok0, bias2, first, last, online,
                 sa_ref, sb_ref, m_ref, o_ref, kpad, vpad_a, vpad_b):
    QB, R = LANES, DIL_RADIUS
    KW = QB + 2 * R
    pitch = L + 2 * R
    per_class = L // QB
    lane_q = lax.broadcasted_iota(jnp.int32, (QB, LANES), 1)
    head_a = lane_q < HEAD_DIM
    c2 = lax.broadcasted_iota(jnp.int32, (2 * QB, KW), 1)

    zpad = jnp.zeros((R, LANES), BF16)
    own_a = lax.broadcasted_iota(jnp.int32, (L, LANES), 1) < HEAD_DIM
    one = jnp.ones((L, LANES), BF16)
    for c in range(nres):
        base = c * pitch
        for pad in (kpad, vpad_a, vpad_b):
            pad[base:base + R, :] = zpad
            pad[base + R + L:base + pitch, :] = zpad
        kpad[base + R:base + R + L, :] = k_at(c)
        v_src = v_at(c)
        vpad_a[base + R:base + R + L, :] = jnp.where(own_a, v_src, one)
        vpad_b[base + R:base + R + L, :] = jnp.where(own_a, one, v_src)

    def rows(c, i0):
        if dil == 1:
            return pl.ds(i0, QB)
        return pl.ds(tok0 + c + dil * i0, QB, stride=dil)

    def scores(c, i0):
        q = q_at(c, i0)
        zero = jnp.zeros_like(q)
        q2 = jnp.concatenate([jnp.where(head_a, q, zero), jnp.where(head_a, zero, q)], axis=0)
        s = lax.dot_general(q2, kpad[pl.ds(c * pitch + i0, KW), :], (((1,), (1,)), ((), ())),
                            preferred_element_type=F32)
        return s + bias2

    def finish(c, i0, s):
        if online:
            key = i0 - R + c2
            s = jnp.where((key >= 0) & (key < L), s, NEG)
            m_c = jnp.max(s, axis=1, keepdims=True)
            s = s - m_c
        p = jnp.exp2(s).astype(BF16)
        win = pl.ds(c * pitch + i0, KW)
        sa = jnp.dot(p[0:QB], vpad_a[win, :], preferred_element_type=F32)
        sb = jnp.dot(p[QB:2 * QB], vpad_b[win, :], preferred_element_type=F32)
        r = rows(c, i0)
        if online:
            ma = jnp.broadcast_to(m_c[0:QB], (QB, LANES))
            mb = jnp.broadcast_to(m_c[QB:2 * QB], (QB, LANES))
            if not first:
                m_p = m_ref[r, :]
                mpa = jnp.max(jnp.where(head_a, m_p, -jnp.inf), axis=1, keepdims=True)
                mpb = jnp.max(jnp.where(head_a, -jnp.inf, m_p), axis=1, keepdims=True)
                mna, mnb = jnp.maximum(mpa, ma), jnp.maximum(mpb, mb)
                sa = sa_ref[r, :] * jnp.exp2(mpa - mna) + sa * jnp.exp2(ma - mna)
                sb = sb_ref[r, :] * jnp.exp2(mpb - mnb) + sb * jnp.exp2(mb - mnb)
                ma, mb = mna, mnb
            if not last:
                m_ref[r, :] = jnp.where(head_a, ma, mb)
        elif not first:
            sa = sa_ref[r, :] + sa
            sb = sb_ref[r, :] + sb
        if last:
            half = LANES // 2
            o_ref[0, 0, r, :] = jnp.where(head_a, sa / pltpu.roll(sa, half, 1),
                                          sb / pltpu.roll(sb, half, 1))
        else:
            sa_ref[r, :] = sa
            sb_ref[r, :] = sb

    def blocks(todo):
        ss = [scores(c, i0) for c, i0 in todo]
        for (c, i0), s in zip(todo, ss):
            finish(c, i0, s)

    if online:
        for c in range(nres):
            def body(n, carry, c=c):
                blocks([(c, pl.multiple_of(n * QB, QB))])
                return carry
            lax.fori_loop(0, per_class, body, 0)
    elif nres * per_class <= DIL_MAX_BLOCKS:
        blocks([(c, u * QB) for c in range(nres) for u in range(per_class)])
    else:
        assert nres == 1 and per_class % DIL_MAX_BLOCKS == 0

        def body(n, carry):
            blocks([(0, pl.multiple_of((n * DIL_MAX_BLOCKS + u) * QB, QB))
                    for u in range(DIL_MAX_BLOCKS)])
            return carry

        lax.fori_loop(0, per_class // DIL_MAX_BLOCKS, body, 0)


def _dil_kernel(slopes_ref, nrm_ref, pb1_ref, pb4_ref, pb16_ref, o_ref,
                sa_ref, sb_ref, m_ref, kpad, vpad_a, vpad_b, *, group16):
    hp = pl.program_id(1)
    t = pl.program_id(2)
    S = pb1_ref.shape[2]
    n16 = 16 // group16
    slope_a = slopes_ref[2 * hp]
    slope_b = slopes_ref[2 * hp + 1]

    big = jnp.max(nrm_ref[0, :, 0, :], axis=0, keepdims=True)
    lane = lax.broadcasted_iota(jnp.int32, big.shape, 1)
    def pick(idx):
        return jnp.max(jnp.where(lane == idx, big, 0.0))

    direct = ((pick(4 * hp) * pick(4 * hp + 1) <= DIRECT_EXP_MAX_LOGIT ** 2)
              & (pick(4 * hp + 2) * pick(4 * hp + 3) <= DIRECT_EXP_MAX_LOGIT ** 2))

    def pattern(online):
        args = (sa_ref, sb_ref, m_ref, o_ref, kpad, vpad_a, vpad_b)

        @pl.when(t < n16)
        def _():
            _dil_segment(lambda c, i0: pb16_ref[0, 0, c, pl.ds(i0, LANES), 0:LANES],
                         lambda c: pb16_ref[0, 0, c, :, LANES:2 * LANES],
                         lambda c: pb16_ref[0, 0, c, :, 2 * LANES:3 * LANES],
                         group16, S // 16, 16, t * group16, _dil_bias(16, slope_a, slope_b),
                         True, False, online, *args)

        @pl.when((t >= n16) & (t < n16 + 4))
        def _():
            _dil_segment(lambda c, i0: pb4_ref[0, 0, 0, pl.ds(i0, LANES), 0:LANES],
                         lambda c: pb4_ref[0, 0, 0, :, LANES:2 * LANES],
                         lambda c: pb4_ref[0, 0, 0, :, 2 * LANES:3 * LANES],
                         1, S // 4, 4, t - n16, _dil_bias(4, slope_a, slope_b),
                         False, False, online, *args)

        @pl.when(t == n16 + 4)
        def _():
            _dil_segment(lambda c, i0: pb1_ref[0, 0, pl.ds(i0, LANES), 0:LANES],
                         lambda c: pb1_ref[0, 0, :, LANES:2 * LANES],
                         lambda c: pb1_ref[0, 0, :, 2 * LANES:3 * LANES],
                         1, S, 1, 0, _dil_bias(1, slope_a, slope_b),
                         False, True, online, *args)

    @pl.when(direct)
    def _():
        pattern(False)

    @pl.when(jnp.logical_not(direct))
    def _():
        pattern(True)


def _dil_attn(pb1, pb4, pb16, nrm, slopes, group16=4):
    B, P, S, _ = pb1.shape
    n16 = 16 // group16
    nt = nrm.shape[1]
    return pl.pallas_call(
        functools.partial(_dil_kernel, group16=group16),
        grid=(B, P, n16 + 4 + 1),
        in_specs=[
            pl.BlockSpec(memory_space=pltpu.SMEM),
            pl.BlockSpec((1, nt, 1, LANES), lambda b, p, t: (b, 0, 0, 0)),
            pl.BlockSpec((1, 1, S, PAIR_W), lambda b, p, t: (b, p, 0, 0)),
            pl.BlockSpec((1, 1, 1, S // 4, PAIR_W),
                         lambda b, p, t: (b, p, jnp.clip(t - n16, 0, 3), 0, 0)),
            pl.BlockSpec((1, 1, group16, S // 16, PAIR_W),
                         lambda b, p, t: (b, p, jnp.minimum(t, n16 - 1), 0, 0)),
        ],
        out_specs=pl.BlockSpec((1, 1, S, LANES), lambda b, p, t: (b, p, 0, 0)),
        out_shape=jax.ShapeDtypeStruct((B, P, S, LANES), F32),
        scratch_shapes=[
            pltpu.VMEM((S, LANES), F32),
            pltpu.VMEM((S, LANES), F32),
            pltpu.VMEM((S, LANES), F32),
            pltpu.VMEM((S + 2 * DIL_RADIUS, LANES), BF16),
            pltpu.VMEM((S + 2 * DIL_RADIUS, LANES), BF16),
            pltpu.VMEM((S + 2 * DIL_RADIUS, LANES), BF16),
        ],
        compiler_params=_params(("parallel", "parallel", "arbitrary")),
        name="dil_attn",
    )(slopes, nrm, pb1, pb4, pb16)


def _out_mlp_kernel(x_ref, oa_ref, ob_ref, gdil_ref, wout_ref, gmlp_ref, wup_ref,
                    wdown_ref, gfin_ref, y_ref, *, ff_chunk):
    ob = jnp.concatenate([ob_ref[0, p] for p in range(N_DIL_PAIRS)], axis=-1)
    ob = _rms(ob, gdil_ref[...])
    mix = jnp.concatenate([oa_ref[0], ob.astype(BF16)], axis=-1)
    x1 = x_ref[0] + jnp.dot(mix, wout_ref[...], preferred_element_type=F32)
    h = _rms(x1, gmlp_ref[...]).astype(BF16)
    y = x1
    for c in range(D_FF // ff_chunk):
        u = jnp.dot(h, wup_ref[:, c * ff_chunk:(c + 1) * ff_chunk], preferred_element_type=F32)
        u = jnp.square(jnp.maximum(u, 0.0)).astype(BF16)
        y = y + jnp.dot(u, wdown_ref[c * ff_chunk:(c + 1) * ff_chunk, :],
                        preferred_element_type=F32)
    y_ref[0] = _rms(y, gfin_ref[...])


def _out_mlp(x, oa, ob, g_dil, w_out, g_mlp, w_up, w_down, g_final, tm, ff_chunk):
    B, S, D = x.shape
    const = lambda shape: pl.BlockSpec(shape, lambda b, i: (0,) * len(shape),
                                       pipeline_mode=pl.Buffered(1))
    return pl.pallas_call(
        functools.partial(_out_mlp_kernel, ff_chunk=ff_chunk),
        grid=(B, S // tm),
        in_specs=[
            pl.BlockSpec((1, tm, D), lambda b, i: (b, i, 0)),
            pl.BlockSpec((1, tm, N_DIFF_HEADS * LANES), lambda b, i: (b, i, 0)),
            pl.BlockSpec((1, N_DIL_PAIRS, tm, LANES), lambda b, i: (b, 0, i, 0)),
            const((1, DIL_W)),
            const((2 * DIL_W, D)),
            const((1, D)),
            const((D, D_FF)),
            const((D_FF, D)),
            const((1, D)),
        ],
        out_specs=pl.BlockSpec((1, tm, D), lambda b, i: (b, i, 0)),
        out_shape=jax.ShapeDtypeStruct((B, S, D), F32),
        compiler_params=_params(("parallel", "parallel")),
        name="out_mlp",
    )(x, oa, ob, g_dil, w_out, g_mlp, w_up, w_down, g_final)


def _alibi_slopes(n):
    return 2.0 ** (-8.0 * jnp.arange(1, n + 1, dtype=F32) / n)


def _layer(x, g_mix, w_in, wq_t, wv_t, lam_qk, g_diff, g_dil, w_out, g_mlp, w_up, w_down,
           g_final, lam_init):
    qt, k, vt, pb1, pb4, pb16, nrm = _in_proj(x, g_mix, w_in, wq_t, wv_t, tm=512)
    oa = _diff_attn(qt, k, vt, nrm, _alibi_slopes(N_DIFF_HEADS), lam_qk, g_diff, lam_init,
                    bq=1024)
    assert DIL_PATTERNS == ((2 * DIL_RADIUS, 1), (8 * DIL_RADIUS, 4), (32 * DIL_RADIUS, 16))
    ob = _dil_attn(pb1, pb4, pb16, nrm, _alibi_slopes(N_DIL_HEADS))
    return _out_mlp(x, oa, ob, g_dil, w_out, g_mlp, w_up, w_down, g_final,
                    tm=512, ff_chunk=1024)


def kernel(x_prompt, x_sample, g_mix, w_in, lam_qk, g_diff, g_dil, w_out, g_mlp, w_up,
           w_down, g_final):
    assert g_mix.shape[0] == 1
    lam_init = 0.8 - 0.6 * math.exp(-0.3 * 0)
    w_in_b = w_in[0].astype(BF16)
    weights = (
        g_mix[0][None, :], w_in_b, w_in_b[:, 0:DIFF_QW].T, w_in_b[:, 2 * DIFF_QW:DIFF_W].T,
        lam_qk[0],
        g_diff[0], g_dil[0][None, :], w_out[0].astype(BF16), g_mlp[0][None, :],
        w_up[0].astype(BF16), w_down[0].astype(BF16), g_final[None, :],
    )
    return tuple(_layer(x, *weights, lam_init) for x in (x_prompt, x_sample))
```

```python
import functools
import math

import jax
import jax.numpy as jnp
from jax import lax
from jax.experimental import pallas as pl
from jax.experimental.pallas import tpu as pltpu

D_MODEL = 1024
HEAD_DIM = 64
N_DIFF_HEADS = 4
N_DIL_HEADS = 8
N_DIL_PAIRS = N_DIL_HEADS // 2
DIL_PATTERNS = ((128, 1), (512, 4), (2048, 16))
DIL_RADIUS = 64
D_FF = 4 * D_MODEL
EPS = 1e-5
DIFF_QW = N_DIFF_HEADS * 2 * HEAD_DIM
DIFF_W = 3 * DIFF_QW
DIL_W = N_DIL_HEADS * HEAD_DIM
IN_W = DIFF_W + 3 * DIL_W
LANES = 128
PAIR_W = 3 * LANES
NEG = -1e30
LOG2E = math.log2(math.e)
ONES_ROWS = 16
DIRECT_EXP_MAX_LOGIT = 60.0
BIAS_ROWS = 16
DIL_MAX_BLOCKS = 16
NRM_DIFF_K = 4 * N_DIL_PAIRS
NRM_DIFF_Q = NRM_DIFF_K + N_DIFF_HEADS
VMEM_LIMIT = 56 * 1024 * 1024

BF16 = jnp.bfloat16
F32 = jnp.float32


def _rms(x, g):
    return x * lax.rsqrt(jnp.mean(x * x, axis=-1, keepdims=True) + EPS) * g


def _params(sem):
    return pltpu.CompilerParams(dimension_semantics=sem, vmem_limit_bytes=VMEM_LIMIT)


def _in_proj_kernel(x_ref, g_ref, w_ref, wqt_ref, wvt_ref, qt_ref, k_ref, vt_ref,
                    pb1_ref, pb4_ref, pb16_ref, nrm_ref, slab_ref, slab4_ref):
    scale = HEAD_DIM ** -0.5
    nt = (((1,), (1,)), ((), ()))
    tm = x_ref.shape[1]
    lane = lax.broadcasted_iota(jnp.int32, (1, LANES), 1)
    nrm = jnp.zeros((1, LANES), F32)
    h = _rms(x_ref[0], g_ref[...]).astype(BF16)
    for c in range(3):
        p = jnp.dot(h, w_ref[:, DIFF_W + c * DIL_W:DIFF_W + (c + 1) * DIL_W],
                    preferred_element_type=F32)
        if c == 0:
            p = p * (scale * LOG2E)
        cols = slice(c * LANES, (c + 1) * LANES)
        for hp in range(N_DIL_PAIRS):
            ph = p[:, hp * LANES:(hp + 1) * LANES]
            if c < 2:
                sq = ph * ph
                for e in range(2):
                    mine = (lane >= e * HEAD_DIM) & (lane < (e + 1) * HEAD_DIM)
                    big = jnp.max(jnp.sum(jnp.where(mine, sq, 0.0), axis=1, keepdims=True),
                                  axis=0, keepdims=True)
                    nrm = jnp.where(lane == 4 * hp + 2 * e + c, big, nrm)
            pb1_ref[0, hp, :, cols] = ph.astype(BF16)
            s = c * N_DIL_PAIRS + hp
            slab_ref[s] = ph
            for g in range(4):
                v4 = slab_ref[s, pl.ds(g, tm // 4, stride=4), :]
                pb4_ref[0, hp, g, :, cols] = v4.astype(BF16)
                slab4_ref[s, g] = v4
            for g in range(4):
                for c2 in range(4):
                    v16 = slab4_ref[s, g, pl.ds(c2, tm // 16, stride=4), :]
                    pb16_ref[0, hp, 4 * c2 + g, :, cols] = v16.astype(BF16)
    kd = jnp.dot(h, w_ref[:, DIFF_QW:2 * DIFF_QW], preferred_element_type=F32)
    k_ref[0] = kd.astype(BF16)
    for hh in range(N_DIFF_HEADS):
        kh = kd[:, hh * LANES:(hh + 1) * LANES]
        big = jnp.max(jnp.sum(kh * kh, axis=1, keepdims=True), axis=0, keepdims=True)
        nrm = jnp.where(lane == NRM_DIFF_K + hh, big, nrm)
    qt = lax.dot_general(wqt_ref[...], h, nt, preferred_element_type=F32) * (scale * LOG2E)
    qt_ref[0] = qt.astype(BF16)
    for hm in range(2 * N_DIFF_HEADS):
        qm = qt[hm * HEAD_DIM:(hm + 1) * HEAD_DIM, :]
        big = jnp.max(jnp.sum(qm * qm, axis=0, keepdims=True), axis=1, keepdims=True)
        nrm = jnp.where(lane == NRM_DIFF_Q + hm, big, nrm)
    nrm_ref[0, 0] = nrm
    vt = lax.dot_general(wvt_ref[...], h, nt, preferred_element_type=F32).astype(BF16)
    for hh in range(N_DIFF_HEADS):
        vt_ref[0, hh, 0] = vt[hh * LANES:(hh + 1) * LANES, :]


def _in_proj(x, g_mix, w_in, wq_t, wv_t, tm):
    B, S, D = x.shape
    H = N_DIFF_HEADS
    P = N_DIL_PAIRS
    return pl.pallas_call(
        _in_proj_kernel,
        grid=(B, S // tm),
        in_specs=[
            pl.BlockSpec((1, tm, D), lambda b, i: (b, i, 0)),
            pl.BlockSpec((1, D), lambda b, i: (0, 0)),
            pl.BlockSpec((D, IN_W), lambda b, i: (0, 0)),
            pl.BlockSpec((DIFF_QW, D), lambda b, i: (0, 0)),
            pl.BlockSpec((DIFF_QW, D), lambda b, i: (0, 0)),
        ],
        out_specs=[
            pl.BlockSpec((1, DIFF_QW, tm), lambda b, i: (b, 0, i)),
            pl.BlockSpec((1, tm, DIFF_QW), lambda b, i: (b, i, 0)),
            pl.BlockSpec((1, H, 1, LANES, tm), lambda b, i: (b, 0, i, 0, 0)),
            pl.BlockSpec((1, P, tm, PAIR_W), lambda b, i: (b, 0, i, 0)),
            pl.BlockSpec((1, P, 4, tm // 4, PAIR_W), lambda b, i: (b, 0, 0, i, 0)),
            pl.BlockSpec((1, P, 16, tm // 16, PAIR_W), lambda b, i: (b, 0, 0, i, 0)),
            pl.BlockSpec((1, 1, 1, LANES), lambda b, i: (b, i, 0, 0)),
        ],
        out_shape=[
            jax.ShapeDtypeStruct((B, DIFF_QW, S), BF16),
            jax.ShapeDtypeStruct((B, S, DIFF_QW), BF16),
            jax.ShapeDtypeStruct((B, H, S // tm, LANES, tm), BF16),
            jax.ShapeDtypeStruct((B, P, S, PAIR_W), BF16),
            jax.ShapeDtypeStruct((B, P, 4, S // 4, PAIR_W), BF16),
            jax.ShapeDtypeStruct((B, P, 16, S // 16, PAIR_W), BF16),
            jax.ShapeDtypeStruct((B, S // tm, 1, LANES), F32),
        ],
        scratch_shapes=[
            pltpu.VMEM((3 * P, tm, LANES), F32),
            pltpu.VMEM((3 * P, 4, tm // 4, LANES), F32),
        ],
        compiler_params=_params(("parallel", "parallel")),
        name="in_proj",
    )(x, g_mix, w_in, wq_t, wv_t)


def _diff_attn_kernel(slopes_ref, lam_ref, g_ref, nrm_ref, qt_ref, k_ref, vt_ref, o_ref,
                      q2t_ref, m_ref, acc_ref, cb_ref, feat_ref, *, bq, bk, lam_init):
    G = cb_ref.shape[1]
    h = pl.program_id(1)
    qi = pl.program_id(2)
    nkv = k_ref.shape[1] // bk
    slope2 = slopes_ref[h] * LOG2E

    qt = qt_ref[0]
    sub = lax.broadcasted_iota(jnp.int32, qt.shape, 0)
    zero = jnp.zeros_like(qt)
    q2t_ref[0:LANES, 0:bq] = jnp.where(sub < HEAD_DIM, qt, zero)
    q2t_ref[0:LANES, bq:2 * bq] = jnp.where(sub >= HEAD_DIM, qt, zero)
    q2t_ref[LANES + 2 * BIAS_ROWS:2 * LANES, :] = jnp.zeros(
        (LANES - 2 * BIAS_ROWS, 2 * bq), BF16)
    f_row = lax.broadcasted_iota(jnp.int32, (bk, LANES), 0)
    f_lane = lax.broadcasted_iota(jnp.int32, (bk, LANES), 1)
    f_slot = f_lane % BIAS_ROWS
    feat = jnp.where(f_slot < 2, f_row % 256,
                     jnp.where(f_slot < 4, f_row // 256,
                               jnp.where(f_slot < 7, 1, 0)))
    for tt in range(2):
        mine = f_lane // BIAS_ROWS == tt
        feat_ref[tt] = jnp.where(mine, feat, 0).astype(F32).astype(BF16)
    m_ref[...] = jnp.full(m_ref.shape, -jnp.inf, F32)
    acc_ref[...] = jnp.zeros(acc_ref.shape, F32)
    key_off = lax.broadcasted_iota(jnp.int32, (bk, G), 0).astype(F32)
    cb_ref[...] = slope2 * key_off
    lane = lax.broadcasted_iota(jnp.int32, (1, 2 * bq), 1)
    pos_q = (qi * bq + jnp.where(lane >= bq, lane - bq, lane)).astype(F32)
    ones = jnp.ones((ONES_ROWS, bk), BF16)

    q_rows = nrm_ref.shape[1] * bq // k_ref.shape[1]
    k_big = jnp.max(nrm_ref[0, :, 0, :], axis=0, keepdims=True)
    q_big = jnp.max(nrm_ref[0, pl.ds(qi * q_rows, q_rows), 0, :], axis=0, keepdims=True)
    nlane = lax.broadcasted_iota(jnp.int32, k_big.shape, 1)
    k_max = jnp.max(jnp.where(nlane == NRM_DIFF_K + h, k_big, 0.0))
    q_mine = (nlane == NRM_DIFF_Q + 2 * h) | (nlane == NRM_DIFF_Q + 2 * h + 1)
    q_max = jnp.max(jnp.where(q_mine, q_big, 0.0))
    direct = q_max * k_max <= DIRECT_EXP_MAX_LOGIT ** 2

    def online_tile(j, side):
        start = pl.multiple_of(j * bk, bk)
        k_t = k_ref[0, pl.ds(start, bk), :]
        vt1 = jnp.concatenate([vt_ref[0, 0, j], ones], axis=0)
        j0 = (j * bk).astype(F32)
        groups = [slice(gi * G, (gi + 1) * G) for gi in range(2 * bq // G)]
        scores = [jnp.dot(k_t, q2t_ref[0:LANES, sl], preferred_element_type=F32)
                  for sl in groups]
        for sl, s in zip(groups, scores):
            pq = pos_q[:, sl]
            if side == 0:
                u = s - slope2 * jnp.abs(pq - (j0 + key_off))
                a = jnp.zeros_like(pq)
            elif side < 0:
                u = s + cb_ref[...]
                a = slope2 * (j0 - pq)
            else:
                u = s - cb_ref[...]
                a = slope2 * (pq - j0)
            m_old = m_ref[:, sl]
            m_new = jnp.maximum(m_old, jnp.max(u, axis=0, keepdims=True) + a)
            alpha = jnp.exp2(m_old - m_new)
            p = jnp.exp2(u - (m_new - a)).astype(BF16)
            acc_ref[:, sl] = alpha * acc_ref[:, sl] + jnp.dot(
                vt1, p, preferred_element_type=F32)
            m_ref[:, sl] = m_new

    def direct_pair(j, side):
        groups = [slice(gi * G, (gi + 1) * G) for gi in range(2 * bq // G)]

        def bias_rows(j0):
            def bf16_part(x):
                return x.astype(BF16).astype(F32)

            sgn = -float(side)
            sv = jnp.zeros_like(pos_q) + sgn * slope2
            s_hi = bf16_part(sv)
            s_lo = bf16_part(sv - s_hi)
            a = sv * (j0 - pos_q)
            a1 = bf16_part(a)
            a2 = bf16_part(a - a1)
            a3 = bf16_part(a - a1 - a2)
            pieces = [s_hi, s_lo, s_hi * 256.0, s_lo * 256.0, a1, a2, a3]
            row = lax.broadcasted_iota(jnp.int32, (BIAS_ROWS, 2 * bq), 0)
            w = jnp.zeros((BIAS_ROWS, 2 * bq), F32)
            for i, piece in enumerate(pieces):
                w = jnp.where(row == i, piece, w)
            return w.astype(BF16)

        def operands(tt):
            jj = j + tt
            k_t = k_ref[0, pl.ds(pl.multiple_of(jj * bk, bk), bk), :]
            j0 = (jj * bk).astype(F32)
            if side != 0:
                q2t_ref[LANES + tt * BIAS_ROWS:LANES + (tt + 1) * BIAS_ROWS, :] = bias_rows(j0)
                k_t = jnp.concatenate([k_t, feat_ref[tt]], axis=1)
            return k_t, vt_ref[0, 0, jj], j0

        def score(k_t, sl):
            rows = slice(0, k_t.shape[1])
            return jnp.dot(k_t, q2t_ref[rows, sl], preferred_element_type=F32)

        def pv(vt, j0, sl, s):
            if side == 0:
                s = s - slope2 * jnp.abs(pos_q[:, sl] - (j0 + key_off))
            p = jnp.exp2(s)
            return (jnp.dot(vt, p.astype(BF16), preferred_element_type=F32),
                    jnp.sum(p, axis=0, keepdims=True))

        k_a, vt_a, j0_a = operands(0)
        k_b, vt_b, j0_b = operands(1)
        s_a = [score(k_a, sl) for sl in groups]
        pv_a, s_b = [], []
        for sl, s in zip(groups, s_a):
            pv_a.append(pv(vt_a, j0_a, sl, s))
            s_b.append(score(k_b, sl))
        pv_b = [pv(vt_b, j0_b, sl, s) for sl, s in zip(groups, s_b)]
        for sl, (x, lx), (y, ly) in zip(groups, pv_a, pv_b):
            acc_ref[0:LANES, sl] += x + y
            acc_ref[LANES:LANES + 1, sl] += lx + ly

    def all_tiles(online):
        step = 1 if online else 2

        def one(j, side):
            if online:
                online_tile(j, side)
            else:
                direct_pair(j, side)

        def before(t, carry):
            one(t * step, -1)
            return carry

        def after(t, carry):
            one(t * step, 1)
            return carry

        ndiag = bq // bk
        assert ndiag % step == 0 and nkv % step == 0
        lax.fori_loop(0, qi * ndiag // step, before, 0)
        for d in range(0, ndiag, step):
            one(qi * ndiag + d, 0)
        lax.fori_loop((qi + 1) * ndiag // step, nkv // step, after, 0)

    @pl.when(direct)
    def _():
        all_tiles(False)

    @pl.when(jnp.logical_not(direct))
    def _():
        all_tiles(True)

    lq = lam_ref[...]
    lam = (jnp.exp(jnp.sum(lq[0:1] * lq[1:2], axis=1, keepdims=True))
           - jnp.exp(jnp.sum(lq[2:3] * lq[3:4], axis=1, keepdims=True)) + lam_init)
    acc = acc_ref[...]
    ot = acc[0:LANES] * (1.0 / acc[LANES:LANES + 1])
    od = (ot[:, 0:bq] - lam * ot[:, bq:2 * bq]).T
    od = od * lax.rsqrt(jnp.mean(od * od, axis=-1, keepdims=True) + EPS)
    od = od * g_ref[pl.ds(h, 1), :] * (1.0 - lam_init)
    o_ref[0] = od.astype(BF16)


def _diff_attn(qt, k, vt, nrm, slopes, lam_qk, g_diff, lam_init, bq):
    B, S, _ = k.shape
    H = N_DIFF_HEADS
    nkv, bk = vt.shape[2], vt.shape[4]
    assert bq % bk == 0
    kern = functools.partial(_diff_attn_kernel, bq=bq, bk=bk, lam_init=lam_init)
    return pl.pallas_call(
        kern,
        grid=(B, H, S // bq),
        in_specs=[
            pl.BlockSpec(memory_space=pltpu.SMEM),
            pl.BlockSpec((4, HEAD_DIM), lambda b, h, i: (0, 0)),
            pl.BlockSpec((H, LANES), lambda b, h, i: (0, 0)),
            pl.BlockSpec((1, nrm.shape[1], 1, LANES), lambda b, h, i: (b, 0, 0, 0)),
            pl.BlockSpec((1, LANES, bq), lambda b, h, i: (b, h, i)),
            pl.BlockSpec((1, S, LANES), lambda b, h, i: (b, 0, h)),
            pl.BlockSpec((1, 1, nkv, LANES, bk), lambda b, h, i: (b, h, 0, 0, 0)),
        ],
        out_specs=pl.BlockSpec((1, bq, LANES), lambda b, h, i: (b, i, h)),
        out_shape=jax.ShapeDtypeStruct((B, S, H * LANES), BF16),
        scratch_shapes=[
            pltpu.VMEM((2 * LANES, 2 * bq), BF16),
            pltpu.VMEM((1, 2 * bq), F32),
            pltpu.VMEM((LANES + ONES_ROWS, 2 * bq), F32),
            pltpu.VMEM((bk, 8 * LANES), F32),
            pltpu.VMEM((2, bk, LANES), BF16),
        ],
        compiler_params=_params(("parallel", "parallel", "arbitrary")),
        name="diff_attn",
    )(slopes, lam_qk, g_diff, nrm, qt, k, vt)


def _dil_bias(dil, slope_a, slope_b):
    QB, R = LANES, DIL_RADIUS
    KW = QB + 2 * R
    r_i = lax.broadcasted_iota(jnp.int32, (QB, KW), 0)
    c_i = lax.broadcasted_iota(jnp.int32, (QB, KW), 1)
    rel = c_i - r_i - R
    band = (rel >= -R) & (rel <= R)
    dist = (jnp.abs(rel) * dil).astype(F32)
    return jnp.concatenate([jnp.where(band, -(slope_a * LOG2E) * dist, NEG),
                            jnp.where(band, -(slope_b * LOG2E) * dist, NEG)], axis=0)


def _dil_segment(q_at, k_at, v_at, nres, L, dil, tok0, bias2, first, last, online,
                 sa_ref, sb_ref, m_ref, o_ref, kpad, vpad_a, vpad_b):
    QB, R = LANES, DIL_RADIUS
    KW = QB + 2 * R
    pitch = L + 2 * R
    per_class = L // QB
    lane_q = lax.broadcasted_iota(jnp.int32, (QB, LANES), 1)
    head_a = lane_q < HEAD_DIM
    c2 = lax.broadcasted_iota(jnp.int32, (2 * QB, KW), 1)

    zpad = jnp.zeros((R, LANES), BF16)
    own_a = lax.broadcasted_iota(jnp.int32, (L, LANES), 1) < HEAD_DIM
    one = jnp.ones((L, LANES), BF16)
    for c in range(nres):
        base = c * pitch
        for pad in (kpad, vpad_a, vpad_b):
            pad[base:base + R, :] = zpad
            pad[base + R + L:base + pitch, :] = zpad
        kpad[base + R:base + R + L, :] = k_at(c)
        v_src = v_at(c)
        vpad_a[base + R:base + R + L, :] = jnp.where(own_a, v_src, one)
        vpad_b[base + R:base + R + L, :] = jnp.where(own_a, one, v_src)

    def rows(c, i0):
        if dil == 1:
            return pl.ds(i0, QB)
        return pl.ds(tok0 + c + dil * i0, QB, stride=dil)

    def scores(c, i0):
        q = q_at(c, i0)
        zero = jnp.zeros_like(q)
        q2 = jnp.concatenate([jnp.where(head_a, q, zero), jnp.where(head_a, zero, q)], axis=0)
        s = lax.dot_general(q2, kpad[pl.ds(c * pitch + i0, KW), :], (((1,), (1,)), ((), ())),
                            preferred_element_type=F32)
        return s + bias2

    def finish(c, i0, s):
        if online:
            key = i0 - R + c2
            s = jnp.where((key >= 0) & (key < L), s, NEG)
            m_c = jnp.max(s, axis=1, keepdims=True)
            s = s - m_c
        p = jnp.exp2(s).astype(BF16)
        win = pl.ds(c * pitch + i0, KW)
        sa = jnp.dot(p[0:QB], vpad_a[win, :], preferred_element_type=F32)
        sb = jnp.dot(p[QB:2 * QB], vpad_b[win, :], preferred_element_type=F32)
        r = rows(c, i0)
        if online:
            ma = jnp.broadcast_to(m_c[0:QB], (QB, LANES))
            mb = jnp.broadcast_to(m_c[QB:2 * QB], (QB, LANES))
            if not first:
                m_p = m_ref[r, :]
                mpa = jnp.max(jnp.where(head_a, m_p, -jnp.inf), axis=1, keepdims=True)
                mpb = jnp.max(jnp.where(head_a, -jnp.inf, m_p), axis=1, keepdims=True)
                mna, mnb = jnp.maximum(mpa, ma), jnp.maximum(mpb, mb)
                sa = sa_ref[r, :] * jnp.exp2(mpa - mna) + sa * jnp.exp2(ma - mna)
                sb = sb_ref[r, :] * jnp.exp2(mpb - mnb) + sb * jnp.exp2(mb - mnb)
                ma, mb = mna, mnb
            if not last:
                m_ref[r, :] = jnp.where(head_a, ma, mb)
        elif not first:
            sa = sa_ref[r, :] + sa
            sb = sb_ref[r, :] + sb
        if last:
            half = LANES // 2
            o_ref[0, 0, r, :] = jnp.where(head_a, sa / pltpu.roll(sa, half, 1),
                                          sb / pltpu.roll(sb, half, 1))
        else:
            sa_ref[r, :] = sa
            sb_ref[r, :] = sb

    def blocks(todo):
        ss = [scores(c, i0) for c, i0 in todo]
        for (c, i0), s in zip(todo, ss):
            finish(c, i0, s)

    if online:
        for c in range(nres):
            def body(n, carry, c=c):
                blocks([(c, pl.multiple_of(n * QB, QB))])
                return carry
            lax.fori_loop(0, per_class, body, 0)
    elif nres * per_class <= DIL_MAX_BLOCKS:
        blocks([(c, u * QB) for c in range(nres) for u in range(per_class)])
    else:
        assert nres == 1 and per_class % DIL_MAX_BLOCKS == 0

        def body(n, carry):
            blocks([(0, pl.multiple_of((n * DIL_MAX_BLOCKS + u) * QB, QB))
                    for u in range(DIL_MAX_BLOCKS)])
            return carry

        lax.fori_loop(0, per_class // DIL_MAX_BLOCKS, body, 0)


def _dil_kernel(slopes_ref, nrm_ref, pb1_ref, pb4_ref, pb16_ref, o_ref,
                sa_ref, sb_ref, m_ref, kpad, vpad_a, vpad_b, *, group16):
    hp = pl.program_id(1)
    t = pl.program_id(2)
    S = pb1_ref.shape[2]
    n16 = 16 // group16
    slope_a = slopes_ref[2 * hp]
    slope_b = slopes_ref[2 * hp + 1]

    big = jnp.max(nrm_ref[0, :, 0, :], axis=0, keepdims=True)
    lane = lax.broadcasted_iota(jnp.int32, big.shape, 1)
    def pick(idx):
        return jnp.max(jnp.where(lane == idx, big, 0.0))

    direct = ((pick(4 * hp) * pick(4 * hp + 1) <= DIRECT_EXP_MAX_LOGIT ** 2)
              & (pick(4 * hp + 2) * pick(4 * hp + 3) <= DIRECT_EXP_MAX_LOGIT ** 2))

    def pattern(online):
        args = (sa_ref, sb_ref, m_ref, o_ref, kpad, vpad_a, vpad_b)

        @pl.when(t < n16)
        def _():
            _dil_segment(lambda c, i0: pb16_ref[0, 0, c, pl.ds(i0, LANES), 0:LANES],
                         lambda c: pb16_ref[0, 0, c, :, LANES:2 * LANES],
                         lambda c: pb16_ref[0, 0, c, :, 2 * LANES:3 * LANES],
                         group16, S // 16, 16, t * group16, _dil_bias(16, slope_a, slope_b),
                         True, False, online, *args)

        @pl.when((t >= n16) & (t < n16 + 4))
        def _():
            _dil_segment(lambda c, i0: pb4_ref[0, 0, 0, pl.ds(i0, LANES), 0:LANES],
                         lambda c: pb4_ref[0, 0, 0, :, LANES:2 * LANES],
                         lambda c: pb4_ref[0, 0, 0, :, 2 * LANES:3 * LANES],
                         1, S // 4, 4, t - n16, _dil_bias(4, slope_a, slope_b),
                         False, False, online, *args)

        @pl.when(t == n16 + 4)
        def _():
            _dil_segment(lambda c, i0: pb1_ref[0, 0, pl.ds(i0, LANES), 0:LANES],
                         lambda c: pb1_ref[0, 0, :, LANES:2 * LANES],
                         lambda c: pb1_ref[0, 0, :, 2 * LANES:3 * LANES],
                         1, S, 1, 0, _dil_bias(1, slope_a, slope_b),
                         False, True, online, *args)

    @pl.when(direct)
    def _():
        pattern(False)

    @pl.when(jnp.logical_not(direct))
    def _():
        pattern(True)


def _dil_attn(pb1, pb4, pb16, nrm, slopes, group16=4):
    B, P, S, _ = pb1.shape
    n16 = 16 // group16
    nt = nrm.shape[1]
    return pl.pallas_call(
        functools.partial(_dil_kernel, group16=group16),
        grid=(B, P, n16 + 4 + 1),
        in_specs=[
            pl.BlockSpec(memory_space=pltpu.SMEM),
            pl.BlockSpec((1, nt, 1, LANES), lambda b, p, t: (b, 0, 0, 0)),
            pl.BlockSpec((1, 1, S, PAIR_W), lambda b, p, t: (b, p, 0, 0)),
            pl.BlockSpec((1, 1, 1, S // 4, PAIR_W),
                         lambda b, p, t: (b, p, jnp.clip(t - n16, 0, 3), 0, 0)),
            pl.BlockSpec((1, 1, group16, S // 16, PAIR_W),
                         lambda b, p, t: (b, p, jnp.minimum(t, n16 - 1), 0, 0)),
        ],
        out_specs=pl.BlockSpec((1, 1, S, LANES), lambda b, p, t: (b, p, 0, 0)),
        out_shape=jax.ShapeDtypeStruct((B, P, S, LANES), F32),
        scratch_shapes=[
            pltpu.VMEM((S, LANES), F32),
            pltpu.VMEM((S, LANES), F32),
            pltpu.VMEM((S, LANES), F32),
            pltpu.VMEM((S + 2 * DIL_RADIUS, LANES), BF16),
            pltpu.VMEM((S + 2 * DIL_RADIUS, LANES), BF16),
            pltpu.VMEM((S + 2 * DIL_RADIUS, LANES), BF16),
        ],
        compiler_params=_params(("parallel", "parallel", "arbitrary")),
        name="dil_attn",
    )(slopes, nrm, pb1, pb4, pb16)


def _out_mlp_kernel(x_ref, oa_ref, ob_ref, gdil_ref, wout_ref, gmlp_ref, wup_ref,
                    wdown_ref, gfin_ref, y_ref, *, ff_chunk):
    ob = jnp.concatenate([ob_ref[0, p] for p in range(N_DIL_PAIRS)], axis=-1)
    ob = _rms(ob, gdil_ref[...])
    mix = jnp.concatenate([oa_ref[0], ob.astype(BF16)], axis=-1)
    x1 = x_ref[0] + jnp.dot(mix, wout_ref[...], preferred_element_type=F32)
    h = _rms(x1, gmlp_ref[...]).astype(BF16)
    y = x1
    for c in range(D_FF // ff_chunk):
        u = jnp.dot(h, wup_ref[:, c * ff_chunk:(c + 1) * ff_chunk], preferred_element_type=F32)
        u = jnp.square(jnp.maximum(u, 0.0)).astype(BF16)
        y = y + jnp.dot(u, wdown_ref[c * ff_chunk:(c + 1) * ff_chunk, :],
                        preferred_element_type=F32)
    y_ref[0] = _rms(y, gfin_ref[...])


def _out_mlp(x, oa, ob, g_dil, w_out, g_mlp, w_up, w_down, g_final, tm, ff_chunk):
    B, S, D = x.shape
    const = lambda shape: pl.BlockSpec(shape, lambda b, i: (0,) * len(shape),
                                       pipeline_mode=pl.Buffered(1))
    return pl.pallas_call(
        functools.partial(_out_mlp_kernel, ff_chunk=ff_chunk),
        grid=(B, S // tm),
        in_specs=[
            pl.BlockSpec((1, tm, D), lambda b, i: (b, i, 0)),
            pl.BlockSpec((1, tm, N_DIFF_HEADS * LANES), lambda b, i: (b, i, 0)),
            pl.BlockSpec((1, N_DIL_PAIRS, tm, LANES), lambda b, i: (b, 0, i, 0)),
            const((1, DIL_W)),
            const((2 * DIL_W, D)),
            const((1, D)),
            const((D, D_FF)),
            const((D_FF, D)),
            const((1, D)),
        ],
        out_specs=pl.BlockSpec((1, tm, D), lambda b, i: (b, i, 0)),
        out_shape=jax.ShapeDtypeStruct((B, S, D), F32),
        compiler_params=_params(("parallel", "parallel")),
        name="out_mlp",
    )(x, oa, ob, g_dil, w_out, g_mlp, w_up, w_down, g_final)


def _alibi_slopes(n):
    return 2.0 ** (-8.0 * jnp.arange(1, n + 1, dtype=F32) / n)


def _layer(x, g_mix, w_in, wq_t, wv_t, lam_qk, g_diff, g_dil, w_out, g_mlp, w_up, w_down,
           g_final, lam_init):
    qt, k, vt, pb1, pb4, pb16, nrm = _in_proj(x, g_mix, w_in, wq_t, wv_t, tm=512)
    oa = _diff_attn(qt, k, vt, nrm, _alibi_slopes(N_DIFF_HEADS), lam_qk, g_diff, lam_init,
                    bq=1024)
    assert DIL_PATTERNS == ((2 * DIL_RADIUS, 1), (8 * DIL_RADIUS, 4), (32 * DIL_RADIUS, 16))
    ob = _dil_attn(pb1, pb4, pb16, nrm, _alibi_slopes(N_DIL_HEADS))
    return _out_mlp(x, oa, ob, g_dil, w_out, g_mlp, w_up, w_down, g_final,
                    tm=512, ff_chunk=1024)


def kernel(x_prompt, x_sample, g_mix, w_in, lam_qk, g_diff, g_dil, w_out, g_mlp, w_up,
           w_down, g_final):
    assert g_mix.shape[0] == 1
    lam_init = 0.8 - 0.6 * math.exp(-0.3 * 0)
    w_in_b = w_in[0].astype(BF16)
    weights = (
        g_mix[0][None, :], w_in_b, w_in_b[:, 0:DIFF_QW].T, w_in_b[:, 2 * DIFF_QW:DIFF_W].T,
        lam_qk[0],
        g_diff[0], g_dil[0][None, :], w_out[0].astype(BF16), g_mlp[0][None, :],
        w_up[0].astype(BF16), w_down[0].astype(BF16), g_final[None, :],
    )
    return tuple(_layer(x, *weights, lam_init) for x in (x_prompt, x_sample))
```

```python
import functools
import math

import jax
import jax.numpy as jnp
from jax import lax
from jax.experimental import pallas as pl
from jax.experimental.pallas import tpu as pltpu

D_MODEL = 1024
HEAD_DIM = 64
N_DIFF_HEADS = 4
N_DIL_HEADS = 8
N_DIL_PAIRS = N_DIL_HEADS // 2
DIL_PATTERNS = ((128, 1), (512, 4), (2048, 16))
DIL_RADIUS = 64
D_FF = 4 * D_MODEL
EPS = 1e-5
DIFF_QW = N_DIFF_HEADS * 2 * HEAD_DIM
DIFF_W = 3 * DIFF_QW
DIL_W = N_DIL_HEADS * HEAD_DIM
IN_W = DIFF_W + 3 * DIL_W
LANES = 128
PAIR_W = 3 * LANES
NEG = -1e30
LOG2E = math.log2(math.e)
ONES_ROWS = 16
DIRECT_EXP_MAX_LOGIT = 60.0
BIAS_ROWS = 16
DIFF_MAX_TILES = 4
DIL_MAX_BLOCKS = 16
NRM_DIFF_K = 4 * N_DIL_PAIRS
NRM_DIFF_Q = NRM_DIFF_K + N_DIFF_HEADS
VMEM_LIMIT = 56 * 1024 * 1024

BF16 = jnp.bfloat16
F32 = jnp.float32


def _rms(x, g):
    return x * lax.rsqrt(jnp.mean(x * x, axis=-1, keepdims=True) + EPS) * g


def _params(sem):
    return pltpu.CompilerParams(dimension_semantics=sem, vmem_limit_bytes=VMEM_LIMIT)


def _in_proj_kernel(x_ref, g_ref, w_ref, wqt_ref, wvt_ref, qt_ref, k_ref, vt_ref,
                    pb1_ref, pb4_ref, pb16_ref, nrm_ref, slab_ref, slab4_ref):
    scale = HEAD_DIM ** -0.5
    nt = (((1,), (1,)), ((), ()))
    tm = x_ref.shape[1]
    lane = lax.broadcasted_iota(jnp.int32, (1, LANES), 1)
    nrm = jnp.zeros((1, LANES), F32)
    h = _rms(x_ref[0], g_ref[...]).astype(BF16)
    for c in range(3):
        p = jnp.dot(h, w_ref[:, DIFF_W + c * DIL_W:DIFF_W + (c + 1) * DIL_W],
                    preferred_element_type=F32)
        if c == 0:
            p = p * (scale * LOG2E)
        cols = slice(c * LANES, (c + 1) * LANES)
        for hp in range(N_DIL_PAIRS):
            ph = p[:, hp * LANES:(hp + 1) * LANES]
            if c < 2:
                sq = ph * ph
                for e in range(2):
                    mine = (lane >= e * HEAD_DIM) & (lane < (e + 1) * HEAD_DIM)
                    big = jnp.max(jnp.sum(jnp.where(mine, sq, 0.0), axis=1, keepdims=True),
                                  axis=0, keepdims=True)
                    nrm = jnp.where(lane == 4 * hp + 2 * e + c, big, nrm)
            pb1_ref[0, hp, :, cols] = ph.astype(BF16)
            s = c * N_DIL_PAIRS + hp
            slab_ref[s] = ph
            for g in range(4):
                v4 = slab_ref[s, pl.ds(g, tm // 4, stride=4), :]
                pb4_ref[0, hp, g, :, cols] = v4.astype(BF16)
                slab4_ref[s, g] = v4
            for g in range(4):
                for c2 in range(4):
                    v16 = slab4_ref[s, g, pl.ds(c2, tm // 16, stride=4), :]
                    pb16_ref[0, hp, 4 * c2 + g, :, cols] = v16.astype(BF16)
    kd = jnp.dot(h, w_ref[:, DIFF_QW:2 * DIFF_QW], preferred_element_type=F32)
    k_ref[0] = kd.astype(BF16)
    for hh in range(N_DIFF_HEADS):
        kh = kd[:, hh * LANES:(hh + 1) * LANES]
        big = jnp.max(jnp.sum(kh * kh, axis=1, keepdims=True), axis=0, keepdims=True)
        nrm = jnp.where(lane == NRM_DIFF_K + hh, big, nrm)
    qt = lax.dot_general(wqt_ref[...], h, nt, preferred_element_type=F32) * (scale * LOG2E)
    qt_ref[0] = qt.astype(BF16)
    for hm in range(2 * N_DIFF_HEADS):
        qm = qt[hm * HEAD_DIM:(hm + 1) * HEAD_DIM, :]
        big = jnp.max(jnp.sum(qm * qm, axis=0, keepdims=True), axis=1, keepdims=True)
        nrm = jnp.where(lane == NRM_DIFF_Q + hm, big, nrm)
    nrm_ref[0, 0] = nrm
    vt = lax.dot_general(wvt_ref[...], h, nt, preferred_element_type=F32).astype(BF16)
    for hh in range(N_DIFF_HEADS):
        vt_ref[0, hh, 0] = vt[hh * LANES:(hh + 1) * LANES, :]


def _in_proj(x, g_mix, w_in, wq_t, wv_t, tm):
    B, S, D = x.shape
    H = N_DIFF_HEADS
    P = N_DIL_PAIRS
    return pl.pallas_call(
        _in_proj_kernel,
        grid=(B, S // tm),
        in_specs=[
            pl.BlockSpec((1, tm, D), lambda b, i: (b, i, 0)),
            pl.BlockSpec((1, D), lambda b, i: (0, 0)),
            pl.BlockSpec((D, IN_W), lambda b, i: (0, 0)),
            pl.BlockSpec((DIFF_QW, D), lambda b, i: (0, 0)),
            pl.BlockSpec((DIFF_QW, D), lambda b, i: (0, 0)),
        ],
        out_specs=[
            pl.BlockSpec((1, DIFF_QW, tm), lambda b, i: (b, 0, i)),
            pl.BlockSpec((1, tm, DIFF_QW), lambda b, i: (b, i, 0)),
            pl.BlockSpec((1, H, 1, LANES, tm), lambda b, i: (b, 0, i, 0, 0)),
            pl.BlockSpec((1, P, tm, PAIR_W), lambda b, i: (b, 0, i, 0)),
            pl.BlockSpec((1, P, 4, tm // 4, PAIR_W), lambda b, i: (b, 0, 0, i, 0)),
            pl.BlockSpec((1, P, 16, tm // 16, PAIR_W), lambda b, i: (b, 0, 0, i, 0)),
            pl.BlockSpec((1, 1, 1, LANES), lambda b, i: (b, i, 0, 0)),
        ],
        out_shape=[
            jax.ShapeDtypeStruct((B, DIFF_QW, S), BF16),
            jax.ShapeDtypeStruct((B, S, DIFF_QW), BF16),
            jax.ShapeDtypeStruct((B, H, S // tm, LANES, tm), BF16),
            jax.ShapeDtypeStruct((B, P, S, PAIR_W), BF16),
            jax.ShapeDtypeStruct((B, P, 4, S // 4, PAIR_W), BF16),
            jax.ShapeDtypeStruct((B, P, 16, S // 16, PAIR_W), BF16),
            jax.ShapeDtypeStruct((B, S // tm, 1, LANES), F32),
        ],
        scratch_shapes=[
            pltpu.VMEM((3 * P, tm, LANES), F32),
            pltpu.VMEM((3 * P, 4, tm // 4, LANES), F32),
        ],
        compiler_params=_params(("parallel", "parallel")),
        name="in_proj",
    )(x, g_mix, w_in, wq_t, wv_t)


def _diff_attn_kernel(slopes_ref, lam_ref, g_ref, nrm_ref, qt_ref, k_ref, vt_ref, o_ref,
                      q2t_ref, m_ref, acc_ref, cb_ref, feat_ref, *, bq, bk, lam_init):
    G = cb_ref.shape[1]
    h = pl.program_id(1)
    qi = pl.program_id(2)
    nkv = k_ref.shape[1] // bk
    slope2 = slopes_ref[h] * LOG2E

    qt = qt_ref[0]
    sub = lax.broadcasted_iota(jnp.int32, qt.shape, 0)
    zero = jnp.zeros_like(qt)
    q2t_ref[0:LANES, 0:bq] = jnp.where(sub < HEAD_DIM, qt, zero)
    q2t_ref[0:LANES, bq:2 * bq] = jnp.where(sub >= HEAD_DIM, qt, zero)
    max_tiles = feat_ref.shape[0]
    assert max_tiles * BIAS_ROWS <= LANES
    q2t_ref[LANES:2 * LANES, :] = jnp.zeros((LANES, 2 * bq), BF16)
    f_row = lax.broadcasted_iota(jnp.int32, (bk, LANES), 0)
    f_lane = lax.broadcasted_iota(jnp.int32, (bk, LANES), 1)
    f_slot = f_lane % BIAS_ROWS
    feat = jnp.where(f_slot < 2, f_row % 256,
                     jnp.where(f_slot < 4, f_row // 256,
                               jnp.where(f_slot < 7, 1, 0)))
    for tt in range(max_tiles):
        mine = f_lane // BIAS_ROWS == tt
        feat_ref[tt] = jnp.where(mine, feat, 0).astype(F32).astype(BF16)
    m_ref[...] = jnp.full(m_ref.shape, -jnp.inf, F32)
    acc_ref[...] = jnp.zeros(acc_ref.shape, F32)
    key_off = lax.broadcasted_iota(jnp.int32, (bk, G), 0).astype(F32)
    cb_ref[...] = slope2 * key_off
    lane = lax.broadcasted_iota(jnp.int32, (1, 2 * bq), 1)
    pos_q = (qi * bq + jnp.where(lane >= bq, lane - bq, lane)).astype(F32)
    ones = jnp.ones((ONES_ROWS, bk), BF16)

    q_rows = nrm_ref.shape[1] * bq // k_ref.shape[1]
    k_big = jnp.max(nrm_ref[0, :, 0, :], axis=0, keepdims=True)
    q_big = jnp.max(nrm_ref[0, pl.ds(qi * q_rows, q_rows), 0, :], axis=0, keepdims=True)
    nlane = lax.broadcasted_iota(jnp.int32, k_big.shape, 1)
    k_max = jnp.max(jnp.where(nlane == NRM_DIFF_K + h, k_big, 0.0))
    q_mine = (nlane == NRM_DIFF_Q + 2 * h) | (nlane == NRM_DIFF_Q + 2 * h + 1)
    q_max = jnp.max(jnp.where(q_mine, q_big, 0.0))
    direct = q_max * k_max <= DIRECT_EXP_MAX_LOGIT ** 2

    def online_tile(j, side):
        start = pl.multiple_of(j * bk, bk)
        k_t = k_ref[0, pl.ds(start, bk), :]
        vt1 = jnp.concatenate([vt_ref[0, 0, j], ones], axis=0)
        j0 = (j * bk).astype(F32)
        groups = [slice(gi * G, (gi + 1) * G) for gi in range(2 * bq // G)]
        scores = [jnp.dot(k_t, q2t_ref[0:LANES, sl], preferred_element_type=F32)
                  for sl in groups]
        for sl, s in zip(groups, scores):
            pq = pos_q[:, sl]
            if side == 0:
                u = s - slope2 * jnp.abs(pq - (j0 + key_off))
                a = jnp.zeros_like(pq)
            elif side < 0:
                u = s + cb_ref[...]
                a = slope2 * (j0 - pq)
            else:
                u = s - cb_ref[...]
                a = slope2 * (pq - j0)
            m_old = m_ref[:, sl]
            m_new = jnp.maximum(m_old, jnp.max(u, axis=0, keepdims=True) + a)
            alpha = jnp.exp2(m_old - m_new)
            p = jnp.exp2(u - (m_new - a)).astype(BF16)
            acc_ref[:, sl] = alpha * acc_ref[:, sl] + jnp.dot(
                vt1, p, preferred_element_type=F32)
            m_ref[:, sl] = m_new

    def direct_tiles(j, n, side):
        groups = [slice(gi * G, (gi + 1) * G) for gi in range(2 * bq // G)]

        def bias_rows(j0):
            def bf16_part(x):
                return x.astype(BF16).astype(F32)

            sgn = -float(side)
            sv = jnp.zeros_like(pos_q) + sgn * slope2
            s_hi = bf16_part(sv)
            s_lo = bf16_part(sv - s_hi)
            a = sv * (j0 - pos_q)
            a1 = bf16_part(a)
            a2 = bf16_part(a - a1)
            a3 = bf16_part(a - a1 - a2)
            pieces = [s_hi, s_lo, s_hi * 256.0, s_lo * 256.0, a1, a2, a3]
            row = lax.broadcasted_iota(jnp.int32, (BIAS_ROWS, 2 * bq), 0)
            w = jnp.zeros((BIAS_ROWS, 2 * bq), F32)
            for i, piece in enumerate(pieces):
                w = jnp.where(row == i, piece, w)
            return w.astype(BF16)

        def operands(tt):
            jj = j + tt
            k_t = k_ref[0, pl.ds(pl.multiple_of(jj * bk, bk), bk), :]
            j0 = (jj * bk).astype(F32)
            if side != 0:
                q2t_ref[LANES + tt * BIAS_ROWS:LANES + (tt + 1) * BIAS_ROWS, :] = bias_rows(j0)
                k_t = jnp.concatenate([k_t, feat_ref[tt]], axis=1)
            return k_t, vt_ref[0, 0, jj], j0

        def score(k_t, sl):
            rows = slice(0, k_t.shape[1])
            return jnp.dot(k_t, q2t_ref[rows, sl], preferred_element_type=F32)

        def pv(vt, j0, sl, s):
            if side == 0:
                s = s - slope2 * jnp.abs(pos_q[:, sl] - (j0 + key_off))
            p = jnp.exp2(s)
            return (jnp.dot(vt, p.astype(BF16), preferred_element_type=F32),
                    jnp.sum(p, axis=0, keepdims=True))

        tiles = [operands(tt) for tt in range(n)]
        scores = [score(tiles[0][0], sl) for sl in groups]
        sums = [None] * len(groups)
        for tt in range(n):
            _, vt, j0 = tiles[tt]
            nxt = []
            for gi, sl in enumerate(groups):
                x, lx = pv(vt, j0, sl, scores[gi])
                sums[gi] = (x, lx) if tt == 0 else (sums[gi][0] + x, sums[gi][1] + lx)
                if tt + 1 < n:
                    nxt.append(score(tiles[tt + 1][0], sl))
            scores = nxt
        for sl, (x, lx) in zip(groups, sums):
            acc_ref[0:LANES, sl] += x
            acc_ref[LANES:LANES + 1, sl] += lx

    def all_tiles(online):
        ndiag = bq // bk
        if online:
            def before(j, carry):
                online_tile(j, -1)
                return carry

            def after(j, carry):
                online_tile(j, 1)
                return carry

            lax.fori_loop(0, qi * ndiag, before, 0)
            for d in range(ndiag):
                online_tile(qi * ndiag + d, 0)
            lax.fori_loop((qi + 1) * ndiag, nkv, after, 0)
            return

        assert max_tiles % 2 == 0 and ndiag % 2 == 0 and nkv % 2 == 0

        def run(first, count, side):
            def body(t, carry):
                direct_tiles(first + t * max_tiles, max_tiles, side)
                return carry

            full = count // max_tiles
            lax.fori_loop(0, full, body, 0)
            for extra in range(2, max_tiles, 2):
                @pl.when(count - full * max_tiles == extra)
                def _():
                    direct_tiles(first + full * max_tiles, extra, side)

        run(0, qi * ndiag, -1)
        for d in range(0, ndiag, 2):
            direct_tiles(qi * ndiag + d, 2, 0)
        run((qi + 1) * ndiag, nkv - (qi + 1) * ndiag, 1)

    @pl.when(direct)
    def _():
        all_tiles(False)

    @pl.when(jnp.logical_not(direct))
    def _():
        all_tiles(True)

    lq = lam_ref[...]
    lam = (jnp.exp(jnp.sum(lq[0:1] * lq[1:2], axis=1, keepdims=True))
           - jnp.exp(jnp.sum(lq[2:3] * lq[3:4], axis=1, keepdims=True)) + lam_init)
    acc = acc_ref[...]
    ot = acc[0:LANES] * (1.0 / acc[LANES:LANES + 1])
    od = (ot[:, 0:bq] - lam * ot[:, bq:2 * bq]).T
    od = od * lax.rsqrt(jnp.mean(od * od, axis=-1, keepdims=True) + EPS)
    od = od * g_ref[pl.ds(h, 1), :] * (1.0 - lam_init)
    o_ref[0] = od.astype(BF16)


def _diff_attn(qt, k, vt, nrm, slopes, lam_qk, g_diff, lam_init, bq):
    B, S, _ = k.shape
    H = N_DIFF_HEADS
    nkv, bk = vt.shape[2], vt.shape[4]
    assert bq % bk == 0
    kern = functools.partial(_diff_attn_kernel, bq=bq, bk=bk, lam_init=lam_init)
    return pl.pallas_call(
        kern,
        grid=(B, H, S // bq),
        in_specs=[
            pl.BlockSpec(memory_space=pltpu.SMEM),
            pl.BlockSpec((4, HEAD_DIM), lambda b, h, i: (0, 0)),
            pl.BlockSpec((H, LANES), lambda b, h, i: (0, 0)),
            pl.BlockSpec((1, nrm.shape[1], 1, LANES), lambda b, h, i: (b, 0, 0, 0)),
            pl.BlockSpec((1, LANES, bq), lambda b, h, i: (b, h, i)),
            pl.BlockSpec((1, S, LANES), lambda b, h, i: (b, 0, h)),
            pl.BlockSpec((1, 1, nkv, LANES, bk), lambda b, h, i: (b, h, 0, 0, 0)),
        ],
        out_specs=pl.BlockSpec((1, bq, LANES), lambda b, h, i: (b, i, h)),
        out_shape=jax.ShapeDtypeStruct((B, S, H * LANES), BF16),
        scratch_shapes=[
            pltpu.VMEM((2 * LANES, 2 * bq), BF16),
            pltpu.VMEM((1, 2 * bq), F32),
            pltpu.VMEM((LANES + ONES_ROWS, 2 * bq), F32),
            pltpu.VMEM((bk, 8 * LANES), F32),
            pltpu.VMEM((DIFF_MAX_TILES, bk, LANES), BF16),
        ],
        compiler_params=_params(("parallel", "parallel", "arbitrary")),
        name="diff_attn",
    )(slopes, lam_qk, g_diff, nrm, qt, k, vt)


def _dil_bias(dil, slope_a, slope_b):
    QB, R = LANES, DIL_RADIUS
    KW = QB + 2 * R
    r_i = lax.broadcasted_iota(jnp.int32, (QB, KW), 0)
    c_i = lax.broadcasted_iota(jnp.int32, (QB, KW), 1)
    rel = c_i - r_i - R
    band = (rel >= -R) & (rel <= R)
    dist = (jnp.abs(rel) * dil).astype(F32)
    return jnp.concatenate([jnp.where(band, -(slope_a * LOG2E) * dist, NEG),
                            jnp.where(band, -(slope_b * LOG2E) * dist, NEG)], axis=0)


def _dil_segment(q_at, k_at, v_at, nres, L, dil, tok0, bias2, first, last, online,
                 sa_ref, sb_ref, m_ref, o_ref, kpad, vpad_a, vpad_b):
    QB, R = LANES, DIL_RADIUS
    KW = QB + 2 * R
    pitch = L + 2 * R
    per_class = L // QB
    lane_q = lax.broadcasted_iota(jnp.int32, (QB, LANES), 1)
    head_a = lane_q < HEAD_DIM
    c2 = lax.broadcasted_iota(jnp.int32, (2 * QB, KW), 1)

    zpad = jnp.zeros((R, LANES), BF16)
    own_a = lax.broadcasted_iota(jnp.int32, (L, LANES), 1) < HEAD_DIM
    one = jnp.ones((L, LANES), BF16)
    for c in range(nres):
        base = c * pitch
        for pad in (kpad, vpad_a, vpad_b):
            pad[base:base + R, :] = zpad
            pad[base + R + L:base + pitch, :] = zpad
        kpad[base + R:base + R + L, :] = k_at(c)
        v_src = v_at(c)
        vpad_a[base + R:base + R + L, :] = jnp.where(own_a, v_src, one)
        vpad_b[base + R:base + R + L, :] = jnp.where(own_a, one, v_src)

    def rows(c, i0):
        if dil == 1:
            return pl.ds(i0, QB)
        return pl.ds(tok0 + c + dil * i0, QB, stride=dil)

    def scores(c, i0):
        q = q_at(c, i0)
        zero = jnp.zeros_like(q)
        q2 = jnp.concatenate([jnp.where(head_a, q, zero), jnp.where(head_a, zero, q)], axis=0)
        s = lax.dot_general(q2, kpad[pl.ds(c * pitch + i0, KW), :], (((1,), (1,)), ((), ())),
                            preferred_element_type=F32)
        return s + bias2

    def finish(c, i0, s):
        if online:
            key = i0 - R + c2
            s = jnp.where((key >= 0) & (key < L), s, NEG)
            m_c = jnp.max(s, axis=1, keepdims=True)
            s = s - m_c
        p = jnp.exp2(s).astype(BF16)
        win = pl.ds(c * pitch + i0, KW)
        sa = jnp.dot(p[0:QB], vpad_a[win, :], preferred_element_type=F32)
        sb = jnp.dot(p[QB:2 * QB], vpad_b[win, :], preferred_element_type=F32)
        r = rows(c, i0)
        if online:
            ma = jnp.broadcast_to(m_c[0:QB], (QB, LANES))
            mb = jnp.broadcast_to(m_c[QB:2 * QB], (QB, LANES))
            if not first:
                m_p = m_ref[r, :]
                mpa = jnp.max(jnp.where(head_a, m_p, -jnp.inf), axis=1, keepdims=True)
                mpb = jnp.max(jnp.where(head_a, -jnp.inf, m_p), axis=1, keepdims=True)
                mna, mnb = jnp.maximum(mpa, ma), jnp.maximum(mpb, mb)
                sa = sa_ref[r, :] * jnp.exp2(mpa - mna) + sa * jnp.exp2(ma - mna)
                sb = sb_ref[r, :] * jnp.exp2(mpb - mnb) + sb * jnp.exp2(mb - mnb)
                ma, mb = mna, mnb
            if not last:
                m_ref[r, :] = jnp.where(head_a, ma, mb)
        elif not first:
            sa = sa_ref[r, :] + sa
            sb = sb_ref[r, :] + sb
        if last:
            half = LANES // 2
            o_ref[0, 0, r, :] = jnp.where(head_a, sa / pltpu.roll(sa, half, 1),
                                          sb / pltpu.roll(sb, half, 1))
        else:
            sa_ref[r, :] = sa
            sb_ref[r, :] = sb

    def blocks(todo):
        ss = [scores(c, i0) for c, i0 in todo]
        for (c, i0), s in zip(todo, ss):
            finish(c, i0, s)

    if online:
        for c in range(nres):
            def body(n, carry, c=c):
                blocks([(c, pl.multiple_of(n * QB, QB))])
                return carry
            lax.fori_loop(0, per_class, body, 0)
    elif nres * per_class <= DIL_MAX_BLOCKS:
        blocks([(c, u * QB) for c in range(nres) for u in range(per_class)])
    else:
        assert nres == 1 and per_class % DIL_MAX_BLOCKS == 0

        def body(n, carry):
            blocks([(0, pl.multiple_of((n * DIL_MAX_BLOCKS + u) * QB, QB))
                    for u in range(DIL_MAX_BLOCKS)])
            return carry

        lax.fori_loop(0, per_class // DIL_MAX_BLOCKS, body, 0)


def _dil_kernel(slopes_ref, nrm_ref, pb1_ref, pb4_ref, pb16_ref, o_ref,
                sa_ref, sb_ref, m_ref, kpad, vpad_a, vpad_b, *, group16):
    hp = pl.program_id(1)
    t = pl.program_id(2)
    S = pb1_ref.shape[2]
    n16 = 16 // group16
    slope_a = slopes_ref[2 * hp]
    slope_b = slopes_ref[2 * hp + 1]

    big = jnp.max(nrm_ref[0, :, 0, :], axis=0, keepdims=True)
    lane = lax.broadcasted_iota(jnp.int32, big.shape, 1)
    def pick(idx):
        return jnp.max(jnp.where(lane == idx, big, 0.0))

    direct = ((pick(4 * hp) * pick(4 * hp + 1) <= DIRECT_EXP_MAX_LOGIT ** 2)
              & (pick(4 * hp + 2) * pick(4 * hp + 3) <= DIRECT_EXP_MAX_LOGIT ** 2))

    def pattern(online):
        args = (sa_ref, sb_ref, m_ref, o_ref, kpad, vpad_a, vpad_b)

        @pl.when(t < n16)
        def _():
            _dil_segment(lambda c, i0: pb16_ref[0, 0, c, pl.ds(i0, LANES), 0:LANES],
                         lambda c: pb16_ref[0, 0, c, :, LANES:2 * LANES],
                         lambda c: pb16_ref[0, 0, c, :, 2 * LANES:3 * LANES],
                         group16, S // 16, 16, t * group16, _dil_bias(16, slope_a, slope_b),
                         True, False, online, *args)

        @pl.when((t >= n16) & (t < n16 + 4))
        def _():
            _dil_segment(lambda c, i0: pb4_ref[0, 0, 0, pl.ds(i0, LANES), 0:LANES],
                         lambda c: pb4_ref[0, 0, 0, :, LANES:2 * LANES],
                         lambda c: pb4_ref[0, 0, 0, :, 2 * LANES:3 * LANES],
                         1, S // 4, 4, t - n16, _dil_bias(4, slope_a, slope_b),
                         False, False, online, *args)

        @pl.when(t == n16 + 4)
        def _():
            _dil_segment(lambda c, i0: pb1_ref[0, 0, pl.ds(i0, LANES), 0:LANES],
                         lambda c: pb1_ref[0, 0, :, LANES:2 * LANES],
                         lambda c: pb1_ref[0, 0, :, 2 * LANES:3 * LANES],
                         1, S, 1, 0, _dil_bias(1, slope_a, slope_b),
                         False, True, online, *args)

    @pl.when(direct)
    def _():
        pattern(False)

    @pl.when(jnp.logical_not(direct))
    def _():
        pattern(True)


def _dil_attn(pb1, pb4, pb16, nrm, slopes, group16=4):
    B, P, S, _ = pb1.shape
    n16 = 16 // group16
    nt = nrm.shape[1]
    return pl.pallas_call(
        functools.partial(_dil_kernel, group16=group16),
        grid=(B, P, n16 + 4 + 1),
        in_specs=[
            pl.BlockSpec(memory_space=pltpu.SMEM),
            pl.BlockSpec((1, nt, 1, LANES), lambda b, p, t: (b, 0, 0, 0)),
            pl.BlockSpec((1, 1, S, PAIR_W), lambda b, p, t: (b, p, 0, 0)),
            pl.BlockSpec((1, 1, 1, S // 4, PAIR_W),
                         lambda b, p, t: (b, p, jnp.clip(t - n16, 0, 3), 0, 0)),
            pl.BlockSpec((1, 1, group16, S // 16, PAIR_W),
                         lambda b, p, t: (b, p, jnp.minimum(t, n16 - 1), 0, 0)),
        ],
        out_specs=pl.BlockSpec((1, 1, S, LANES), lambda b, p, t: (b, p, 0, 0)),
        out_shape=jax.ShapeDtypeStruct((B, P, S, LANES), F32),
        scratch_shapes=[
            pltpu.VMEM((S, LANES), F32),
            pltpu.VMEM((S, LANES), F32),
            pltpu.VMEM((S, LANES), F32),
            pltpu.VMEM((S + 2 * DIL_RADIUS, LANES), BF16),
            pltpu.VMEM((S + 2 * DIL_RADIUS, LANES), BF16),
            pltpu.VMEM((S + 2 * DIL_RADIUS, LANES), BF16),
        ],
        compiler_params=_params(("parallel", "parallel", "arbitrary")),
        name="dil_attn",
    )(slopes, nrm, pb1, pb4, pb16)


def _out_mlp_kernel(x_ref, oa_ref, ob_ref, gdil_ref, wout_ref, gmlp_ref, wup_ref,
                    wdown_ref, gfin_ref, y_ref, *, ff_chunk):
    ob = jnp.concatenate([ob_ref[0, p] for p in range(N_DIL_PAIRS)], axis=-1)
    ob = _rms(ob, gdil_ref[...])
    mix = jnp.concatenate([oa_ref[0], ob.astype(BF16)], axis=-1)
    x1 = x_ref[0] + jnp.dot(mix, wout_ref[...], preferred_element_type=F32)
    h = _rms(x1, gmlp_ref[...]).astype(BF16)
    y = x1
    for c in range(D_FF // ff_chunk):
        u = jnp.dot(h, wup_ref[:, c * ff_chunk:(c + 1) * ff_chunk], preferred_element_type=F32)
        u = jnp.square(jnp.maximum(u, 0.0)).astype(BF16)
        y = y + jnp.dot(u, wdown_ref[c * ff_chunk:(c + 1) * ff_chunk, :],
                        preferred_element_type=F32)
    y_ref[0] = _rms(y, gfin_ref[...])


def _out_mlp(x, oa, ob, g_dil, w_out, g_mlp, w_up, w_down, g_final, tm, ff_chunk):
    B, S, D = x.shape
    const = lambda shape: pl.BlockSpec(shape, lambda b, i: (0,) * len(shape),
                                       pipeline_mode=pl.Buffered(1))
    return pl.pallas_call(
        functools.partial(_out_mlp_kernel, ff_chunk=ff_chunk),
        grid=(B, S // tm),
        in_specs=[
            pl.BlockSpec((1, tm, D), lambda b, i: (b, i, 0)),
            pl.BlockSpec((1, tm, N_DIFF_HEADS * LANES), lambda b, i: (b, i, 0)),
            pl.BlockSpec((1, N_DIL_PAIRS, tm, LANES), lambda b, i: (b, 0, i, 0)),
            const((1, DIL_W)),
            const((2 * DIL_W, D)),
            const((1, D)),
            const((D, D_FF)),
            const((D_FF, D)),
            const((1, D)),
        ],
        out_specs=pl.BlockSpec((1, tm, D), lambda b, i: (b, i, 0)),
        out_shape=jax.ShapeDtypeStruct((B, S, D), F32),
        compiler_params=_params(("parallel", "parallel")),
        name="out_mlp",
    )(x, oa, ob, g_dil, w_out, g_mlp, w_up, w_down, g_final)


def _alibi_slopes(n):
    return 2.0 ** (-8.0 * jnp.arange(1, n + 1, dtype=F32) / n)


def _layer(x, g_mix, w_in, wq_t, wv_t, lam_qk, g_diff, g_dil, w_out, g_mlp, w_up, w_down,
           g_final, lam_init):
    qt, k, vt, pb1, pb4, pb16, nrm = _in_proj(x, g_mix, w_in, wq_t, wv_t, tm=512)
    oa = _diff_attn(qt, k, vt, nrm, _alibi_slopes(N_DIFF_HEADS), lam_qk, g_diff, lam_init,
                    bq=1024)
    assert DIL_PATTERNS == ((2 * DIL_RADIUS, 1), (8 * DIL_RADIUS, 4), (32 * DIL_RADIUS, 16))
    ob = _dil_attn(pb1, pb4, pb16, nrm, _alibi_slopes(N_DIL_HEADS))
    return _out_mlp(x, oa, ob, g_dil, w_out, g_mlp, w_up, w_down, g_final,
                    tm=512, ff_chunk=1024)


def kernel(x_prompt, x_sample, g_mix, w_in, lam_qk, g_diff, g_dil, w_out, g_mlp, w_up,
           w_down, g_final):
    assert g_mix.shape[0] == 1
    lam_init = 0.8 - 0.6 * math.exp(-0.3 * 0)
    w_in_b = w_in[0].astype(BF16)
    weights = (
        g_mix[0][None, :], w_in_b, w_in_b[:, 0:DIFF_QW].T, w_in_b[:, 2 * DIFF_QW:DIFF_W].T,
        lam_qk[0],
        g_diff[0], g_dil[0][None, :], w_out[0].astype(BF16), g_mlp[0][None, :],
        w_up[0].astype(BF16), w_down[0].astype(BF16), g_final[None, :],
    )
    return tuple(_layer(x, *weights, lam_init) for x in (x_prompt, x_sample))
```

```python
import functools
import math

import jax
import jax.numpy as jnp
from jax import lax
from jax.experimental import pallas as pl
from jax.experimental.pallas import tpu as pltpu

D_MODEL = 1024
HEAD_DIM = 64
N_DIFF_HEADS = 4
N_DIL_HEADS = 8
N_DIL_PAIRS = N_DIL_HEADS // 2
DIL_PATTERNS = ((128, 1), (512, 4), (2048, 16))
DIL_RADIUS = 64
D_FF = 4 * D_MODEL
EPS = 1e-5
DIFF_QW = N_DIFF_HEADS * 2 * HEAD_DIM
DIFF_W = 3 * DIFF_QW
DIL_W = N_DIL_HEADS * HEAD_DIM
IN_W = DIFF_W + 3 * DIL_W
LANES = 128
PAIR_W = 3 * LANES
NEG = -1e30
LOG2E = math.log2(math.e)
ONES_ROWS = 16
DIRECT_EXP_MAX_LOGIT = 60.0
BIAS_ROWS = 16
DIFF_MAX_TILES = 2
DIL_MAX_BLOCKS = 16
NRM_DIFF_K = 4 * N_DIL_PAIRS
NRM_DIFF_Q = NRM_DIFF_K + N_DIFF_HEADS
VMEM_LIMIT = 56 * 1024 * 1024

BF16 = jnp.bfloat16
F32 = jnp.float32


def _rms(x, g):
    return x * lax.rsqrt(jnp.mean(x * x, axis=-1, keepdims=True) + EPS) * g


def _params(sem):
    return pltpu.CompilerParams(dimension_semantics=sem, vmem_limit_bytes=VMEM_LIMIT)


def _in_proj_kernel(x_ref, g_ref, w_ref, wqt_ref, wvt_ref, qt_ref, k_ref, vt_ref,
                    pb1_ref, pb4_ref, pb16_ref, nrm_ref, slab_ref, slab4_ref):
    scale = HEAD_DIM ** -0.5
    nt = (((1,), (1,)), ((), ()))
    tm = x_ref.shape[1]
    lane = lax.broadcasted_iota(jnp.int32, (1, LANES), 1)
    nrm = jnp.zeros((1, LANES), F32)
    h = _rms(x_ref[0], g_ref[...]).astype(BF16)
    for c in range(3):
        p = jnp.dot(h, w_ref[:, DIFF_W + c * DIL_W:DIFF_W + (c + 1) * DIL_W],
                    preferred_element_type=F32)
        if c == 0:
            p = p * (scale * LOG2E)
        cols = slice(c * LANES, (c + 1) * LANES)
        for hp in range(N_DIL_PAIRS):
            ph = p[:, hp * LANES:(hp + 1) * LANES]
            if c < 2:
                sq = ph * ph
                for e in range(2):
                    mine = (lane >= e * HEAD_DIM) & (lane < (e + 1) * HEAD_DIM)
                    big = jnp.max(jnp.sum(jnp.where(mine, sq, 0.0), axis=1, keepdims=True),
                                  axis=0, keepdims=True)
                    nrm = jnp.where(lane == 4 * hp + 2 * e + c, big, nrm)
            pb1_ref[0, hp, :, cols] = ph.astype(BF16)
            s = c * N_DIL_PAIRS + hp
            slab_ref[s] = ph
            for g in range(4):
                v4 = slab_ref[s, pl.ds(g, tm // 4, stride=4), :]
                pb4_ref[0, hp, g, :, cols] = v4.astype(BF16)
                slab4_ref[s, g] = v4
            for g in range(4):
                for c2 in range(4):
                    v16 = slab4_ref[s, g, pl.ds(c2, tm // 16, stride=4), :]
                    pb16_ref[0, hp, 4 * c2 + g, :, cols] = v16.astype(BF16)
    kd = jnp.dot(h, w_ref[:, DIFF_QW:2 * DIFF_QW], preferred_element_type=F32)
    k_ref[0] = kd.astype(BF16)
    for hh in range(N_DIFF_HEADS):
        kh = kd[:, hh * LANES:(hh + 1) * LANES]
        big = jnp.max(jnp.sum(kh * kh, axis=1, keepdims=True), axis=0, keepdims=True)
        nrm = jnp.where(lane == NRM_DIFF_K + hh, big, nrm)
    qt = lax.dot_general(wqt_ref[...], h, nt, preferred_element_type=F32) * (scale * LOG2E)
    qt_ref[0] = qt.astype(BF16)
    for hm in range(2 * N_DIFF_HEADS):
        qm = qt[hm * HEAD_DIM:(hm + 1) * HEAD_DIM, :]
        big = jnp.max(jnp.sum(qm * qm, axis=0, keepdims=True), axis=1, keepdims=True)
        nrm = jnp.where(lane == NRM_DIFF_Q + hm, big, nrm)
    nrm_ref[0, 0] = nrm
    vt = lax.dot_general(wvt_ref[...], h, nt, preferred_element_type=F32).astype(BF16)
    for hh in range(N_DIFF_HEADS):
        vt_ref[0, hh, 0] = vt[hh * LANES:(hh + 1) * LANES, :]


def _in_proj(x, g_mix, w_in, wq_t, wv_t, tm):
    B, S, D = x.shape
    H = N_DIFF_HEADS
    P = N_DIL_PAIRS
    return pl.pallas_call(
        _in_proj_kernel,
        grid=(B, S // tm),
        in_specs=[
            pl.BlockSpec((1, tm, D), lambda b, i: (b, i, 0)),
            pl.BlockSpec((1, D), lambda b, i: (0, 0)),
            pl.BlockSpec((D, IN_W), lambda b, i: (0, 0)),
            pl.BlockSpec((DIFF_QW, D), lambda b, i: (0, 0)),
            pl.BlockSpec((DIFF_QW, D), lambda b, i: (0, 0)),
        ],
        out_specs=[
            pl.BlockSpec((1, DIFF_QW, tm), lambda b, i: (b, 0, i)),
            pl.BlockSpec((1, tm, DIFF_QW), lambda b, i: (b, i, 0)),
            pl.BlockSpec((1, H, 1, LANES, tm), lambda b, i: (b, 0, i, 0, 0)),
            pl.BlockSpec((1, P, tm, PAIR_W), lambda b, i: (b, 0, i, 0)),
            pl.BlockSpec((1, P, 4, tm // 4, PAIR_W), lambda b, i: (b, 0, 0, i, 0)),
            pl.BlockSpec((1, P, 16, tm // 16, PAIR_W), lambda b, i: (b, 0, 0, i, 0)),
            pl.BlockSpec((1, 1, 1, LANES), lambda b, i: (b, i, 0, 0)),
        ],
        out_shape=[
            jax.ShapeDtypeStruct((B, DIFF_QW, S), BF16),
            jax.ShapeDtypeStruct((B, S, DIFF_QW), BF16),
            jax.ShapeDtypeStruct((B, H, S // tm, LANES, tm), BF16),
            jax.ShapeDtypeStruct((B, P, S, PAIR_W), BF16),
            jax.ShapeDtypeStruct((B, P, 4, S // 4, PAIR_W), BF16),
            jax.ShapeDtypeStruct((B, P, 16, S // 16, PAIR_W), BF16),
            jax.ShapeDtypeStruct((B, S // tm, 1, LANES), F32),
        ],
        scratch_shapes=[
            pltpu.VMEM((3 * P, tm, LANES), F32),
            pltpu.VMEM((3 * P, 4, tm // 4, LANES), F32),
        ],
        compiler_params=_params(("parallel", "parallel")),
        name="in_proj",
    )(x, g_mix, w_in, wq_t, wv_t)


def _diff_attn_kernel(slopes_ref, lam_ref, g_ref, nrm_ref, qt_ref, k_ref, vt_ref, o_ref,
                      q2t_ref, m_ref, acc_ref, cb_ref, feat_ref, *, bq, bk, lam_init):
    G = cb_ref.shape[1]
    h = pl.program_id(1)
    qi = pl.program_id(2)
    nkv = k_ref.shape[1] // bk
    slope2 = slopes_ref[h] * LOG2E

    qt = qt_ref[0]
    sub = lax.broadcasted_iota(jnp.int32, qt.shape, 0)
    zero = jnp.zeros_like(qt)
    q2t_ref[0:LANES, 0:bq] = jnp.where(sub < HEAD_DIM, qt, zero)
    q2t_ref[0:LANES, bq:2 * bq] = jnp.where(sub >= HEAD_DIM, qt, zero)
    max_tiles = feat_ref.shape[0]
    assert max_tiles * BIAS_ROWS <= LANES
    q2t_ref[LANES:2 * LANES, :] = jnp.zeros((LANES, 2 * bq), BF16)
    f_row = lax.broadcasted_iota(jnp.int32, (bk, LANES), 0)
    f_lane = lax.broadcasted_iota(jnp.int32, (bk, LANES), 1)
    f_slot = f_lane % BIAS_ROWS
    feat = jnp.where(f_slot < 2, f_row % 256,
                     jnp.where(f_slot < 4, f_row // 256,
                               jnp.where(f_slot < 7, 1, 0)))
    for tt in range(max_tiles):
        mine = f_lane // BIAS_ROWS == tt
        feat_ref[tt] = jnp.where(mine, feat, 0).astype(F32).astype(BF16)
    m_ref[...] = jnp.full(m_ref.shape, -jnp.inf, F32)
    acc_ref[...] = jnp.zeros(acc_ref.shape, F32)
    key_off = lax.broadcasted_iota(jnp.int32, (bk, G), 0).astype(F32)
    cb_ref[...] = slope2 * key_off
    lane = lax.broadcasted_iota(jnp.int32, (1, 2 * bq), 1)
    pos_q = (qi * bq + jnp.where(lane >= bq, lane - bq, lane)).astype(F32)
    ones = jnp.ones((ONES_ROWS, bk), BF16)

    q_rows = nrm_ref.shape[1] * bq // k_ref.shape[1]
    k_big = jnp.max(nrm_ref[0, :, 0, :], axis=0, keepdims=True)
    q_big = jnp.max(nrm_ref[0, pl.ds(qi * q_rows, q_rows), 0, :], axis=0, keepdims=True)
    nlane = lax.broadcasted_iota(jnp.int32, k_big.shape, 1)
    k_max = jnp.max(jnp.where(nlane == NRM_DIFF_K + h, k_big, 0.0))
    q_mine = (nlane == NRM_DIFF_Q + 2 * h) | (nlane == NRM_DIFF_Q + 2 * h + 1)
    q_max = jnp.max(jnp.where(q_mine, q_big, 0.0))
    direct = q_max * k_max <= DIRECT_EXP_MAX_LOGIT ** 2

    def online_tile(j, side):
        start = pl.multiple_of(j * bk, bk)
        k_t = k_ref[0, pl.ds(start, bk), :]
        vt1 = jnp.concatenate([vt_ref[0, 0, j], ones], axis=0)
        j0 = (j * bk).astype(F32)
        groups = [slice(gi * G, (gi + 1) * G) for gi in range(2 * bq // G)]
        scores = [jnp.dot(k_t, q2t_ref[0:LANES, sl], preferred_element_type=F32)
                  for sl in groups]
        for sl, s in zip(groups, scores):
            pq = pos_q[:, sl]
            if side == 0:
                u = s - slope2 * jnp.abs(pq - (j0 + key_off))
                a = jnp.zeros_like(pq)
            elif side < 0:
                u = s + cb_ref[...]
                a = slope2 * (j0 - pq)
            else:
                u = s - cb_ref[...]
                a = slope2 * (pq - j0)
            m_old = m_ref[:, sl]
            m_new = jnp.maximum(m_old, jnp.max(u, axis=0, keepdims=True) + a)
            alpha = jnp.exp2(m_old - m_new)
            p = jnp.exp2(u - (m_new - a)).astype(BF16)
            acc_ref[:, sl] = alpha * acc_ref[:, sl] + jnp.dot(
                vt1, p, preferred_element_type=F32)
            m_ref[:, sl] = m_new

    def direct_tiles(j, n, side):
        groups = [slice(gi * G, (gi + 1) * G) for gi in range(2 * bq // G)]

        def bias_rows(j0):
            def bf16_part(x):
                return x.astype(BF16).astype(F32)

            sgn = -float(side)
            sv = jnp.zeros_like(pos_q) + sgn * slope2
            s_hi = bf16_part(sv)
            s_lo = bf16_part(sv - s_hi)
            a = sv * (j0 - pos_q)
            a1 = bf16_part(a)
            a2 = bf16_part(a - a1)
            a3 = bf16_part(a - a1 - a2)
            pieces = [s_hi, s_lo, s_hi * 256.0, s_lo * 256.0, a1, a2, a3]
            row = lax.broadcasted_iota(jnp.int32, (BIAS_ROWS, 2 * bq), 0)
            w = jnp.zeros((BIAS_ROWS, 2 * bq), F32)
            for i, piece in enumerate(pieces):
                w = jnp.where(row == i, piece, w)
            return w.astype(BF16)

        def operands(tt):
            jj = j + tt
            k_t = k_ref[0, pl.ds(pl.multiple_of(jj * bk, bk), bk), :]
            j0 = (jj * bk).astype(F32)
            if side != 0:
                q2t_ref[LANES + tt * BIAS_ROWS:LANES + (tt + 1) * BIAS_ROWS, :] = bias_rows(j0)
                k_t = jnp.concatenate([k_t, feat_ref[tt]], axis=1)
            return k_t, vt_ref[0, 0, jj], j0

        def score(k_t, sl):
            rows = slice(0, k_t.shape[1])
            return jnp.dot(k_t, q2t_ref[rows, sl], preferred_element_type=F32)

        def pv(vt, j0, sl, s):
            if side == 0:
                s = s - slope2 * jnp.abs(pos_q[:, sl] - (j0 + key_off))
            p = jnp.exp2(s)
            return (jnp.dot(vt, p.astype(BF16), preferred_element_type=F32),
                    jnp.sum(p, axis=0, keepdims=True))

        tiles = [operands(tt) for tt in range(n)]
        scores = [score(tiles[0][0], sl) for sl in groups]
        sums = [None] * len(groups)
        for tt in range(n):
            _, vt, j0 = tiles[tt]
            nxt = []
            for gi, sl in enumerate(groups):
                x, lx = pv(vt, j0, sl, scores[gi])
                sums[gi] = (x, lx) if tt == 0 else (sums[gi][0] + x, sums[gi][1] + lx)
                if tt + 1 < n:
                    nxt.append(score(tiles[tt + 1][0], sl))
            scores = nxt
        for sl, (x, lx) in zip(groups, sums):
            acc_ref[0:LANES, sl] += x
            acc_ref[LANES:LANES + 1, sl] += lx

    def all_tiles(online):
        ndiag = bq // bk
        if online:
            def before(j, carry):
                online_tile(j, -1)
                return carry

            def after(j, carry):
                online_tile(j, 1)
                return carry

            lax.fori_loop(0, qi * ndiag, before, 0)
            for d in range(ndiag):
                online_tile(qi * ndiag + d, 0)
            lax.fori_loop((qi + 1) * ndiag, nkv, after, 0)
            return

        assert max_tiles % 2 == 0 and ndiag % 2 == 0 and nkv % 2 == 0

        def run(first, count, side):
            def body(t, carry):
                direct_tiles(first + t * max_tiles, max_tiles, side)
                return carry

            full = count // max_tiles
            lax.fori_loop(0, full, body, 0)
            for extra in range(2, max_tiles, 2):
                @pl.when(count - full * max_tiles == extra)
                def _():
                    direct_tiles(first + full * max_tiles, extra, side)

        run(0, qi * ndiag, -1)
        for d in range(0, ndiag, 2):
            direct_tiles(qi * ndiag + d, 2, 0)
        run((qi + 1) * ndiag, nkv - (qi + 1) * ndiag, 1)

    @pl.when(direct)
    def _():
        all_tiles(False)

    @pl.when(jnp.logical_not(direct))
    def _():
        all_tiles(True)

    lq = lam_ref[...]
    lam = (jnp.exp(jnp.sum(lq[0:1] * lq[1:2], axis=1, keepdims=True))
           - jnp.exp(jnp.sum(lq[2:3] * lq[3:4], axis=1, keepdims=True)) + lam_init)
    acc = acc_ref[...]
    ot = acc[0:LANES] * (1.0 / acc[LANES:LANES + 1])
    od = (ot[:, 0:bq] - lam * ot[:, bq:2 * bq]).T
    od = od * lax.rsqrt(jnp.mean(od * od, axis=-1, keepdims=True) + EPS)
    od = od * g_ref[pl.ds(h, 1), :] * (1.0 - lam_init)
    o_ref[0] = od.astype(BF16)


def _diff_attn(qt, k, vt, nrm, slopes, lam_qk, g_diff, lam_init, bq):
    B, S, _ = k.shape
    H = N_DIFF_HEADS
    nkv, bk = vt.shape[2], vt.shape[4]
    assert bq % bk == 0
    kern = functools.partial(_diff_attn_kernel, bq=bq, bk=bk, lam_init=lam_init)
    return pl.pallas_call(
        kern,
        grid=(B, H, S // bq),
        in_specs=[
            pl.BlockSpec(memory_space=pltpu.SMEM),
            pl.BlockSpec((4, HEAD_DIM), lambda b, h, i: (0, 0)),
            pl.BlockSpec((H, LANES), lambda b, h, i: (0, 0)),
            pl.BlockSpec((1, nrm.shape[1], 1, LANES), lambda b, h, i: (b, 0, 0, 0)),
            pl.BlockSpec((1, LANES, bq), lambda b, h, i: (b, h, i)),
            pl.BlockSpec((1, S, LANES), lambda b, h, i: (b, 0, h)),
            pl.BlockSpec((1, 1, nkv, LANES, bk), lambda b, h, i: (b, h, 0, 0, 0)),
        ],
        out_specs=pl.BlockSpec((1, bq, LANES), lambda b, h, i: (b, i, h)),
        out_shape=jax.ShapeDtypeStruct((B, S, H * LANES), BF16),
        scratch_shapes=[
            pltpu.VMEM((2 * LANES, 2 * bq), BF16),
            pltpu.VMEM((1, 2 * bq), F32),
            pltpu.VMEM((LANES + ONES_ROWS, 2 * bq), F32),
            pltpu.VMEM((bk, 8 * LANES), F32),
            pltpu.VMEM((DIFF_MAX_TILES, bk, LANES), BF16),
        ],
        compiler_params=_params(("parallel", "parallel", "arbitrary")),
        name="diff_attn",
    )(slopes, lam_qk, g_diff, nrm, qt, k, vt)


def _dil_bias(dil, slope_a, slope_b):
    QB, R = LANES, DIL_RADIUS
    KW = QB + 2 * R
    r_i = lax.broadcasted_iota(jnp.int32, (QB, KW), 0)
    c_i = lax.broadcasted_iota(jnp.int32, (QB, KW), 1)
    rel = c_i - r_i - R
    band = (rel >= -R) & (rel <= R)
    dist = (jnp.abs(rel) * dil).astype(F32)
    return jnp.concatenate([jnp.where(band, -(slope_a * LOG2E) * dist, NEG),
                            jnp.where(band, -(slope_b * LOG2E) * dist, NEG)], axis=0)


def _dil_segment(q_at, k_at, v_at, nres, L, dil, tok0, bias_at, first, last, online,
                 sa_ref, sb_ref, m_ref, o_ref, kpad, vpad_a, vpad_b):
    QB, R = LANES, DIL_RADIUS
    KW = QB + 2 * R
    pitch = L + 2 * R
    per_class = L // QB
    lane_q = lax.broadcasted_iota(jnp.int32, (QB, LANES), 1)
    head_a = lane_q < HEAD_DIM
    c2 = lax.broadcasted_iota(jnp.int32, (2 * QB, KW), 1)

    zpad = jnp.zeros((R, LANES), BF16)
    own_a = lax.broadcasted_iota(jnp.int32, (L, LANES), 1) < HEAD_DIM
    one = jnp.ones((L, LANES), BF16)
    for c in range(nres):
        base = c * pitch
        for pad in (kpad, vpad_a, vpad_b):
            pad[base:base + R, :] = zpad
            pad[base + R + L:base + pitch, :] = zpad
        kpad[base + R:base + R + L, :] = k_at(c)
        v_src = v_at(c)
        vpad_a[base + R:base + R + L, :] = jnp.where(own_a, v_src, one)
        vpad_b[base + R:base + R + L, :] = jnp.where(own_a, one, v_src)

    def rows(c, i0):
        if dil == 1:
            return pl.ds(i0, QB)
        return pl.ds(tok0 + c + dil * i0, QB, stride=dil)

    def scores(c, i0):
        q = q_at(c, i0)
        zero = jnp.zeros_like(q)
        q2 = jnp.concatenate([jnp.where(head_a, q, zero), jnp.where(head_a, zero, q)], axis=0)
        s = lax.dot_general(q2, kpad[pl.ds(c * pitch + i0, KW), :], (((1,), (1,)), ((), ())),
                            preferred_element_type=F32)
        return s + bias_at[...]

    def finish(c, i0, s):
        if online:
            key = i0 - R + c2
            s = jnp.where((key >= 0) & (key < L), s, NEG)
            m_c = jnp.max(s, axis=1, keepdims=True)
            s = s - m_c
        p = jnp.exp2(s).astype(BF16)
        win = pl.ds(c * pitch + i0, KW)
        sa = jnp.dot(p[0:QB], vpad_a[win, :], preferred_element_type=F32)
        sb = jnp.dot(p[QB:2 * QB], vpad_b[win, :], preferred_element_type=F32)
        r = rows(c, i0)
        if online:
            ma = jnp.broadcast_to(m_c[0:QB], (QB, LANES))
            mb = jnp.broadcast_to(m_c[QB:2 * QB], (QB, LANES))
            if not first:
                m_p = m_ref[r, :]
                mpa = jnp.max(jnp.where(head_a, m_p, -jnp.inf), axis=1, keepdims=True)
                mpb = jnp.max(jnp.where(head_a, -jnp.inf, m_p), axis=1, keepdims=True)
                mna, mnb = jnp.maximum(mpa, ma), jnp.maximum(mpb, mb)
                sa = sa_ref[r, :] * jnp.exp2(mpa - mna) + sa * jnp.exp2(ma - mna)
                sb = sb_ref[r, :] * jnp.exp2(mpb - mnb) + sb * jnp.exp2(mb - mnb)
                ma, mb = mna, mnb
            if not last:
                m_ref[r, :] = jnp.where(head_a, ma, mb)
        elif not first:
            sa = sa_ref[r, :] + sa
            sb = sb_ref[r, :] + sb
        if last:
            half = LANES // 2
            o_ref[0, 0, r, :] = jnp.where(head_a, sa / pltpu.roll(sa, half, 1),
                                          sb / pltpu.roll(sb, half, 1))
        else:
            sa_ref[r, :] = sa
            sb_ref[r, :] = sb

    def blocks(todo):
        ss = [scores(c, i0) for c, i0 in todo]
        for (c, i0), s in zip(todo, ss):
            finish(c, i0, s)

    if online:
        for c in range(nres):
            def body(n, carry, c=c):
                blocks([(c, pl.multiple_of(n * QB, QB))])
                return carry
            lax.fori_loop(0, per_class, body, 0)
    elif per_class <= DIL_MAX_BLOCKS:
        todo = [(c, u * QB) for c in range(nres) for u in range(per_class)]
        for at in range(0, len(todo), DIL_MAX_BLOCKS):
            blocks(todo[at:at + DIL_MAX_BLOCKS])
    else:
        assert nres == 1 and per_class % DIL_MAX_BLOCKS == 0

        def body(n, carry):
            blocks([(0, pl.multiple_of((n * DIL_MAX_BLOCKS + u) * QB, QB))
                    for u in range(DIL_MAX_BLOCKS)])
            return carry

        lax.fori_loop(0, per_class // DIL_MAX_BLOCKS, body, 0)


def _dil_kernel(slopes_ref, nrm_ref, pb1_ref, pb4_ref, pb16_ref, o_ref,
                sa_ref, sb_ref, m_ref, kpad, vpad_a, vpad_b, bias_ref, *, group16, group4):
    hp = pl.program_id(1)
    t = pl.program_id(2)
    S = pb1_ref.shape[2]
    n16, n4 = 16 // group16, 4 // group4
    dils = (16, 4, 1)

    @pl.when(t == 0)
    def _():
        for i, dil in enumerate(dils):
            bias_ref[i] = _dil_bias(dil, slopes_ref[2 * hp], slopes_ref[2 * hp + 1])

    big = jnp.max(nrm_ref[0, :, 0, :], axis=0, keepdims=True)
    lane = lax.broadcasted_iota(jnp.int32, big.shape, 1)

    def pick(idx):
        return jnp.max(jnp.where(lane == idx, big, 0.0))

    direct = ((pick(4 * hp) * pick(4 * hp + 1) <= DIRECT_EXP_MAX_LOGIT ** 2)
              & (pick(4 * hp + 2) * pick(4 * hp + 3) <= DIRECT_EXP_MAX_LOGIT ** 2))

    def pattern(online):
        args = (sa_ref, sb_ref, m_ref, o_ref, kpad, vpad_a, vpad_b)

        @pl.when(t < n16)
        def _():
            _dil_segment(lambda c, i0: pb16_ref[0, 0, c, pl.ds(i0, LANES), 0:LANES],
                         lambda c: pb16_ref[0, 0, c, :, LANES:2 * LANES],
                         lambda c: pb16_ref[0, 0, c, :, 2 * LANES:3 * LANES],
                         group16, S // 16, 16, t * group16, bias_ref.at[0],
                         True, False, online, *args)

        @pl.when((t >= n16) & (t < n16 + n4))
        def _():
            _dil_segment(lambda c, i0: pb4_ref[0, 0, c, pl.ds(i0, LANES), 0:LANES],
                         lambda c: pb4_ref[0, 0, c, :, LANES:2 * LANES],
                         lambda c: pb4_ref[0, 0, c, :, 2 * LANES:3 * LANES],
                         group4, S // 4, 4, (t - n16) * group4, bias_ref.at[1],
                         False, False, online, *args)

        @pl.when(t == n16 + n4)
        def _():
            _dil_segment(lambda c, i0: pb1_ref[0, 0, pl.ds(i0, LANES), 0:LANES],
                         lambda c: pb1_ref[0, 0, :, LANES:2 * LANES],
                         lambda c: pb1_ref[0, 0, :, 2 * LANES:3 * LANES],
                         1, S, 1, 0, bias_ref.at[2],
                         False, True, online, *args)

    @pl.when(direct)
    def _():
        pattern(False)

    @pl.when(jnp.logical_not(direct))
    def _():
        pattern(True)


def _dil_attn(pb1, pb4, pb16, nrm, slopes, group16=8, group4=2):
    B, P, S, _ = pb1.shape
    n16, n4 = 16 // group16, 4 // group4
    nt = nrm.shape[1]
    pad_rows = max(group16 * (S // 16 + 2 * DIL_RADIUS), group4 * (S // 4 + 2 * DIL_RADIUS),
                   S + 2 * DIL_RADIUS)
    return pl.pallas_call(
        functools.partial(_dil_kernel, group16=group16, group4=group4),
        grid=(B, P, n16 + n4 + 1),
        in_specs=[
            pl.BlockSpec(memory_space=pltpu.SMEM),
            pl.BlockSpec((1, nt, 1, LANES), lambda b, p, t: (b, 0, 0, 0)),
            pl.BlockSpec((1, 1, S, PAIR_W), lambda b, p, t: (b, p, 0, 0)),
            pl.BlockSpec((1, 1, group4, S // 4, PAIR_W),
                         lambda b, p, t: (b, p, jnp.clip(t - n16, 0, n4 - 1), 0, 0)),
            pl.BlockSpec((1, 1, group16, S // 16, PAIR_W),
                         lambda b, p, t: (b, p, jnp.minimum(t, n16 - 1), 0, 0)),
        ],
        out_specs=pl.BlockSpec((1, 1, S, LANES), lambda b, p, t: (b, p, 0, 0)),
        out_shape=jax.ShapeDtypeStruct((B, P, S, LANES), F32),
        scratch_shapes=[
            pltpu.VMEM((S, LANES), F32),
            pltpu.VMEM((S, LANES), F32),
            pltpu.VMEM((S, LANES), F32),
            pltpu.VMEM((pad_rows, LANES), BF16),
            pltpu.VMEM((pad_rows, LANES), BF16),
            pltpu.VMEM((pad_rows, LANES), BF16),
            pltpu.VMEM((3, 2 * LANES, LANES + 2 * DIL_RADIUS), F32),
        ],
        compiler_params=_params(("parallel", "parallel", "arbitrary")),
        name="dil_attn",
    )(slopes, nrm, pb1, pb4, pb16)


def _out_mlp_kernel(x_ref, oa_ref, ob_ref, gdil_ref, wout_ref, gmlp_ref, wup_ref,
                    wdown_ref, gfin_ref, y_ref, *, ff_chunk):
    ob = jnp.concatenate([ob_ref[0, p] for p in range(N_DIL_PAIRS)], axis=-1)
    ob = _rms(ob, gdil_ref[...])
    mix = jnp.concatenate([oa_ref[0], ob.astype(BF16)], axis=-1)
    x1 = x_ref[0] + jnp.dot(mix, wout_ref[...], preferred_element_type=F32)
    h = _rms(x1, gmlp_ref[...]).astype(BF16)
    y = x1
    for c in range(D_FF // ff_chunk):
        u = jnp.dot(h, wup_ref[:, c * ff_chunk:(c + 1) * ff_chunk], preferred_element_type=F32)
        u = jnp.square(jnp.maximum(u, 0.0)).astype(BF16)
        y = y + jnp.dot(u, wdown_ref[c * ff_chunk:(c + 1) * ff_chunk, :],
                        preferred_element_type=F32)
    y_ref[0] = _rms(y, gfin_ref[...])


def _out_mlp(x, oa, ob, g_dil, w_out, g_mlp, w_up, w_down, g_final, tm, ff_chunk):
    B, S, D = x.shape
    const = lambda shape: pl.BlockSpec(shape, lambda b, i: (0,) * len(shape),
                                       pipeline_mode=pl.Buffered(1))
    return pl.pallas_call(
        functools.partial(_out_mlp_kernel, ff_chunk=ff_chunk),
        grid=(B, S // tm),
        in_specs=[
            pl.BlockSpec((1, tm, D), lambda b, i: (b, i, 0)),
            pl.BlockSpec((1, tm, N_DIFF_HEADS * LANES), lambda b, i: (b, i, 0)),
            pl.BlockSpec((1, N_DIL_PAIRS, tm, LANES), lambda b, i: (b, 0, i, 0)),
            const((1, DIL_W)),
            const((2 * DIL_W, D)),
            const((1, D)),
            const((D, D_FF)),
            const((D_FF, D)),
            const((1, D)),
        ],
        out_specs=pl.BlockSpec((1, tm, D), lambda b, i: (b, i, 0)),
        out_shape=jax.ShapeDtypeStruct((B, S, D), F32),
        compiler_params=_params(("parallel", "parallel")),
        name="out_mlp",
    )(x, oa, ob, g_dil, w_out, g_mlp, w_up, w_down, g_final)


def _alibi_slopes(n):
    return 2.0 ** (-8.0 * jnp.arange(1, n + 1, dtype=F32) / n)


def _layer(x, g_mix, w_in, wq_t, wv_t, lam_qk, g_diff, g_dil, w_out, g_mlp, w_up, w_down,
           g_final, lam_init):
    qt, k, vt, pb1, pb4, pb16, nrm = _in_proj(x, g_mix, w_in, wq_t, wv_t, tm=512)
    oa = _diff_attn(qt, k, vt, nrm, _alibi_slopes(N_DIFF_HEADS), lam_qk, g_diff, lam_init,
                    bq=1024)
    assert DIL_PATTERNS == ((2 * DIL_RADIUS, 1), (8 * DIL_RADIUS, 4), (32 * DIL_RADIUS, 16))
    ob = _dil_attn(pb1, pb4, pb16, nrm, _alibi_slopes(N_DIL_HEADS))
    return _out_mlp(x, oa, ob, g_dil, w_out, g_mlp, w_up, w_down, g_final,
                    tm=512, ff_chunk=1024)


def kernel(x_prompt, x_sample, g_mix, w_in, lam_qk, g_diff, g_dil, w_out, g_mlp, w_up,
           w_down, g_final):
    assert g_mix.shape[0] == 1
    lam_init = 0.8 - 0.6 * math.exp(-0.3 * 0)
    w_in_b = w_in[0].astype(BF16)
    weights = (
        g_mix[0][None, :], w_in_b, w_in_b[:, 0:DIFF_QW].T, w_in_b[:, 2 * DIFF_QW:DIFF_W].T,
        lam_qk[0],
        g_diff[0], g_dil[0][None, :], w_out[0].astype(BF16), g_mlp[0][None, :],
        w_up[0].astype(BF16), w_down[0].astype(BF16), g_final[None, :],
    )
    return tuple(_layer(x, *weights, lam_init) for x in (x_prompt, x_sample))
```

```python
import functools
import math

import jax
import jax.numpy as jnp
from jax import lax
from jax.experimental import pallas as pl
from jax.experimental.pallas import tpu as pltpu

D_MODEL = 1024
HEAD_DIM = 64
N_DIFF_HEADS = 4
N_DIL_HEADS = 8
N_DIL_PAIRS = N_DIL_HEADS // 2
DIL_PATTERNS = ((128, 1), (512, 4), (2048, 16))
DIL_RADIUS = 64
D_FF = 4 * D_MODEL
EPS = 1e-5
DIFF_QW = N_DIFF_HEADS * 2 * HEAD_DIM
DIFF_W = 3 * DIFF_QW
DIL_W = N_DIL_HEADS * HEAD_DIM
IN_W = DIFF_W + 3 * DIL_W
LANES = 128
PAIR_W = 3 * LANES
NEG = -1e30
LOG2E = math.log2(math.e)
ONES_ROWS = 16
DIRECT_EXP_MAX_LOGIT = 60.0
BIAS_ROWS = 16
DIFF_MAX_TILES = 2
DIL_MAX_BLOCKS = 16
NRM_DIFF_K = 4 * N_DIL_PAIRS
NRM_DIFF_Q = NRM_DIFF_K + N_DIFF_HEADS
VMEM_LIMIT = 56 * 1024 * 1024

BF16 = jnp.bfloat16
F32 = jnp.float32


def _rms(x, g):
    return x * lax.rsqrt(jnp.mean(x * x, axis=-1, keepdims=True) + EPS) * g


def _params(sem):
    return pltpu.CompilerParams(dimension_semantics=sem, vmem_limit_bytes=VMEM_LIMIT)


def _in_proj_kernel(x_ref, g_ref, w_ref, wqt_ref, wvt_ref, qt_ref, k_ref, vt_ref,
                    pb1_ref, pb4_ref, pb16_ref, nrm_ref, slab_ref, slab4_ref):
    scale = HEAD_DIM ** -0.5
    nt = (((1,), (1,)), ((), ()))
    tm = x_ref.shape[1]
    lane = lax.broadcasted_iota(jnp.int32, (1, LANES), 1)
    nrm = jnp.zeros((1, LANES), F32)
    h = _rms(x_ref[0], g_ref[...]).astype(BF16)
    for c in range(3):
        p = jnp.dot(h, w_ref[:, DIFF_W + c * DIL_W:DIFF_W + (c + 1) * DIL_W],
                    preferred_element_type=F32)
        if c == 0:
            p = p * (scale * LOG2E)
        cols = slice(c * LANES, (c + 1) * LANES)
        for hp in range(N_DIL_PAIRS):
            ph = p[:, hp * LANES:(hp + 1) * LANES]
            if c < 2:
                sq = ph * ph
                for e in range(2):
                    mine = (lane >= e * HEAD_DIM) & (lane < (e + 1) * HEAD_DIM)
                    big = jnp.max(jnp.sum(jnp.where(mine, sq, 0.0), axis=1, keepdims=True),
                                  axis=0, keepdims=True)
                    nrm = jnp.where(lane == 4 * hp + 2 * e + c, big, nrm)
            pb1_ref[0, hp, :, cols] = ph.astype(BF16)
            s = c * N_DIL_PAIRS + hp
            slab_ref[s] = ph
            for g in range(4):
                v4 = slab_ref[s, pl.ds(g, tm // 4, stride=4), :]
                pb4_ref[0, hp, g, :, cols] = v4.astype(BF16)
                slab4_ref[s, g] = v4
            for g in range(4):
                for c2 in range(4):
                    v16 = slab4_ref[s, g, pl.ds(c2, tm // 16, stride=4), :]
                    pb16_ref[0, hp, 4 * c2 + g, :, cols] = v16.astype(BF16)
    kd = jnp.dot(h, w_ref[:, DIFF_QW:2 * DIFF_QW], preferred_element_type=F32)
    k_ref[0] = kd.astype(BF16)
    for hh in range(N_DIFF_HEADS):
        kh = kd[:, hh * LANES:(hh + 1) * LANES]
        big = jnp.max(jnp.sum(kh * kh, axis=1, keepdims=True), axis=0, keepdims=True)
        nrm = jnp.where(lane == NRM_DIFF_K + hh, big, nrm)
    qt = lax.dot_general(wqt_ref[...], h, nt, preferred_element_type=F32) * (scale * LOG2E)
    qt_ref[0] = qt.astype(BF16)
    for hm in range(2 * N_DIFF_HEADS):
        qm = qt[hm * HEAD_DIM:(hm + 1) * HEAD_DIM, :]
        big = jnp.max(jnp.sum(qm * qm, axis=0, keepdims=True), axis=1, keepdims=True)
        nrm = jnp.where(lane == NRM_DIFF_Q + hm, big, nrm)
    nrm_ref[0, 0] = nrm
    vt = lax.dot_general(wvt_ref[...], h, nt, preferred_element_type=F32).astype(BF16)
    for hh in range(N_DIFF_HEADS):
        vt_ref[0, hh, 0] = vt[hh * LANES:(hh + 1) * LANES, :]


def _in_proj(x, g_mix, w_in, wq_t, wv_t, tm):
    B, S, D = x.shape
    H = N_DIFF_HEADS
    P = N_DIL_PAIRS
    return pl.pallas_call(
        _in_proj_kernel,
        grid=(B, S // tm),
        in_specs=[
            pl.BlockSpec((1, tm, D), lambda b, i: (b, i, 0)),
            pl.BlockSpec((1, D), lambda b, i: (0, 0)),
            pl.BlockSpec((D, IN_W), lambda b, i: (0, 0)),
            pl.BlockSpec((DIFF_QW, D), lambda b, i: (0, 0)),
            pl.BlockSpec((DIFF_QW, D), lambda b, i: (0, 0)),
        ],
        out_specs=[
            pl.BlockSpec((1, DIFF_QW, tm), lambda b, i: (b, 0, i)),
            pl.BlockSpec((1, tm, DIFF_QW), lambda b, i: (b, i, 0)),
            pl.BlockSpec((1, H, 1, LANES, tm), lambda b, i: (b, 0, i, 0, 0)),
            pl.BlockSpec((1, P, tm, PAIR_W), lambda b, i: (b, 0, i, 0)),
            pl.BlockSpec((1, P, 4, tm // 4, PAIR_W), lambda b, i: (b, 0, 0, i, 0)),
            pl.BlockSpec((1, P, 16, tm // 16, PAIR_W), lambda b, i: (b, 0, 0, i, 0)),
            pl.BlockSpec((1, 1, 1, LANES), lambda b, i: (b, i, 0, 0)),
        ],
        out_shape=[
            jax.ShapeDtypeStruct((B, DIFF_QW, S), BF16),
            jax.ShapeDtypeStruct((B, S, DIFF_QW), BF16),
            jax.ShapeDtypeStruct((B, H, S // tm, LANES, tm), BF16),
            jax.ShapeDtypeStruct((B, P, S, PAIR_W), BF16),
            jax.ShapeDtypeStruct((B, P, 4, S // 4, PAIR_W), BF16),
            jax.ShapeDtypeStruct((B, P, 16, S // 16, PAIR_W), BF16),
            jax.ShapeDtypeStruct((B, S // tm, 1, LANES), F32),
        ],
        scratch_shapes=[
            pltpu.VMEM((3 * P, tm, LANES), F32),
            pltpu.VMEM((3 * P, 4, tm // 4, LANES), F32),
        ],
        compiler_params=_params(("parallel", "parallel")),
        name="in_proj",
    )(x, g_mix, w_in, wq_t, wv_t)


def _diff_attn_kernel(slopes_ref, lam_ref, g_ref, nrm_ref, qt_ref, k_ref, vt_ref, o_ref,
                      q2t_ref, m_ref, acc_ref, cb_ref, feat_ref, *, bq, bk, lam_init):
    G = cb_ref.shape[1]
    h = pl.program_id(1)
    qi = pl.program_id(2)
    nkv = k_ref.shape[1] // bk
    slope2 = slopes_ref[h] * LOG2E

    qt = qt_ref[0]
    sub = lax.broadcasted_iota(jnp.int32, qt.shape, 0)
    zero = jnp.zeros_like(qt)
    q2t_ref[0:LANES, 0:bq] = jnp.where(sub < HEAD_DIM, qt, zero)
    q2t_ref[0:LANES, bq:2 * bq] = jnp.where(sub >= HEAD_DIM, qt, zero)
    max_tiles = feat_ref.shape[0]
    assert max_tiles * BIAS_ROWS <= LANES
    q2t_ref[LANES:2 * LANES, :] = jnp.zeros((LANES, 2 * bq), BF16)
    f_row = lax.broadcasted_iota(jnp.int32, (bk, LANES), 0)
    f_lane = lax.broadcasted_iota(jnp.int32, (bk, LANES), 1)
    f_slot = f_lane % BIAS_ROWS
    feat = jnp.where(f_slot < 2, f_row % 256,
                     jnp.where(f_slot < 4, f_row // 256,
                               jnp.where(f_slot < 7, 1, 0)))
    for tt in range(max_tiles):
        mine = f_lane // BIAS_ROWS == tt
        feat_ref[tt] = jnp.where(mine, feat, 0).astype(F32).astype(BF16)
    acc_ref[...] = jnp.zeros(acc_ref.shape, F32)
    key_off = lax.broadcasted_iota(jnp.int32, (bk, G), 0).astype(F32)
    lane = lax.broadcasted_iota(jnp.int32, (1, 2 * bq), 1)
    pos_q = (qi * bq + jnp.where(lane >= bq, lane - bq, lane)).astype(F32)
    ones = jnp.ones((ONES_ROWS, bk), BF16)

    q_rows = nrm_ref.shape[1] * bq // k_ref.shape[1]
    k_big = jnp.max(nrm_ref[0, :, 0, :], axis=0, keepdims=True)
    q_big = jnp.max(nrm_ref[0, pl.ds(qi * q_rows, q_rows), 0, :], axis=0, keepdims=True)
    nlane = lax.broadcasted_iota(jnp.int32, k_big.shape, 1)
    k_max = jnp.max(jnp.where(nlane == NRM_DIFF_K + h, k_big, 0.0))
    q_mine = (nlane == NRM_DIFF_Q + 2 * h) | (nlane == NRM_DIFF_Q + 2 * h + 1)
    q_max = jnp.max(jnp.where(q_mine, q_big, 0.0))
    direct = q_max * k_max <= DIRECT_EXP_MAX_LOGIT ** 2

    def online_tile(j, side):
        start = pl.multiple_of(j * bk, bk)
        k_t = k_ref[0, pl.ds(start, bk), :]
        vt1 = jnp.concatenate([vt_ref[0, 0, j], ones], axis=0)
        j0 = (j * bk).astype(F32)
        groups = [slice(gi * G, (gi + 1) * G) for gi in range(2 * bq // G)]
        scores = [jnp.dot(k_t, q2t_ref[0:LANES, sl], preferred_element_type=F32)
                  for sl in groups]
        for sl, s in zip(groups, scores):
            pq = pos_q[:, sl]
            if side == 0:
                u = s - slope2 * jnp.abs(pq - (j0 + key_off))
                a = jnp.zeros_like(pq)
            elif side < 0:
                u = s + cb_ref[...]
                a = slope2 * (j0 - pq)
            else:
                u = s - cb_ref[...]
                a = slope2 * (pq - j0)
            m_old = m_ref[:, sl]
            m_new = jnp.maximum(m_old, jnp.max(u, axis=0, keepdims=True) + a)
            alpha = jnp.exp2(m_old - m_new)
            p = jnp.exp2(u - (m_new - a)).astype(BF16)
            acc_ref[:, sl] = alpha * acc_ref[:, sl] + jnp.dot(
                vt1, p, preferred_element_type=F32)
            m_ref[:, sl] = m_new

    def direct_tiles(j, n, side):
        groups = [slice(gi * G, (gi + 1) * G) for gi in range(2 * bq // G)]

        def bias_rows(j0):
            def bf16_part(x):
                return x.astype(BF16).astype(F32)

            sgn = -float(side)
            sv = jnp.zeros_like(pos_q) + sgn * slope2
            s_hi = bf16_part(sv)
            s_lo = bf16_part(sv - s_hi)
            a = sv * (j0 - pos_q)
            a1 = bf16_part(a)
            a2 = bf16_part(a - a1)
            a3 = bf16_part(a - a1 - a2)
            pieces = [s_hi, s_lo, s_hi * 256.0, s_lo * 256.0, a1, a2, a3]
            row = lax.broadcasted_iota(jnp.int32, (BIAS_ROWS, 2 * bq), 0)
            w = jnp.zeros((BIAS_ROWS, 2 * bq), F32)
            for i, piece in enumerate(pieces):
                w = jnp.where(row == i, piece, w)
            return w.astype(BF16)

        def operands(tt):
            jj = j + tt
            k_t = k_ref[0, pl.ds(pl.multiple_of(jj * bk, bk), bk), :]
            j0 = (jj * bk).astype(F32)
            if side != 0:
                q2t_ref[LANES + tt * BIAS_ROWS:LANES + (tt + 1) * BIAS_ROWS, :] = bias_rows(j0)
                k_t = jnp.concatenate([k_t, feat_ref[tt]], axis=1)
            return k_t, vt_ref[0, 0, jj], j0

        def score(k_t, sl):
            rows = slice(0, k_t.shape[1])
            return jnp.dot(k_t, q2t_ref[rows, sl], preferred_element_type=F32)

        def pv(vt, j0, sl, s):
            if side == 0:
                s = s - slope2 * jnp.abs(pos_q[:, sl] - (j0 + key_off))
            p = jnp.exp2(s)
            return (jnp.dot(vt, p.astype(BF16), preferred_element_type=F32),
                    jnp.sum(p, axis=0, keepdims=True))

        tiles = [operands(tt) for tt in range(n)]
        scores = [score(tiles[0][0], sl) for sl in groups]
        sums = [None] * len(groups)
        for tt in range(n):
            _, vt, j0 = tiles[tt]
            nxt = []
            for gi, sl in enumerate(groups):
                x, lx = pv(vt, j0, sl, scores[gi])
                sums[gi] = (x, lx) if tt == 0 else (sums[gi][0] + x, sums[gi][1] + lx)
                if tt + 1 < n:
                    nxt.append(score(tiles[tt + 1][0], sl))
            scores = nxt
        for sl, (x, lx) in zip(groups, sums):
            acc_ref[0:LANES, sl] += x
            acc_ref[LANES:LANES + 1, sl] += lx

    def all_tiles(online):
        ndiag = bq // bk
        if online:
            m_ref[...] = jnp.full(m_ref.shape, -jnp.inf, F32)
            cb_ref[...] = slope2 * key_off

            def before(j, carry):
                online_tile(j, -1)
                return carry

            def after(j, carry):
                online_tile(j, 1)
                return carry

            lax.fori_loop(0, qi * ndiag, before, 0)
            for d in range(ndiag):
                online_tile(qi * ndiag + d, 0)
            lax.fori_loop((qi + 1) * ndiag, nkv, after, 0)
            return

        assert max_tiles % 2 == 0 and ndiag % 2 == 0 and nkv % 2 == 0

        def run(first, count, side):
            def body(t, carry):
                direct_tiles(first + t * max_tiles, max_tiles, side)
                return carry

            full = count // max_tiles
            lax.fori_loop(0, full, body, 0)
            for extra in range(2, max_tiles, 2):
                @pl.when(count - full * max_tiles == extra)
                def _():
                    direct_tiles(first + full * max_tiles, extra, side)

        run(0, qi * ndiag, -1)
        for d in range(0, ndiag, 2):
            direct_tiles(qi * ndiag + d, 2, 0)
        run((qi + 1) * ndiag, nkv - (qi + 1) * ndiag, 1)

    @pl.when(direct)
    def _():
        all_tiles(False)

    @pl.when(jnp.logical_not(direct))
    def _():
        all_tiles(True)

    lq = lam_ref[...]
    lam = (jnp.exp(jnp.sum(lq[0:1] * lq[1:2], axis=1, keepdims=True))
           - jnp.exp(jnp.sum(lq[2:3] * lq[3:4], axis=1, keepdims=True)) + lam_init)
    acc = acc_ref[...]
    ot = acc[0:LANES] * (1.0 / acc[LANES:LANES + 1])
    od = (ot[:, 0:bq] - lam * ot[:, bq:2 * bq]).T
    od = od * lax.rsqrt(jnp.mean(od * od, axis=-1, keepdims=True) + EPS)
    od = od * g_ref[pl.ds(h, 1), :] * (1.0 - lam_init)
    o_ref[0] = od.astype(BF16)


def _diff_attn(qt, k, vt, nrm, slopes, lam_qk, g_diff, lam_init, bq):
    B, S, _ = k.shape
    H = N_DIFF_HEADS
    nkv, bk = vt.shape[2], vt.shape[4]
    assert bq % bk == 0
    kern = functools.partial(_diff_attn_kernel, bq=bq, bk=bk, lam_init=lam_init)
    return pl.pallas_call(
        kern,
        grid=(B, H, S // bq),
        in_specs=[
            pl.BlockSpec(memory_space=pltpu.SMEM),
            pl.BlockSpec((4, HEAD_DIM), lambda b, h, i: (0, 0)),
            pl.BlockSpec((H, LANES), lambda b, h, i: (0, 0)),
            pl.BlockSpec((1, nrm.shape[1], 1, LANES), lambda b, h, i: (b, 0, 0, 0)),
            pl.BlockSpec((1, LANES, bq), lambda b, h, i: (b, h, i)),
            pl.BlockSpec((1, S, LANES), lambda b, h, i: (b, 0, h)),
            pl.BlockSpec((1, 1, nkv, LANES, bk), lambda b, h, i: (b, h, 0, 0, 0)),
        ],
        out_specs=pl.BlockSpec((1, bq, LANES), lambda b, h, i: (b, i, h)),
        out_shape=jax.ShapeDtypeStruct((B, S, H * LANES), BF16),
        scratch_shapes=[
            pltpu.VMEM((2 * LANES, 2 * bq), BF16),
            pltpu.VMEM((1, 2 * bq), F32),
            pltpu.VMEM((LANES + ONES_ROWS, 2 * bq), F32),
            pltpu.VMEM((bk, 8 * LANES), F32),
            pltpu.VMEM((DIFF_MAX_TILES, bk, LANES), BF16),
        ],
        compiler_params=_params(("parallel", "parallel", "arbitrary")),
        name="diff_attn",
    )(slopes, lam_qk, g_diff, nrm, qt, k, vt)


def _dil_bias(dil, slope_a, slope_b):
    QB, R = LANES, DIL_RADIUS
    KW = QB + 2 * R
    r_i = lax.broadcasted_iota(jnp.int32, (QB, KW), 0)
    c_i = lax.broadcasted_iota(jnp.int32, (QB, KW), 1)
    rel = c_i - r_i - R
    band = (rel >= -R) & (rel <= R)
    dist = (jnp.abs(rel) * dil).astype(F32)
    return jnp.concatenate([jnp.where(band, -(slope_a * LOG2E) * dist, NEG),
                            jnp.where(band, -(slope_b * LOG2E) * dist, NEG)], axis=0)


def _dil_segment(q_at, k_at, v_at, nres, L, dil, tok0, bias_at, first, last, online,
                 sa_ref, sb_ref, m_ref, o_ref, kpad, vpad_a, vpad_b):
    QB, R = LANES, DIL_RADIUS
    KW = QB + 2 * R
    pitch = L + 2 * R
    per_class = L // QB
    lane_q = lax.broadcasted_iota(jnp.int32, (QB, LANES), 1)
    head_a = lane_q < HEAD_DIM
    c2 = lax.broadcasted_iota(jnp.int32, (2 * QB, KW), 1)

    zpad = jnp.zeros((R, LANES), BF16)
    own_a = lax.broadcasted_iota(jnp.int32, (L, LANES), 1) < HEAD_DIM
    one = jnp.ones((L, LANES), BF16)
    for c in range(nres):
        base = c * pitch
        for pad in (kpad, vpad_a, vpad_b):
            pad[base:base + R, :] = zpad
            pad[base + R + L:base + pitch, :] = zpad
        kpad[base + R:base + R + L, :] = k_at(c)
        v_src = v_at(c)
        vpad_a[base + R:base + R + L, :] = jnp.where(own_a, v_src, one)
        vpad_b[base + R:base + R + L, :] = jnp.where(own_a, one, v_src)

    def rows(c, i0):
        if dil == 1:
            return pl.ds(i0, QB)
        return pl.ds(tok0 + c + dil * i0, QB, stride=dil)

    def scores(c, i0):
        q = q_at(c, i0)
        zero = jnp.zeros_like(q)
        q2 = jnp.concatenate([jnp.where(head_a, q, zero), jnp.where(head_a, zero, q)], axis=0)
        s = lax.dot_general(q2, kpad[pl.ds(c * pitch + i0, KW), :], (((1,), (1,)), ((), ())),
                            preferred_element_type=F32)
        return s + bias_at[...]

    def finish(c, i0, s):
        if online:
            key = i0 - R + c2
            s = jnp.where((key >= 0) & (key < L), s, NEG)
            m_c = jnp.max(s, axis=1, keepdims=True)
            s = s - m_c
        p = jnp.exp2(s).astype(BF16)
        win = pl.ds(c * pitch + i0, KW)
        sa = jnp.dot(p[0:QB], vpad_a[win, :], preferred_element_type=F32)
        sb = jnp.dot(p[QB:2 * QB], vpad_b[win, :], preferred_element_type=F32)
        r = rows(c, i0)
        if online:
            ma = jnp.broadcast_to(m_c[0:QB], (QB, LANES))
            mb = jnp.broadcast_to(m_c[QB:2 * QB], (QB, LANES))
            if not first:
                m_p = m_ref[r, :]
                mpa = jnp.max(jnp.where(head_a, m_p, -jnp.inf), axis=1, keepdims=True)
                mpb = jnp.max(jnp.where(head_a, -jnp.inf, m_p), axis=1, keepdims=True)
                mna, mnb = jnp.maximum(mpa, ma), jnp.maximum(mpb, mb)
                sa = sa_ref[r, :] * jnp.exp2(mpa - mna) + sa * jnp.exp2(ma - mna)
                sb = sb_ref[r, :] * jnp.exp2(mpb - mnb) + sb * jnp.exp2(mb - mnb)
                ma, mb = mna, mnb
            if not last:
                m_ref[r, :] = jnp.where(head_a, ma, mb)
        elif not first:
            sa = sa_ref[r, :] + sa
            sb = sb_ref[r, :] + sb
        if last:
            half = LANES // 2
            o_ref[0, 0, r, :] = jnp.where(head_a, sa / pltpu.roll(sa, half, 1),
                                          sb / pltpu.roll(sb, half, 1))
        else:
            sa_ref[r, :] = sa
            sb_ref[r, :] = sb

    def blocks(todo):
        ss = [scores(c, i0) for c, i0 in todo]
        for (c, i0), s in zip(todo, ss):
            finish(c, i0, s)

    if online:
        for c in range(nres):
            def body(n, carry, c=c):
                blocks([(c, pl.multiple_of(n * QB, QB))])
                return carry
            lax.fori_loop(0, per_class, body, 0)
    elif per_class <= DIL_MAX_BLOCKS:
        todo = [(c, u * QB) for c in range(nres) for u in range(per_class)]
        for at in range(0, len(todo), DIL_MAX_BLOCKS):
            blocks(todo[at:at + DIL_MAX_BLOCKS])
    else:
        assert nres == 1 and per_class % DIL_MAX_BLOCKS == 0

        def body(n, carry):
            blocks([(0, pl.multiple_of((n * DIL_MAX_BLOCKS + u) * QB, QB))
                    for u in range(DIL_MAX_BLOCKS)])
            return carry

        lax.fori_loop(0, per_class // DIL_MAX_BLOCKS, body, 0)


def _dil_kernel(slopes_ref, nrm_ref, pb1_ref, pb4_ref, pb16_ref, o_ref,
                sa_ref, sb_ref, m_ref, kpad, vpad_a, vpad_b, bias_ref, *, group16, group4):
    hp = pl.program_id(1)
    t = pl.program_id(2)
    S = pb1_ref.shape[2]
    n16, n4 = 16 // group16, 4 // group4
    dils = (16, 4, 1)

    @pl.when(t == 0)
    def _():
        for i, dil in enumerate(dils):
            bias_ref[i] = _dil_bias(dil, slopes_ref[2 * hp], slopes_ref[2 * hp + 1])

    big = jnp.max(nrm_ref[0, :, 0, :], axis=0, keepdims=True)
    lane = lax.broadcasted_iota(jnp.int32, big.shape, 1)

    def pick(idx):
        return jnp.max(jnp.where(lane == idx, big, 0.0))

    direct = ((pick(4 * hp) * pick(4 * hp + 1) <= DIRECT_EXP_MAX_LOGIT ** 2)
              & (pick(4 * hp + 2) * pick(4 * hp + 3) <= DIRECT_EXP_MAX_LOGIT ** 2))

    def pattern(online):
        args = (sa_ref, sb_ref, m_ref, o_ref, kpad, vpad_a, vpad_b)

        @pl.when(t < n16)
        def _():
            _dil_segment(lambda c, i0: pb16_ref[0, 0, c, pl.ds(i0, LANES), 0:LANES],
                         lambda c: pb16_ref[0, 0, c, :, LANES:2 * LANES],
                         lambda c: pb16_ref[0, 0, c, :, 2 * LANES:3 * LANES],
                         group16, S // 16, 16, t * group16, bias_ref.at[0],
                         True, False, online, *args)

        @pl.when((t >= n16) & (t < n16 + n4))
        def _():
            _dil_segment(lambda c, i0: pb4_ref[0, 0, c, pl.ds(i0, LANES), 0:LANES],
                         lambda c: pb4_ref[0, 0, c, :, LANES:2 * LANES],
                         lambda c: pb4_ref[0, 0, c, :, 2 * LANES:3 * LANES],
                         group4, S // 4, 4, (t - n16) * group4, bias_ref.at[1],
                         False, False, online, *args)

        @pl.when(t == n16 + n4)
        def _():
            _dil_segment(lambda c, i0: pb1_ref[0, 0, pl.ds(i0, LANES), 0:LANES],
                         lambda c: pb1_ref[0, 0, :, LANES:2 * LANES],
                         lambda c: pb1_ref[0, 0, :, 2 * LANES:3 * LANES],
                         1, S, 1, 0, bias_ref.at[2],
                         False, True, online, *args)

    @pl.when(direct)
    def _():
        pattern(False)

    @pl.when(jnp.logical_not(direct))
    def _():
        pattern(True)


def _dil_attn(pb1, pb4, pb16, nrm, slopes, group16=8, group4=2):
    B, P, S, _ = pb1.shape
    n16, n4 = 16 // group16, 4 // group4
    nt = nrm.shape[1]
    pad_rows = max(group16 * (S // 16 + 2 * DIL_RADIUS), group4 * (S // 4 + 2 * DIL_RADIUS),
                   S + 2 * DIL_RADIUS)
    return pl.pallas_call(
        functools.partial(_dil_kernel, group16=group16, group4=group4),
        grid=(B, P, n16 + n4 + 1),
        in_specs=[
            pl.BlockSpec(memory_space=pltpu.SMEM),
            pl.BlockSpec((1, nt, 1, LANES), lambda b, p, t: (b, 0, 0, 0)),
            pl.BlockSpec((1, 1, S, PAIR_W), lambda b, p, t: (b, p, 0, 0)),
            pl.BlockSpec((1, 1, group4, S // 4, PAIR_W),
                         lambda b, p, t: (b, p, jnp.clip(t - n16, 0, n4 - 1), 0, 0)),
            pl.BlockSpec((1, 1, group16, S // 16, PAIR_W),
                         lambda b, p, t: (b, p, jnp.minimum(t, n16 - 1), 0, 0)),
        ],
        out_specs=pl.BlockSpec((1, 1, S, LANES), lambda b, p, t: (b, p, 0, 0)),
        out_shape=jax.ShapeDtypeStruct((B, P, S, LANES), F32),
        scratch_shapes=[
            pltpu.VMEM((S, LANES), F32),
            pltpu.VMEM((S, LANES), F32),
            pltpu.VMEM((S, LANES), F32),
            pltpu.VMEM((pad_rows, LANES), BF16),
            pltpu.VMEM((pad_rows, LANES), BF16),
            pltpu.VMEM((pad_rows, LANES), BF16),
            pltpu.VMEM((3, 2 * LANES, LANES + 2 * DIL_RADIUS), F32),
        ],
        compiler_params=_params(("parallel", "parallel", "arbitrary")),
        name="dil_attn",
    )(slopes, nrm, pb1, pb4, pb16)


def _out_mlp_kernel(x_ref, oa_ref, ob_ref, gdil_ref, wout_ref, gmlp_ref, wup_ref,
                    wdown_ref, gfin_ref, y_ref, *, ff_chunk):
    ob = jnp.concatenate([ob_ref[0, p] for p in range(N_DIL_PAIRS)], axis=-1)
    ob = _rms(ob, gdil_ref[...])
    mix = jnp.concatenate([oa_ref[0], ob.astype(BF16)], axis=-1)
    x1 = x_ref[0] + jnp.dot(mix, wout_ref[...], preferred_element_type=F32)
    h = _rms(x1, gmlp_ref[...]).astype(BF16)
    y = x1
    for c in range(D_FF // ff_chunk):
        u = jnp.dot(h, wup_ref[:, c * ff_chunk:(c + 1) * ff_chunk], preferred_element_type=F32)
        u = jnp.square(jnp.maximum(u, 0.0)).astype(BF16)
        y = y + jnp.dot(u, wdown_ref[c * ff_chunk:(c + 1) * ff_chunk, :],
                        preferred_element_type=F32)
    y_ref[0] = _rms(y, gfin_ref[...])


def _out_mlp(x, oa, ob, g_dil, w_out, g_mlp, w_up, w_down, g_final, tm, ff_chunk):
    B, S, D = x.shape
    const = lambda shape: pl.BlockSpec(shape, lambda b, i: (0,) * len(shape),
                                       pipeline_mode=pl.Buffered(1))
    return pl.pallas_call(
        functools.partial(_out_mlp_kernel, ff_chunk=ff_chunk),
        grid=(B, S // tm),
        in_specs=[
            pl.BlockSpec((1, tm, D), lambda b, i: (b, i, 0)),
            pl.BlockSpec((1, tm, N_DIFF_HEADS * LANES), lambda b, i: (b, i, 0)),
            pl.BlockSpec((1, N_DIL_PAIRS, tm, LANES), lambda b, i: (b, 0, i, 0)),
            const((1, DIL_W)),
            const((2 * DIL_W, D)),
            const((1, D)),
            const((D, D_FF)),
            const((D_FF, D)),
            const((1, D)),
        ],
        out_specs=pl.BlockSpec((1, tm, D), lambda b, i: (b, i, 0)),
        out_shape=jax.ShapeDtypeStruct((B, S, D), F32),
        compiler_params=_params(("parallel", "parallel")),
        name="out_mlp",
    )(x, oa, ob, g_dil, w_out, g_mlp, w_up, w_down, g_final)


def _alibi_slopes(n):
    return 2.0 ** (-8.0 * jnp.arange(1, n + 1, dtype=F32) / n)


def _layer(x, g_mix, w_in, wq_t, wv_t, lam_qk, g_diff, g_dil, w_out, g_mlp, w_up, w_down,
           g_final, lam_init):
    qt, k, vt, pb1, pb4, pb16, nrm = _in_proj(x, g_mix, w_in, wq_t, wv_t, tm=512)
    oa = _diff_attn(qt, k, vt, nrm, _alibi_slopes(N_DIFF_HEADS), lam_qk, g_diff, lam_init,
                    bq=1024)
    assert DIL_PATTERNS == ((2 * DIL_RADIUS, 1), (8 * DIL_RADIUS, 4), (32 * DIL_RADIUS, 16))
    ob = _dil_attn(pb1, pb4, pb16, nrm, _alibi_slopes(N_DIL_HEADS))
    return _out_mlp(x, oa, ob, g_dil, w_out, g_mlp, w_up, w_down, g_final,
                    tm=512, ff_chunk=1024)


def kernel(x_prompt, x_sample, g_mix, w_in, lam_qk, g_diff, g_dil, w_out, g_mlp, w_up,
           w_down, g_final):
    assert g_mix.shape[0] == 1
    lam_init = 0.8 - 0.6 * math.exp(-0.3 * 0)
    w_in_b = w_in[0].astype(BF16)
    weights = (
        g_mix[0][None, :], w_in_b, w_in_b[:, 0:DIFF_QW].T, w_in_b[:, 2 * DIFF_QW:DIFF_W].T,
        lam_qk[0],
        g_diff[0], g_dil[0][None, :], w_out[0].astype(BF16), g_mlp[0][None, :],
        w_up[0].astype(BF16), w_down[0].astype(BF16), g_final[None, :],
    )
    return tuple(_layer(x, *weights, lam_init) for x in (x_prompt, x_sample))
```

```python
import functools
import math

import jax
import jax.numpy as jnp
from jax import lax
from jax.experimental import pallas as pl
from jax.experimental.pallas import tpu as pltpu

D_MODEL = 1024
HEAD_DIM = 64
N_DIFF_HEADS = 4
N_DIL_HEADS = 8
N_DIL_PAIRS = N_DIL_HEADS // 2
DIL_PATTERNS = ((128, 1), (512, 4), (2048, 16))
DIL_RADIUS = 64
D_FF = 4 * D_MODEL
EPS = 1e-5
DIFF_QW = N_DIFF_HEADS * 2 * HEAD_DIM
DIFF_W = 3 * DIFF_QW
DIL_W = N_DIL_HEADS * HEAD_DIM
IN_W = DIFF_W + 3 * DIL_W
LANES = 128
PAIR_W = 3 * LANES
NEG = -1e30
LOG2E = math.log2(math.e)
ONES_ROWS = 16
DIRECT_EXP_MAX_LOGIT = 60.0
BIAS_ROWS = 16
DIFF_MAX_TILES = 2
DIL_MAX_BLOCKS = 16
NRM_DIFF_K = 4 * N_DIL_PAIRS
NRM_DIFF_Q = NRM_DIFF_K + N_DIFF_HEADS
VMEM_LIMIT = 56 * 1024 * 1024

BF16 = jnp.bfloat16
F32 = jnp.float32


def _rms(x, g):
    return x * lax.rsqrt(jnp.mean(x * x, axis=-1, keepdims=True) + EPS) * g


def _params(sem):
    return pltpu.CompilerParams(dimension_semantics=sem, vmem_limit_bytes=VMEM_LIMIT)


def _in_proj_kernel(x_ref, g_ref, w_ref, wqt_ref, wvt_ref, qt_ref, k_ref, vt_ref,
                    pb1_ref, pb4_ref, pb16_ref, nrm_ref, slab_ref, slab4_ref):
    scale = HEAD_DIM ** -0.5
    nt = (((1,), (1,)), ((), ()))
    tm = x_ref.shape[1]
    lane = lax.broadcasted_iota(jnp.int32, (1, LANES), 1)
    nrm = jnp.zeros((1, LANES), F32)
    h = _rms(x_ref[0], g_ref[...]).astype(BF16)
    for c in range(3):
        p = jnp.dot(h, w_ref[:, DIFF_W + c * DIL_W:DIFF_W + (c + 1) * DIL_W],
                    preferred_element_type=F32)
        if c == 0:
            p = p * (scale * LOG2E)
        cols = slice(c * LANES, (c + 1) * LANES)
        for hp in range(N_DIL_PAIRS):
            ph = p[:, hp * LANES:(hp + 1) * LANES]
            if c < 2:
                sq = ph * ph
                for e in range(2):
                    mine = (lane >= e * HEAD_DIM) & (lane < (e + 1) * HEAD_DIM)
                    big = jnp.max(jnp.sum(jnp.where(mine, sq, 0.0), axis=1, keepdims=True),
                                  axis=0, keepdims=True)
                    nrm = jnp.where(lane == 4 * hp + 2 * e + c, big, nrm)
            pb1_ref[0, hp, :, cols] = ph.astype(BF16)
            s = c * N_DIL_PAIRS + hp
            slab_ref[s] = ph
            for g in range(4):
                v4 = slab_ref[s, pl.ds(g, tm // 4, stride=4), :]
                pb4_ref[0, hp, g, :, cols] = v4.astype(BF16)
                slab4_ref[s, g] = v4
            for g in range(4):
                for c2 in range(4):
                    v16 = slab4_ref[s, g, pl.ds(c2, tm // 16, stride=4), :]
                    pb16_ref[0, hp, 4 * c2 + g, :, cols] = v16.astype(BF16)
    kd = jnp.dot(h, w_ref[:, DIFF_QW:2 * DIFF_QW], preferred_element_type=F32)
    k_ref[0] = kd.astype(BF16)
    for hh in range(N_DIFF_HEADS):
        kh = kd[:, hh * LANES:(hh + 1) * LANES]
        big = jnp.max(jnp.sum(kh * kh, axis=1, keepdims=True), axis=0, keepdims=True)
        nrm = jnp.where(lane == NRM_DIFF_K + hh, big, nrm)
    qt = lax.dot_general(wqt_ref[...], h, nt, preferred_element_type=F32) * (scale * LOG2E)
    qt_ref[0] = qt.astype(BF16)
    for hm in range(2 * N_DIFF_HEADS):
        qm = qt[hm * HEAD_DIM:(hm + 1) * HEAD_DIM, :]
        big = jnp.max(jnp.sum(qm * qm, axis=0, keepdims=True), axis=1, keepdims=True)
        nrm = jnp.where(lane == NRM_DIFF_Q + hm, big, nrm)
    nrm_ref[0, 0] = nrm
    vt = lax.dot_general(wvt_ref[...], h, nt, preferred_element_type=F32).astype(BF16)
    for hh in range(N_DIFF_HEADS):
        vt_ref[0, hh, 0] = vt[hh * LANES:(hh + 1) * LANES, :]


def _in_proj(x, g_mix, w_in, wq_t, wv_t, tm):
    B, S, D = x.shape
    H = N_DIFF_HEADS
    P = N_DIL_PAIRS
    return pl.pallas_call(
        _in_proj_kernel,
        grid=(B, S // tm),
        in_specs=[
            pl.BlockSpec((1, tm, D), lambda b, i: (b, i, 0)),
            pl.BlockSpec((1, D), lambda b, i: (0, 0)),
            pl.BlockSpec((D, IN_W), lambda b, i: (0, 0)),
            pl.BlockSpec((DIFF_QW, D), lambda b, i: (0, 0)),
            pl.BlockSpec((DIFF_QW, D), lambda b, i: (0, 0)),
        ],
        out_specs=[
            pl.BlockSpec((1, DIFF_QW, tm), lambda b, i: (b, 0, i)),
            pl.BlockSpec((1, tm, DIFF_QW), lambda b, i: (b, i, 0)),
            pl.BlockSpec((1, H, 1, LANES, tm), lambda b, i: (b, 0, i, 0, 0)),
            pl.BlockSpec((1, P, tm, PAIR_W), lambda b, i: (b, 0, i, 0)),
            pl.BlockSpec((1, P, 4, tm // 4, PAIR_W), lambda b, i: (b, 0, 0, i, 0)),
            pl.BlockSpec((1, P, 16, tm // 16, PAIR_W), lambda b, i: (b, 0, 0, i, 0)),
            pl.BlockSpec((1, 1, 1, LANES), lambda b, i: (b, i, 0, 0)),
        ],
        out_shape=[
            jax.ShapeDtypeStruct((B, DIFF_QW, S), BF16),
            jax.ShapeDtypeStruct((B, S, DIFF_QW), BF16),
            jax.ShapeDtypeStruct((B, H, S // tm, LANES, tm), BF16),
            jax.ShapeDtypeStruct((B, P, S, PAIR_W), BF16),
            jax.ShapeDtypeStruct((B, P, 4, S // 4, PAIR_W), BF16),
            jax.ShapeDtypeStruct((B, P, 16, S // 16, PAIR_W), BF16),
            jax.ShapeDtypeStruct((B, S // tm, 1, LANES), F32),
        ],
        scratch_shapes=[
            pltpu.VMEM((3 * P, tm, LANES), F32),
            pltpu.VMEM((3 * P, 4, tm // 4, LANES), F32),
        ],
        compiler_params=_params(("parallel", "parallel")),
        name="in_proj",
    )(x, g_mix, w_in, wq_t, wv_t)


def _diff_attn_kernel(slopes_ref, lam_ref, g_ref, nrm_ref, qt_ref, k_ref, vt_ref, o_ref,
                      q2t_ref, m_ref, acc_ref, cb_ref, feat_ref, *, bq, bk, lam_init):
    G = cb_ref.shape[1]
    h = pl.program_id(1)
    qi = pl.program_id(2)
    nkv = k_ref.shape[1] // bk
    slope2 = slopes_ref[h] * LOG2E

    qt = qt_ref[0]
    sub = lax.broadcasted_iota(jnp.int32, qt.shape, 0)
    zero = jnp.zeros_like(qt)
    q2t_ref[0:LANES, 0:bq] = jnp.where(sub < HEAD_DIM, qt, zero)
    q2t_ref[0:LANES, bq:2 * bq] = jnp.where(sub >= HEAD_DIM, qt, zero)
    max_tiles = feat_ref.shape[0]
    assert max_tiles * BIAS_ROWS <= LANES

    @pl.when(qi == 0)
    def _():
        q2t_ref[LANES:2 * LANES, :] = jnp.zeros((LANES, 2 * bq), BF16)
        f_row = lax.broadcasted_iota(jnp.int32, (bk, LANES), 0)
        f_lane = lax.broadcasted_iota(jnp.int32, (bk, LANES), 1)
        f_slot = f_lane % BIAS_ROWS
        feat = jnp.where(f_slot < 2, f_row % 256,
                         jnp.where(f_slot < 4, f_row // 256,
                                   jnp.where(f_slot < 7, 1, 0)))
        for tt in range(max_tiles):
            mine = f_lane // BIAS_ROWS == tt
            feat_ref[tt] = jnp.where(mine, feat, 0).astype(F32).astype(BF16)

    acc_ref[...] = jnp.zeros(acc_ref.shape, F32)
    key_off = lax.broadcasted_iota(jnp.int32, (bk, G), 0).astype(F32)
    lane = lax.broadcasted_iota(jnp.int32, (1, 2 * bq), 1)
    pos_q = (qi * bq + jnp.where(lane >= bq, lane - bq, lane)).astype(F32)
    ones = jnp.ones((ONES_ROWS, bk), BF16)

    q_rows = nrm_ref.shape[1] * bq // k_ref.shape[1]
    k_big = jnp.max(nrm_ref[0, :, 0, :], axis=0, keepdims=True)
    q_big = jnp.max(nrm_ref[0, pl.ds(qi * q_rows, q_rows), 0, :], axis=0, keepdims=True)
    nlane = lax.broadcasted_iota(jnp.int32, k_big.shape, 1)
    k_max = jnp.max(jnp.where(nlane == NRM_DIFF_K + h, k_big, 0.0))
    q_mine = (nlane == NRM_DIFF_Q + 2 * h) | (nlane == NRM_DIFF_Q + 2 * h + 1)
    q_max = jnp.max(jnp.where(q_mine, q_big, 0.0))
    direct = q_max * k_max <= DIRECT_EXP_MAX_LOGIT ** 2

    def online_tile(j, side):
        start = pl.multiple_of(j * bk, bk)
        k_t = k_ref[0, pl.ds(start, bk), :]
        vt1 = jnp.concatenate([vt_ref[0, 0, j], ones], axis=0)
        j0 = (j * bk).astype(F32)
        groups = [slice(gi * G, (gi + 1) * G) for gi in range(2 * bq // G)]
        scores = [jnp.dot(k_t, q2t_ref[0:LANES, sl], preferred_element_type=F32)
                  for sl in groups]
        for sl, s in zip(groups, scores):
            pq = pos_q[:, sl]
            if side == 0:
                u = s - slope2 * jnp.abs(pq - (j0 + key_off))
                a = jnp.zeros_like(pq)
            elif side < 0:
                u = s + cb_ref[...]
                a = slope2 * (j0 - pq)
            else:
                u = s - cb_ref[...]
                a = slope2 * (pq - j0)
            m_old = m_ref[:, sl]
            m_new = jnp.maximum(m_old, jnp.max(u, axis=0, keepdims=True) + a)
            alpha = jnp.exp2(m_old - m_new)
            p = jnp.exp2(u - (m_new - a)).astype(BF16)
            acc_ref[:, sl] = alpha * acc_ref[:, sl] + jnp.dot(
                vt1, p, preferred_element_type=F32)
            m_ref[:, sl] = m_new

    def direct_tiles(j, n, side):
        groups = [slice(gi * G, (gi + 1) * G) for gi in range(2 * bq // G)]

        def bias_rows(j0):
            def bf16_part(x):
                return x.astype(BF16).astype(F32)

            sgn = -float(side)
            sv = jnp.zeros_like(pos_q) + sgn * slope2
            s_hi = bf16_part(sv)
            s_lo = bf16_part(sv - s_hi)
            a = sv * (j0 - pos_q)
            a1 = bf16_part(a)
            a2 = bf16_part(a - a1)
            a3 = bf16_part(a - a1 - a2)
            pieces = [s_hi, s_lo, s_hi * 256.0, s_lo * 256.0, a1, a2, a3]
            row = lax.broadcasted_iota(jnp.int32, (BIAS_ROWS, 2 * bq), 0)
            w = jnp.zeros((BIAS_ROWS, 2 * bq), F32)
            for i, piece in enumerate(pieces):
                w = jnp.where(row == i, piece, w)
            return w.astype(BF16)

        def operands(tt):
            jj = j + tt
            k_t = k_ref[0, pl.ds(pl.multiple_of(jj * bk, bk), bk), :]
            j0 = (jj * bk).astype(F32)
            if side != 0:
                q2t_ref[LANES + tt * BIAS_ROWS:LANES + (tt + 1) * BIAS_ROWS, :] = bias_rows(j0)
                k_t = jnp.concatenate([k_t, feat_ref[tt]], axis=1)
            return k_t, vt_ref[0, 0, jj], j0

        def score(k_t, sl):
            rows = slice(0, k_t.shape[1])
            return jnp.dot(k_t, q2t_ref[rows, sl], preferred_element_type=F32)

        def pv(vt, j0, sl, s):
            if side == 0:
                s = s - slope2 * jnp.abs(pos_q[:, sl] - (j0 + key_off))
            p = jnp.exp2(s)
            return (jnp.dot(vt, p.astype(BF16), preferred_element_type=F32),
                    jnp.sum(p, axis=0, keepdims=True))

        tiles = [operands(tt) for tt in range(n)]
        scores = [score(tiles[0][0], sl) for sl in groups]
        sums = [None] * len(groups)
        for tt in range(n):
            _, vt, j0 = tiles[tt]
            nxt = []
            for gi, sl in enumerate(groups):
                x, lx = pv(vt, j0, sl, scores[gi])
                sums[gi] = (x, lx) if tt == 0 else (sums[gi][0] + x, sums[gi][1] + lx)
                if tt + 1 < n:
                    nxt.append(score(tiles[tt + 1][0], sl))
            scores = nxt
        for sl, (x, lx) in zip(groups, sums):
            acc_ref[0:LANES, sl] += x
            acc_ref[LANES:LANES + 1, sl] += lx

    def all_tiles(online):
        ndiag = bq // bk
        if online:
            m_ref[...] = jnp.full(m_ref.shape, -jnp.inf, F32)
            cb_ref[...] = slope2 * key_off

            def before(j, carry):
                online_tile(j, -1)
                return carry

            def after(j, carry):
                online_tile(j, 1)
                return carry

            lax.fori_loop(0, qi * ndiag, before, 0)
            for d in range(ndiag):
                online_tile(qi * ndiag + d, 0)
            lax.fori_loop((qi + 1) * ndiag, nkv, after, 0)
            return

        assert max_tiles % 2 == 0 and ndiag % 2 == 0 and nkv % 2 == 0

        def run(first, count, side):
            def body(t, carry):
                direct_tiles(first + t * max_tiles, max_tiles, side)
                return carry

            full = count // max_tiles
            lax.fori_loop(0, full, body, 0)
            for extra in range(2, max_tiles, 2):
                @pl.when(count - full * max_tiles == extra)
                def _():
                    direct_tiles(first + full * max_tiles, extra, side)

        run(0, qi * ndiag, -1)
        for d in range(0, ndiag, 2):
            direct_tiles(qi * ndiag + d, 2, 0)
        run((qi + 1) * ndiag, nkv - (qi + 1) * ndiag, 1)

    @pl.when(direct)
    def _():
        all_tiles(False)

    @pl.when(jnp.logical_not(direct))
    def _():
        all_tiles(True)

    lq = lam_ref[...]
    lam = (jnp.exp(jnp.sum(lq[0:1] * lq[1:2], axis=1, keepdims=True))
           - jnp.exp(jnp.sum(lq[2:3] * lq[3:4], axis=1, keepdims=True)) + lam_init)
    acc = acc_ref[...]
    ot = acc[0:LANES] * (1.0 / acc[LANES:LANES + 1])
    od = (ot[:, 0:bq] - lam * ot[:, bq:2 * bq]).T
    od = od * lax.rsqrt(jnp.mean(od * od, axis=-1, keepdims=True) + EPS)
    od = od * g_ref[pl.ds(h, 1), :] * (1.0 - lam_init)
    o_ref[0] = od.astype(BF16)


def _diff_attn(qt, k, vt, nrm, slopes, lam_qk, g_diff, lam_init, bq):
    B, S, _ = k.shape
    H = N_DIFF_HEADS
    nkv, bk = vt.shape[2], vt.shape[4]
    assert bq % bk == 0
    kern = functools.partial(_diff_attn_kernel, bq=bq, bk=bk, lam_init=lam_init)
    return pl.pallas_call(
        kern,
        grid=(B, H, S // bq),
        in_specs=[
            pl.BlockSpec(memory_space=pltpu.SMEM),
            pl.BlockSpec((4, HEAD_DIM), lambda b, h, i: (0, 0)),
            pl.BlockSpec((H, LANES), lambda b, h, i: (0, 0)),
            pl.BlockSpec((1, nrm.shape[1], 1, LANES), lambda b, h, i: (b, 0, 0, 0)),
            pl.BlockSpec((1, LANES, bq), lambda b, h, i: (b, h, i)),
            pl.BlockSpec((1, S, LANES), lambda b, h, i: (b, 0, h)),
            pl.BlockSpec((1, 1, nkv, LANES, bk), lambda b, h, i: (b, h, 0, 0, 0)),
        ],
        out_specs=pl.BlockSpec((1, bq, LANES), lambda b, h, i: (b, i, h)),
        out_shape=jax.ShapeDtypeStruct((B, S, H * LANES), BF16),
        scratch_shapes=[
            pltpu.VMEM((2 * LANES, 2 * bq), BF16),
            pltpu.VMEM((1, 2 * bq), F32),
            pltpu.VMEM((LANES + ONES_ROWS, 2 * bq), F32),
            pltpu.VMEM((bk, 8 * LANES), F32),
            pltpu.VMEM((DIFF_MAX_TILES, bk, LANES), BF16),
        ],
        compiler_params=_params(("parallel", "parallel", "arbitrary")),
        name="diff_attn",
    )(slopes, lam_qk, g_diff, nrm, qt, k, vt)


def _dil_bias(dil, slope_a, slope_b):
    QB, R = LANES, DIL_RADIUS
    KW = QB + 2 * R
    r_i = lax.broadcasted_iota(jnp.int32, (QB, KW), 0)
    c_i = lax.broadcasted_iota(jnp.int32, (QB, KW), 1)
    rel = c_i - r_i - R
    band = (rel >= -R) & (rel <= R)
    dist = (jnp.abs(rel) * dil).astype(F32)
    return jnp.concatenate([jnp.where(band, -(slope_a * LOG2E) * dist, NEG),
                            jnp.where(band, -(slope_b * LOG2E) * dist, NEG)], axis=0)


def _dil_segment(q_at, k_at, v_at, nres, L, dil, tok0, bias_at, first, last, online,
                 sa_ref, sb_ref, m_ref, o_ref, kpad, vpad_a, vpad_b):
    QB, R = LANES, DIL_RADIUS
    KW = QB + 2 * R
    pitch = L + 2 * R
    per_class = L // QB
    lane_q = lax.broadcasted_iota(jnp.int32, (QB, LANES), 1)
    head_a = lane_q < HEAD_DIM
    c2 = lax.broadcasted_iota(jnp.int32, (2 * QB, KW), 1)

    zpad = jnp.zeros((R, LANES), BF16)
    own_a = lax.broadcasted_iota(jnp.int32, (L, LANES), 1) < HEAD_DIM
    one = jnp.ones((L, LANES), BF16)
    for c in range(nres):
        base = c * pitch
        for pad in (kpad, vpad_a, vpad_b):
            pad[base:base + R, :] = zpad
            pad[base + R + L:base + pitch, :] = zpad
        kpad[base + R:base + R + L, :] = k_at(c)
        v_src = v_at(c)
        vpad_a[base + R:base + R + L, :] = jnp.where(own_a, v_src, one)
        vpad_b[base + R:base + R + L, :] = jnp.where(own_a, one, v_src)

    def rows(c, i0):
        if dil == 1:
            return pl.ds(i0, QB)
        return pl.ds(tok0 + c + dil * i0, QB, stride=dil)

    def scores(c, i0):
        q = q_at(c, i0)
        zero = jnp.zeros_like(q)
        q2 = jnp.concatenate([jnp.where(head_a, q, zero), jnp.where(head_a, zero, q)], axis=0)
        s = lax.dot_general(q2, kpad[pl.ds(c * pitch + i0, KW), :], (((1,), (1,)), ((), ())),
                            preferred_element_type=F32)
        return s + bias_at[...]

    def finish(c, i0, s):
        if online:
            key = i0 - R + c2
            s = jnp.where((key >= 0) & (key < L), s, NEG)
            m_c = jnp.max(s, axis=1, keepdims=True)
            s = s - m_c
        p = jnp.exp2(s).astype(BF16)
        win = pl.ds(c * pitch + i0, KW)
        sa = jnp.dot(p[0:QB], vpad_a[win, :], preferred_element_type=F32)
        sb = jnp.dot(p[QB:2 * QB], vpad_b[win, :], preferred_element_type=F32)
        r = rows(c, i0)
        if online:
            ma = jnp.broadcast_to(m_c[0:QB], (QB, LANES))
            mb = jnp.broadcast_to(m_c[QB:2 * QB], (QB, LANES))
            if not first:
                m_p = m_ref[r, :]
                mpa = jnp.max(jnp.where(head_a, m_p, -jnp.inf), axis=1, keepdims=True)
                mpb = jnp.max(jnp.where(head_a, -jnp.inf, m_p), axis=1, keepdims=True)
                mna, mnb = jnp.maximum(mpa, ma), jnp.maximum(mpb, mb)
                sa = sa_ref[r, :] * jnp.exp2(mpa - mna) + sa * jnp.exp2(ma - mna)
                sb = sb_ref[r, :] * jnp.exp2(mpb - mnb) + sb * jnp.exp2(mb - mnb)
                ma, mb = mna, mnb
            if not last:
                m_ref[r, :] = jnp.where(head_a, ma, mb)
        elif not first:
            sa = sa_ref[r, :] + sa
            sb = sb_ref[r, :] + sb
        if last:
            half = LANES // 2
            o_ref[0, 0, r, :] = jnp.where(head_a, sa / pltpu.roll(sa, half, 1),
                                          sb / pltpu.roll(sb, half, 1))
        else:
            sa_ref[r, :] = sa
            sb_ref[r, :] = sb

    def blocks(todo):
        ss = [scores(c, i0) for c, i0 in todo]
        for (c, i0), s in zip(todo, ss):
            finish(c, i0, s)

    if online:
        for c in range(nres):
            def body(n, carry, c=c):
                blocks([(c, pl.multiple_of(n * QB, QB))])
                return carry
            lax.fori_loop(0, per_class, body, 0)
    elif per_class <= DIL_MAX_BLOCKS:
        todo = [(c, u * QB) for c in range(nres) for u in range(per_class)]
        for at in range(0, len(todo), DIL_MAX_BLOCKS):
            blocks(todo[at:at + DIL_MAX_BLOCKS])
    else:
        assert nres == 1 and per_class % DIL_MAX_BLOCKS == 0

        def body(n, carry):
            blocks([(0, pl.multiple_of((n * DIL_MAX_BLOCKS + u) * QB, QB))
                    for u in range(DIL_MAX_BLOCKS)])
            return carry

        lax.fori_loop(0, per_class // DIL_MAX_BLOCKS, body, 0)


def _dil_kernel(slopes_ref, nrm_ref, pb1_ref, pb4_ref, pb16_ref, o_ref,
                sa_ref, sb_ref, m_ref, kpad, vpad_a, vpad_b, bias_ref, *, group16, group4):
    hp = pl.program_id(1)
    t = pl.program_id(2)
    S = pb1_ref.shape[2]
    n16, n4 = 16 // group16, 4 // group4
    dils = (16, 4, 1)

    @pl.when(t == 0)
    def _():
        for i, dil in enumerate(dils):
            bias_ref[i] = _dil_bias(dil, slopes_ref[2 * hp], slopes_ref[2 * hp + 1])

    big = jnp.max(nrm_ref[0, :, 0, :], axis=0, keepdims=True)
    lane = lax.broadcasted_iota(jnp.int32, big.shape, 1)

    def pick(idx):
        return jnp.max(jnp.where(lane == idx, big, 0.0))

    direct = ((pick(4 * hp) * pick(4 * hp + 1) <= DIRECT_EXP_MAX_LOGIT ** 2)
              & (pick(4 * hp + 2) * pick(4 * hp + 3) <= DIRECT_EXP_MAX_LOGIT ** 2))

    def pattern(online):
        args = (sa_ref, sb_ref, m_ref, o_ref, kpad, vpad_a, vpad_b)

        @pl.when(t < n16)
        def _():
            _dil_segment(lambda c, i0: pb16_ref[0, 0, c, pl.ds(i0, LANES), 0:LANES],
                         lambda c: pb16_ref[0, 0, c, :, LANES:2 * LANES],
                         lambda c: pb16_ref[0, 0, c, :, 2 * LANES:3 * LANES],
                         group16, S // 16, 16, t * group16, bias_ref.at[0],
                         True, False, online, *args)

        @pl.when((t >= n16) & (t < n16 + n4))
        def _():
            _dil_segment(lambda c, i0: pb4_ref[0, 0, c, pl.ds(i0, LANES), 0:LANES],
                         lambda c: pb4_ref[0, 0, c, :, LANES:2 * LANES],
                         lambda c: pb4_ref[0, 0, c, :, 2 * LANES:3 * LANES],
                         group4, S // 4, 4, (t - n16) * group4, bias_ref.at[1],
                         False, False, online, *args)

        @pl.when(t == n16 + n4)
        def _():
            _dil_segment(lambda c, i0: pb1_ref[0, 0, pl.ds(i0, LANES), 0:LANES],
                         lambda c: pb1_ref[0, 0, :, LANES:2 * LANES],
                         lambda c: pb1_ref[0, 0, :, 2 * LANES:3 * LANES],
                         1, S, 1, 0, bias_ref.at[2],
                         False, True, online, *args)

    @pl.when(direct)
    def _():
        pattern(False)

    @pl.when(jnp.logical_not(direct))
    def _():
        pattern(True)


def _dil_attn(pb1, pb4, pb16, nrm, slopes, group16=8, group4=2):
    B, P, S, _ = pb1.shape
    n16, n4 = 16 // group16, 4 // group4
    nt = nrm.shape[1]
    pad_rows = max(group16 * (S // 16 + 2 * DIL_RADIUS), group4 * (S // 4 + 2 * DIL_RADIUS),
                   S + 2 * DIL_RADIUS)
    return pl.pallas_call(
        functools.partial(_dil_kernel, group16=group16, group4=group4),
        grid=(B, P, n16 + n4 + 1),
        in_specs=[
            pl.BlockSpec(memory_space=pltpu.SMEM),
            pl.BlockSpec((1, nt, 1, LANES), lambda b, p, t: (b, 0, 0, 0)),
            pl.BlockSpec((1, 1, S, PAIR_W), lambda b, p, t: (b, p, 0, 0)),
            pl.BlockSpec((1, 1, group4, S // 4, PAIR_W),
                         lambda b, p, t: (b, p, jnp.clip(t - n16, 0, n4 - 1), 0, 0)),
            pl.BlockSpec((1, 1, group16, S // 16, PAIR_W),
                         lambda b, p, t: (b, p, jnp.minimum(t, n16 - 1), 0, 0)),
        ],
        out_specs=pl.BlockSpec((1, 1, S, LANES), lambda b, p, t: (b, p, 0, 0)),
        out_shape=jax.ShapeDtypeStruct((B, P, S, LANES), F32),
        scratch_shapes=[
            pltpu.VMEM((S, LANES), F32),
            pltpu.VMEM((S, LANES), F32),
            pltpu.VMEM((S, LANES), F32),
            pltpu.VMEM((pad_rows, LANES), BF16),
            pltpu.VMEM((pad_rows, LANES), BF16),
            pltpu.VMEM((pad_rows, LANES), BF16),
            pltpu.VMEM((3, 2 * LANES, LANES + 2 * DIL_RADIUS), F32),
        ],
        compiler_params=_params(("parallel", "parallel", "arbitrary")),
        name="dil_attn",
    )(slopes, nrm, pb1, pb4, pb16)


def _out_mlp_kernel(x_ref, oa_ref, ob_ref, gdil_ref, wout_ref, gmlp_ref, wup_ref,
                    wdown_ref, gfin_ref, y_ref, *, ff_chunk):
    ob = jnp.concatenate([ob_ref[0, p] for p in range(N_DIL_PAIRS)], axis=-1)
    ob = _rms(ob, gdil_ref[...])
    mix = jnp.concatenate([oa_ref[0], ob.astype(BF16)], axis=-1)
    x1 = x_ref[0] + jnp.dot(mix, wout_ref[...], preferred_element_type=F32)
    h = _rms(x1, gmlp_ref[...]).astype(BF16)
    y = x1
    for c in range(D_FF // ff_chunk):
        u = jnp.dot(h, wup_ref[:, c * ff_chunk:(c + 1) * ff_chunk], preferred_element_type=F32)
        u = jnp.square(jnp.maximum(u, 0.0)).astype(BF16)
        y = y + jnp.dot(u, wdown_ref[c * ff_chunk:(c + 1) * ff_chunk, :],
                        preferred_element_type=F32)
    y_ref[0] = _rms(y, gfin_ref[...])


def _out_mlp(x, oa, ob, g_dil, w_out, g_mlp, w_up, w_down, g_final, tm, ff_chunk):
    B, S, D = x.shape
    const = lambda shape: pl.BlockSpec(shape, lambda b, i: (0,) * len(shape),
                                       pipeline_mode=pl.Buffered(1))
    return pl.pallas_call(
        functools.partial(_out_mlp_kernel, ff_chunk=ff_chunk),
        grid=(B, S // tm),
        in_specs=[
            pl.BlockSpec((1, tm, D), lambda b, i: (b, i, 0)),
            pl.BlockSpec((1, tm, N_DIFF_HEADS * LANES), lambda b, i: (b, i, 0)),
            pl.BlockSpec((1, N_DIL_PAIRS, tm, LANES), lambda b, i: (b, 0, i, 0)),
            const((1, DIL_W)),
            const((2 * DIL_W, D)),
            const((1, D)),
            const((D, D_FF)),
            const((D_FF, D)),
            const((1, D)),
        ],
        out_specs=pl.BlockSpec((1, tm, D), lambda b, i: (b, i, 0)),
        out_shape=jax.ShapeDtypeStruct((B, S, D), F32),
        compiler_params=_params(("parallel", "parallel")),
        name="out_mlp",
    )(x, oa, ob, g_dil, w_out, g_mlp, w_up, w_down, g_final)


def _alibi_slopes(n):
    return 2.0 ** (-8.0 * jnp.arange(1, n + 1, dtype=F32) / n)


def _layer(x, g_mix, w_in, wq_t, wv_t, lam_qk, g_diff, g_dil, w_out, g_mlp, w_up, w_down,
           g_final, lam_init):
    qt, k, vt, pb1, pb4, pb16, nrm = _in_proj(x, g_mix, w_in, wq_t, wv_t, tm=512)
    oa = _diff_attn(qt, k, vt, nrm, _alibi_slopes(N_DIFF_HEADS), lam_qk, g_diff, lam_init,
                    bq=1024)
    assert DIL_PATTERNS == ((2 * DIL_RADIUS, 1), (8 * DIL_RADIUS, 4), (32 * DIL_RADIUS, 16))
    ob = _dil_attn(pb1, pb4, pb16, nrm, _alibi_slopes(N_DIL_HEADS))
    return _out_mlp(x, oa, ob, g_dil, w_out, g_mlp, w_up, w_down, g_final,
                    tm=1024, ff_chunk=1024)


def kernel(x_prompt, x_sample, g_mix, w_in, lam_qk, g_diff, g_dil, w_out, g_mlp, w_up,
           w_down, g_final):
    assert g_mix.shape[0] == 1
    lam_init = 0.8 - 0.6 * math.exp(-0.3 * 0)
    w_in_b = w_in[0].astype(BF16)
    weights = (
        g_mix[0][None, :], w_in_b, w_in_b[:, 0:DIFF_QW].T, w_in_b[:, 2 * DIFF_QW:DIFF_W].T,
        lam_qk[0],
        g_diff[0], g_dil[0][None, :], w_out[0].astype(BF16), g_mlp[0][None, :],
        w_up[0].astype(BF16), w_down[0].astype(BF16), g_final[None, :],
    )
    return tuple(_layer(x, *weights, lam_init) for x in (x_prompt, x_sample))
```

```python
import functools
import math

import jax
import jax.numpy as jnp
from jax import lax
from jax.experimental import pallas as pl
from jax.experimental.pallas import tpu as pltpu

D_MODEL = 1024
HEAD_DIM = 64
N_DIFF_HEADS = 4
N_DIL_HEADS = 8
N_DIL_PAIRS = N_DIL_HEADS // 2
DIL_PATTERNS = ((128, 1), (512, 4), (2048, 16))
DIL_RADIUS = 64
D_FF = 4 * D_MODEL
EPS = 1e-5
DIFF_QW = N_DIFF_HEADS * 2 * HEAD_DIM
DIFF_W = 3 * DIFF_QW
DIL_W = N_DIL_HEADS * HEAD_DIM
IN_W = DIFF_W + 3 * DIL_W
LANES = 128
PAIR_W = 3 * LANES
NEG = -1e30
LOG2E = math.log2(math.e)
ONES_ROWS = 16
DIRECT_EXP_MAX_LOGIT = 60.0
BIAS_ROWS = 16
DIFF_MAX_TILES = 2
DIFF_GROUP_LANES = 8 * LANES
DIL_MAX_BLOCKS = 16
NRM_DIFF_K = 4 * N_DIL_PAIRS
NRM_DIFF_Q = NRM_DIFF_K + N_DIFF_HEADS
VMEM_LIMIT = 56 * 1024 * 1024

BF16 = jnp.bfloat16
F32 = jnp.float32


def _rms(x, g):
    return x * lax.rsqrt(jnp.mean(x * x, axis=-1, keepdims=True) + EPS) * g


def _params(sem):
    return pltpu.CompilerParams(dimension_semantics=sem, vmem_limit_bytes=VMEM_LIMIT)


def _in_proj_kernel(x_ref, g_ref, w_ref, wqt_ref, wvt_ref, qt_ref, k_ref, vt_ref,
                    pb1_ref, pb4_ref, pb16_ref, nrm_ref, slab_ref, slab4_ref):
    scale = HEAD_DIM ** -0.5
    nt = (((1,), (1,)), ((), ()))
    tm = x_ref.shape[1]
    lane = lax.broadcasted_iota(jnp.int32, (1, LANES), 1)
    nrm = jnp.zeros((1, LANES), F32)
    h = _rms(x_ref[0], g_ref[...]).astype(BF16)
    for c in range(3):
        p = jnp.dot(h, w_ref[:, DIFF_W + c * DIL_W:DIFF_W + (c + 1) * DIL_W],
                    preferred_element_type=F32)
        if c == 0:
            p = p * (scale * LOG2E)
        cols = slice(c * LANES, (c + 1) * LANES)
        for hp in range(N_DIL_PAIRS):
            ph = p[:, hp * LANES:(hp + 1) * LANES]
            if c < 2:
                sq = ph * ph
                for e in range(2):
                    mine = (lane >= e * HEAD_DIM) & (lane < (e + 1) * HEAD_DIM)
                    big = jnp.max(jnp.sum(jnp.where(mine, sq, 0.0), axis=1, keepdims=True),
                                  axis=0, keepdims=True)
                    nrm = jnp.where(lane == 4 * hp + 2 * e + c, big, nrm)
            pb1_ref[0, hp, :, cols] = ph.astype(BF16)
            s = c * N_DIL_PAIRS + hp
            slab_ref[s] = ph
            for g in range(4):
                v4 = slab_ref[s, pl.ds(g, tm // 4, stride=4), :]
                pb4_ref[0, hp, g, :, cols] = v4.astype(BF16)
                slab4_ref[s, g] = v4
            for g in range(4):
                for c2 in range(4):
                    v16 = slab4_ref[s, g, pl.ds(c2, tm // 16, stride=4), :]
                    pb16_ref[0, hp, 4 * c2 + g, :, cols] = v16.astype(BF16)
    kd = jnp.dot(h, w_ref[:, DIFF_QW:2 * DIFF_QW], preferred_element_type=F32)
    k_ref[0] = kd.astype(BF16)
    for hh in range(N_DIFF_HEADS):
        kh = kd[:, hh * LANES:(hh + 1) * LANES]
        big = jnp.max(jnp.sum(kh * kh, axis=1, keepdims=True), axis=0, keepdims=True)
        nrm = jnp.where(lane == NRM_DIFF_K + hh, big, nrm)
    qt = lax.dot_general(wqt_ref[...], h, nt, preferred_element_type=F32) * (scale * LOG2E)
    qt_ref[0] = qt.astype(BF16)
    for hm in range(2 * N_DIFF_HEADS):
        qm = qt[hm * HEAD_DIM:(hm + 1) * HEAD_DIM, :]
        big = jnp.max(jnp.sum(qm * qm, axis=0, keepdims=True), axis=1, keepdims=True)
        nrm = jnp.where(lane == NRM_DIFF_Q + hm, big, nrm)
    nrm_ref[0, 0] = nrm
    vt = lax.dot_general(wvt_ref[...], h, nt, preferred_element_type=F32).astype(BF16)
    for hh in range(N_DIFF_HEADS):
        vt_ref[0, hh, 0] = vt[hh * LANES:(hh + 1) * LANES, :]


def _in_proj(x, g_mix, w_in, wq_t, wv_t, tm):
    B, S, D = x.shape
    H = N_DIFF_HEADS
    P = N_DIL_PAIRS
    return pl.pallas_call(
        _in_proj_kernel,
        grid=(B, S // tm),
        in_specs=[
            pl.BlockSpec((1, tm, D), lambda b, i: (b, i, 0)),
            pl.BlockSpec((1, D), lambda b, i: (0, 0)),
            pl.BlockSpec((D, IN_W), lambda b, i: (0, 0)),
            pl.BlockSpec((DIFF_QW, D), lambda b, i: (0, 0)),
            pl.BlockSpec((DIFF_QW, D), lambda b, i: (0, 0)),
        ],
        out_specs=[
            pl.BlockSpec((1, DIFF_QW, tm), lambda b, i: (b, 0, i)),
            pl.BlockSpec((1, tm, DIFF_QW), lambda b, i: (b, i, 0)),
            pl.BlockSpec((1, H, 1, LANES, tm), lambda b, i: (b, 0, i, 0, 0)),
            pl.BlockSpec((1, P, tm, PAIR_W), lambda b, i: (b, 0, i, 0)),
            pl.BlockSpec((1, P, 4, tm // 4, PAIR_W), lambda b, i: (b, 0, 0, i, 0)),
            pl.BlockSpec((1, P, 16, tm // 16, PAIR_W), lambda b, i: (b, 0, 0, i, 0)),
            pl.BlockSpec((1, 1, 1, LANES), lambda b, i: (b, i, 0, 0)),
        ],
        out_shape=[
            jax.ShapeDtypeStruct((B, DIFF_QW, S), BF16),
            jax.ShapeDtypeStruct((B, S, DIFF_QW), BF16),
            jax.ShapeDtypeStruct((B, H, S // tm, LANES, tm), BF16),
            jax.ShapeDtypeStruct((B, P, S, PAIR_W), BF16),
            jax.ShapeDtypeStruct((B, P, 4, S // 4, PAIR_W), BF16),
            jax.ShapeDtypeStruct((B, P, 16, S // 16, PAIR_W), BF16),
            jax.ShapeDtypeStruct((B, S // tm, 1, LANES), F32),
        ],
        scratch_shapes=[
            pltpu.VMEM((3 * P, tm, LANES), F32),
            pltpu.VMEM((3 * P, 4, tm // 4, LANES), F32),
        ],
        compiler_params=_params(("parallel", "parallel")),
        name="in_proj",
    )(x, g_mix, w_in, wq_t, wv_t)


def _diff_attn_kernel(slopes_ref, lam_ref, g_ref, nrm_ref, qt_ref, k_ref, vt_ref, o_ref,
                      q2t_ref, m_ref, acc_ref, feat_ref, *, bq, bk, lam_init):
    G = min(DIFF_GROUP_LANES, 2 * bq)
    h = pl.program_id(1)
    qi = pl.program_id(2)
    nkv = k_ref.shape[1] // bk
    slope2 = slopes_ref[h] * LOG2E

    qt = qt_ref[0]
    sub = lax.broadcasted_iota(jnp.int32, qt.shape, 0)
    zero = jnp.zeros_like(qt)
    q2t_ref[0:LANES, 0:bq] = jnp.where(sub < HEAD_DIM, qt, zero)
    q2t_ref[0:LANES, bq:2 * bq] = jnp.where(sub >= HEAD_DIM, qt, zero)
    max_tiles = feat_ref.shape[0]
    assert max_tiles * BIAS_ROWS <= LANES

    @pl.when(qi == 0)
    def _():
        q2t_ref[LANES:2 * LANES, :] = jnp.zeros((LANES, 2 * bq), BF16)
        f_row = lax.broadcasted_iota(jnp.int32, (bk, LANES), 0)
        f_lane = lax.broadcasted_iota(jnp.int32, (bk, LANES), 1)
        f_slot = f_lane % BIAS_ROWS
        feat = jnp.where(f_slot < 2, f_row % 256,
                         jnp.where(f_slot < 4, f_row // 256,
                                   jnp.where(f_slot < 7, 1, 0)))
        for tt in range(max_tiles):
            mine = f_lane // BIAS_ROWS == tt
            feat_ref[tt] = jnp.where(mine, feat, 0).astype(F32).astype(BF16)

    acc_ref[...] = jnp.zeros(acc_ref.shape, F32)
    key_off = lax.broadcasted_iota(jnp.int32, (bk, G), 0).astype(F32)
    lane = lax.broadcasted_iota(jnp.int32, (1, 2 * bq), 1)
    pos_q = (qi * bq + jnp.where(lane >= bq, lane - bq, lane)).astype(F32)
    ones = jnp.ones((ONES_ROWS, bk), BF16)

    q_rows = nrm_ref.shape[1] * bq // k_ref.shape[1]
    k_big = jnp.max(nrm_ref[0, :, 0, :], axis=0, keepdims=True)
    q_big = jnp.max(nrm_ref[0, pl.ds(qi * q_rows, q_rows), 0, :], axis=0, keepdims=True)
    nlane = lax.broadcasted_iota(jnp.int32, k_big.shape, 1)
    k_max = jnp.max(jnp.where(nlane == NRM_DIFF_K + h, k_big, 0.0))
    q_mine = (nlane == NRM_DIFF_Q + 2 * h) | (nlane == NRM_DIFF_Q + 2 * h + 1)
    q_max = jnp.max(jnp.where(q_mine, q_big, 0.0))
    direct = q_max * k_max <= DIRECT_EXP_MAX_LOGIT ** 2

    def online_tile(j, carry):
        start = pl.multiple_of(j * bk, bk)
        k_t = k_ref[0, pl.ds(start, bk), :]
        vt1 = jnp.concatenate([vt_ref[0, 0, j], ones], axis=0)
        j0 = (j * bk).astype(F32)
        groups = [slice(gi * G, (gi + 1) * G) for gi in range(2 * bq // G)]
        scores = [jnp.dot(k_t, q2t_ref[0:LANES, sl], preferred_element_type=F32)
                  for sl in groups]
        for sl, s in zip(groups, scores):
            u = s - slope2 * jnp.abs(pos_q[:, sl] - (j0 + key_off))
            m_old = m_ref[:, sl]
            m_new = jnp.maximum(m_old, jnp.max(u, axis=0, keepdims=True))
            alpha = jnp.exp2(m_old - m_new)
            p = jnp.exp2(u - m_new).astype(BF16)
            acc_ref[:, sl] = alpha * acc_ref[:, sl] + jnp.dot(
                vt1, p, preferred_element_type=F32)
            m_ref[:, sl] = m_new
        return carry

    def direct_tiles(j, n, side):
        groups = [slice(gi * G, (gi + 1) * G) for gi in range(2 * bq // G)]

        def bias_rows(j0):
            def bf16_part(x):
                return x.astype(BF16).astype(F32)

            sgn = -float(side)
            sv = jnp.zeros_like(pos_q) + sgn * slope2
            s_hi = bf16_part(sv)
            s_lo = bf16_part(sv - s_hi)
            a = sv * (j0 - pos_q)
            a1 = bf16_part(a)
            a2 = bf16_part(a - a1)
            a3 = bf16_part(a - a1 - a2)
            pieces = [s_hi, s_lo, s_hi * 256.0, s_lo * 256.0, a1, a2, a3]
            row = lax.broadcasted_iota(jnp.int32, (BIAS_ROWS, 2 * bq), 0)
            w = jnp.zeros((BIAS_ROWS, 2 * bq), F32)
            for i, piece in enumerate(pieces):
                w = jnp.where(row == i, piece, w)
            return w.astype(BF16)

        def operands(tt):
            jj = j + tt
            k_t = k_ref[0, pl.ds(pl.multiple_of(jj * bk, bk), bk), :]
            j0 = (jj * bk).astype(F32)
            if side != 0:
                q2t_ref[LANES + tt * BIAS_ROWS:LANES + (tt + 1) * BIAS_ROWS, :] = bias_rows(j0)
                k_t = jnp.concatenate([k_t, feat_ref[tt]], axis=1)
            return k_t, vt_ref[0, 0, jj], j0

        def score(k_t, sl):
            rows = slice(0, k_t.shape[1])
            return jnp.dot(k_t, q2t_ref[rows, sl], preferred_element_type=F32)

        def pv(vt, j0, sl, s):
            if side == 0:
                s = s - slope2 * jnp.abs(pos_q[:, sl] - (j0 + key_off))
            p = jnp.exp2(s)
            return (jnp.dot(vt, p.astype(BF16), preferred_element_type=F32),
                    jnp.sum(p, axis=0, keepdims=True))

        tiles = [operands(tt) for tt in range(n)]
        scores = [score(tiles[0][0], sl) for sl in groups]
        sums = [None] * len(groups)
        for tt in range(n):
            _, vt, j0 = tiles[tt]
            nxt = []
            for gi, sl in enumerate(groups):
                x, lx = pv(vt, j0, sl, scores[gi])
                sums[gi] = (x, lx) if tt == 0 else (sums[gi][0] + x, sums[gi][1] + lx)
                if tt + 1 < n:
                    nxt.append(score(tiles[tt + 1][0], sl))
            scores = nxt
        for sl, (x, lx) in zip(groups, sums):
            acc_ref[0:LANES, sl] += x
            acc_ref[LANES:LANES + 1, sl] += lx

    def all_tiles(online):
        ndiag = bq // bk
        if online:
            m_ref[...] = jnp.full(m_ref.shape, -jnp.inf, F32)
            lax.fori_loop(0, nkv, online_tile, 0)
            return

        assert max_tiles % 2 == 0 and ndiag % 2 == 0 and nkv % 2 == 0

        def run(first, count, side):
            def body(t, carry):
                direct_tiles(first + t * max_tiles, max_tiles, side)
                return carry

            full = count // max_tiles
            lax.fori_loop(0, full, body, 0)
            for extra in range(2, max_tiles, 2):
                @pl.when(count - full * max_tiles == extra)
                def _():
                    direct_tiles(first + full * max_tiles, extra, side)

        run(0, qi * ndiag, -1)
        for d in range(0, ndiag, 2):
            direct_tiles(qi * ndiag + d, 2, 0)
        run((qi + 1) * ndiag, nkv - (qi + 1) * ndiag, 1)

    @pl.when(direct)
    def _():
        all_tiles(False)

    @pl.when(jnp.logical_not(direct))
    def _():
        all_tiles(True)

    lq = lam_ref[...]
    lam = (jnp.exp(jnp.sum(lq[0:1] * lq[1:2], axis=1, keepdims=True))
           - jnp.exp(jnp.sum(lq[2:3] * lq[3:4], axis=1, keepdims=True)) + lam_init)
    acc = acc_ref[...]
    ot = acc[0:LANES] * (1.0 / acc[LANES:LANES + 1])
    od = (ot[:, 0:bq] - lam * ot[:, bq:2 * bq]).T
    od = od * lax.rsqrt(jnp.mean(od * od, axis=-1, keepdims=True) + EPS)
    od = od * g_ref[pl.ds(h, 1), :] * (1.0 - lam_init)
    o_ref[0] = od.astype(BF16)


def _diff_attn(qt, k, vt, nrm, slopes, lam_qk, g_diff, lam_init, bq):
    B, S, _ = k.shape
    H = N_DIFF_HEADS
    nkv, bk = vt.shape[2], vt.shape[4]
    assert bq % bk == 0
    kern = functools.partial(_diff_attn_kernel, bq=bq, bk=bk, lam_init=lam_init)
    return pl.pallas_call(
        kern,
        grid=(B, H, S // bq),
        in_specs=[
            pl.BlockSpec(memory_space=pltpu.SMEM),
            pl.BlockSpec((4, HEAD_DIM), lambda b, h, i: (0, 0)),
            pl.BlockSpec((H, LANES), lambda b, h, i: (0, 0)),
            pl.BlockSpec((1, nrm.shape[1], 1, LANES), lambda b, h, i: (b, 0, 0, 0)),
            pl.BlockSpec((1, LANES, bq), lambda b, h, i: (b, h, i)),
            pl.BlockSpec((1, S, LANES), lambda b, h, i: (b, 0, h)),
            pl.BlockSpec((1, 1, nkv, LANES, bk), lambda b, h, i: (b, h, 0, 0, 0)),
        ],
        out_specs=pl.BlockSpec((1, bq, LANES), lambda b, h, i: (b, i, h)),
        out_shape=jax.ShapeDtypeStruct((B, S, H * LANES), BF16),
        scratch_shapes=[
            pltpu.VMEM((2 * LANES, 2 * bq), BF16),
            pltpu.VMEM((1, 2 * bq), F32),
            pltpu.VMEM((LANES + ONES_ROWS, 2 * bq), F32),
            pltpu.VMEM((DIFF_MAX_TILES, bk, LANES), BF16),
        ],
        compiler_params=_params(("parallel", "parallel", "arbitrary")),
        name="diff_attn",
    )(slopes, lam_qk, g_diff, nrm, qt, k, vt)


def _dil_bias(dil, slope_a, slope_b):
    QB, R = LANES, DIL_RADIUS
    KW = QB + 2 * R
    r_i = lax.broadcasted_iota(jnp.int32, (QB, KW), 0)
    c_i = lax.broadcasted_iota(jnp.int32, (QB, KW), 1)
    rel = c_i - r_i - R
    band = (rel >= -R) & (rel <= R)
    dist = (jnp.abs(rel) * dil).astype(F32)
    return jnp.concatenate([jnp.where(band, -(slope_a * LOG2E) * dist, NEG),
                            jnp.where(band, -(slope_b * LOG2E) * dist, NEG)], axis=0)


def _dil_segment(q_at, k_at, v_at, nres, L, dil, tok0, bias_at, first, last, online,
                 sa_ref, sb_ref, m_ref, o_ref, kpad, vpad_a, vpad_b):
    QB, R = LANES, DIL_RADIUS
    KW = QB + 2 * R
    pitch = L + 2 * R
    per_class = L // QB
    lane_q = lax.broadcasted_iota(jnp.int32, (QB, LANES), 1)
    head_a = lane_q < HEAD_DIM
    c2 = lax.broadcasted_iota(jnp.int32, (2 * QB, KW), 1)

    zpad = jnp.zeros((R, LANES), BF16)
    own_a = lax.broadcasted_iota(jnp.int32, (L, LANES), 1) < HEAD_DIM
    one = jnp.ones((L, LANES), BF16)
    for c in range(nres):
        base = c * pitch
        for pad in (kpad, vpad_a, vpad_b):
            pad[base:base + R, :] = zpad
            pad[base + R + L:base + pitch, :] = zpad
        kpad[base + R:base + R + L, :] = k_at(c)
        v_src = v_at(c)
        vpad_a[base + R:base + R + L, :] = jnp.where(own_a, v_src, one)
        vpad_b[base + R:base + R + L, :] = jnp.where(own_a, one, v_src)

    def rows(c, i0):
        if dil == 1:
            return pl.ds(i0, QB)
        return pl.ds(tok0 + c + dil * i0, QB, stride=dil)

    def scores(c, i0):
        q = q_at(c, i0)
        zero = jnp.zeros_like(q)
        q2 = jnp.concatenate([jnp.where(head_a, q, zero), jnp.where(head_a, zero, q)], axis=0)
        s = lax.dot_general(q2, kpad[pl.ds(c * pitch + i0, KW), :], (((1,), (1,)), ((), ())),
                            preferred_element_type=F32)
        return s + bias_at[...]

    def finish(c, i0, s):
        if online:
            key = i0 - R + c2
            s = jnp.where((key >= 0) & (key < L), s, NEG)
            m_c = jnp.max(s, axis=1, keepdims=True)
            s = s - m_c
        p = jnp.exp2(s).astype(BF16)
        win = pl.ds(c * pitch + i0, KW)
        sa = jnp.dot(p[0:QB], vpad_a[win, :], preferred_element_type=F32)
        sb = jnp.dot(p[QB:2 * QB], vpad_b[win, :], preferred_element_type=F32)
        r = rows(c, i0)
        if online:
            ma = jnp.broadcast_to(m_c[0:QB], (QB, LANES))
            mb = jnp.broadcast_to(m_c[QB:2 * QB], (QB, LANES))
            if not first:
                m_p = m_ref[r, :]
                mpa = jnp.max(jnp.where(head_a, m_p, -jnp.inf), axis=1, keepdims=True)
                mpb = jnp.max(jnp.where(head_a, -jnp.inf, m_p), axis=1, keepdims=True)
                mna, mnb = jnp.maximum(mpa, ma), jnp.maximum(mpb, mb)
                sa = sa_ref[r, :] * jnp.exp2(mpa - mna) + sa * jnp.exp2(ma - mna)
                sb = sb_ref[r, :] * jnp.exp2(mpb - mnb) + sb * jnp.exp2(mb - mnb)
                ma, mb = mna, mnb
            if not last:
                m_ref[r, :] = jnp.where(head_a, ma, mb)
        elif not first:
            sa = sa_ref[r, :] + sa
            sb = sb_ref[r, :] + sb
        if last:
            half = LANES // 2
            o_ref[0, 0, r, :] = jnp.where(head_a, sa / pltpu.roll(sa, half, 1),
                                          sb / pltpu.roll(sb, half, 1))
        else:
            sa_ref[r, :] = sa
            sb_ref[r, :] = sb

    def blocks(todo):
        ss = [scores(c, i0) for c, i0 in todo]
        for (c, i0), s in zip(todo, ss):
            finish(c, i0, s)

    if online:
        for c in range(nres):
            def body(n, carry, c=c):
                blocks([(c, pl.multiple_of(n * QB, QB))])
                return carry
            lax.fori_loop(0, per_class, body, 0)
    elif per_class <= DIL_MAX_BLOCKS:
        todo = [(c, u * QB) for c in range(nres) for u in range(per_class)]
        for at in range(0, len(todo), DIL_MAX_BLOCKS):
            blocks(todo[at:at + DIL_MAX_BLOCKS])
    else:
        assert nres == 1 and per_class % DIL_MAX_BLOCKS == 0

        def body(n, carry):
            blocks([(0, pl.multiple_of((n * DIL_MAX_BLOCKS + u) * QB, QB))
                    for u in range(DIL_MAX_BLOCKS)])
            return carry

        lax.fori_loop(0, per_class // DIL_MAX_BLOCKS, body, 0)


def _dil_kernel(slopes_ref, nrm_ref, pb1_ref, pb4_ref, pb16_ref, o_ref,
                sa_ref, sb_ref, m_ref, kpad, vpad_a, vpad_b, bias_ref, *, group16, group4):
    hp = pl.program_id(1)
    t = pl.program_id(2)
    S = pb1_ref.shape[2]
    n16, n4 = 16 // group16, 4 // group4
    dils = (16, 4, 1)

    @pl.when(t == 0)
    def _():
        for i, dil in enumerate(dils):
            bias_ref[i] = _dil_bias(dil, slopes_ref[2 * hp], slopes_ref[2 * hp + 1])

    big = jnp.max(nrm_ref[0, :, 0, :], axis=0, keepdims=True)
    lane = lax.broadcasted_iota(jnp.int32, big.shape, 1)

    def pick(idx):
        return jnp.max(jnp.where(lane == idx, big, 0.0))

    direct = ((pick(4 * hp) * pick(4 * hp + 1) <= DIRECT_EXP_MAX_LOGIT ** 2)
              & (pick(4 * hp + 2) * pick(4 * hp + 3) <= DIRECT_EXP_MAX_LOGIT ** 2))

    def pattern(online):
        args = (sa_ref, sb_ref, m_ref, o_ref, kpad, vpad_a, vpad_b)

        @pl.when(t < n16)
        def _():
            _dil_segment(lambda c, i0: pb16_ref[0, 0, c, pl.ds(i0, LANES), 0:LANES],
                         lambda c: pb16_ref[0, 0, c, :, LANES:2 * LANES],
                         lambda c: pb16_ref[0, 0, c, :, 2 * LANES:3 * LANES],
                         group16, S // 16, 16, t * group16, bias_ref.at[0],
                         True, False, online, *args)

        @pl.when((t >= n16) & (t < n16 + n4))
        def _():
            _dil_segment(lambda c, i0: pb4_ref[0, 0, c, pl.ds(i0, LANES), 0:LANES],
                         lambda c: pb4_ref[0, 0, c, :, LANES:2 * LANES],
                         lambda c: pb4_ref[0, 0, c, :, 2 * LANES:3 * LANES],
                         group4, S // 4, 4, (t - n16) * group4, bias_ref.at[1],
                         False, False, online, *args)

        @pl.when(t == n16 + n4)
        def _():
            _dil_segment(lambda c, i0: pb1_ref[0, 0, pl.ds(i0, LANES), 0:LANES],
                         lambda c: pb1_ref[0, 0, :, LANES:2 * LANES],
                         lambda c: pb1_ref[0, 0, :, 2 * LANES:3 * LANES],
                         1, S, 1, 0, bias_ref.at[2],
                         False, True, online, *args)

    @pl.when(direct)
    def _():
        pattern(False)

    @pl.when(jnp.logical_not(direct))
    def _():
        pattern(True)


def _dil_attn(pb1, pb4, pb16, nrm, slopes, group16=8, group4=2):
    B, P, S, _ = pb1.shape
    n16, n4 = 16 // group16, 4 // group4
    nt = nrm.shape[1]
    pad_rows = max(group16 * (S // 16 + 2 * DIL_RADIUS), group4 * (S // 4 + 2 * DIL_RADIUS),
                   S + 2 * DIL_RADIUS)
    return pl.pallas_call(
        functools.partial(_dil_kernel, group16=group16, group4=group4),
        grid=(B, P, n16 + n4 + 1),
        in_specs=[
            pl.BlockSpec(memory_space=pltpu.SMEM),
            pl.BlockSpec((1, nt, 1, LANES), lambda b, p, t: (b, 0, 0, 0)),
            pl.BlockSpec((1, 1, S, PAIR_W), lambda b, p, t: (b, p, 0, 0)),
            pl.BlockSpec((1, 1, group4, S // 4, PAIR_W),
                         lambda b, p, t: (b, p, jnp.clip(t - n16, 0, n4 - 1), 0, 0)),
            pl.BlockSpec((1, 1, group16, S // 16, PAIR_W),
                         lambda b, p, t: (b, p, jnp.minimum(t, n16 - 1), 0, 0)),
        ],
        out_specs=pl.BlockSpec((1, 1, S, LANES), lambda b, p, t: (b, p, 0, 0)),
        out_shape=jax.ShapeDtypeStruct((B, P, S, LANES), F32),
        scratch_shapes=[
            pltpu.VMEM((S, LANES), F32),
            pltpu.VMEM((S, LANES), F32),
            pltpu.VMEM((S, LANES), F32),
            pltpu.VMEM((pad_rows, LANES), BF16),
            pltpu.VMEM((pad_rows, LANES), BF16),
            pltpu.VMEM((pad_rows, LANES), BF16),
            pltpu.VMEM((3, 2 * LANES, LANES + 2 * DIL_RADIUS), F32),
        ],
        compiler_params=_params(("parallel", "parallel", "arbitrary")),
        name="dil_attn",
    )(slopes, nrm, pb1, pb4, pb16)


def _out_mlp_kernel(x_ref, oa_ref, ob_ref, gdil_ref, wout_ref, gmlp_ref, wup_ref,
                    wdown_ref, gfin_ref, y_ref, *, ff_chunk):
    ob = jnp.concatenate([ob_ref[0, p] for p in range(N_DIL_PAIRS)], axis=-1)
    ob = _rms(ob, gdil_ref[...])
    mix = jnp.concatenate([oa_ref[0], ob.astype(BF16)], axis=-1)
    x1 = x_ref[0] + jnp.dot(mix, wout_ref[...], preferred_element_type=F32)
    h = _rms(x1, gmlp_ref[...]).astype(BF16)
    y = x1
    for c in range(D_FF // ff_chunk):
        u = jnp.dot(h, wup_ref[:, c * ff_chunk:(c + 1) * ff_chunk], preferred_element_type=F32)
        u = jnp.square(jnp.maximum(u, 0.0)).astype(BF16)
        y = y + jnp.dot(u, wdown_ref[c * ff_chunk:(c + 1) * ff_chunk, :],
                        preferred_element_type=F32)
    y_ref[0] = _rms(y, gfin_ref[...])


def _out_mlp(x, oa, ob, g_dil, w_out, g_mlp, w_up, w_down, g_final, tm, ff_chunk):
    B, S, D = x.shape
    const = lambda shape: pl.BlockSpec(shape, lambda b, i: (0,) * len(shape),
                                       pipeline_mode=pl.Buffered(1))
    return pl.pallas_call(
        functools.partial(_out_mlp_kernel, ff_chunk=ff_chunk),
        grid=(B, S // tm),
        in_specs=[
            pl.BlockSpec((1, tm, D), lambda b, i: (b, i, 0)),
            pl.BlockSpec((1, tm, N_DIFF_HEADS * LANES), lambda b, i: (b, i, 0)),
            pl.BlockSpec((1, N_DIL_PAIRS, tm, LANES), lambda b, i: (b, 0, i, 0)),
            const((1, DIL_W)),
            const((2 * DIL_W, D)),
            const((1, D)),
            const((D, D_FF)),
            const((D_FF, D)),
            const((1, D)),
        ],
        out_specs=pl.BlockSpec((1, tm, D), lambda b, i: (b, i, 0)),
        out_shape=jax.ShapeDtypeStruct((B, S, D), F32),
        compiler_params=_params(("parallel", "parallel")),
        name="out_mlp",
    )(x, oa, ob, g_dil, w_out, g_mlp, w_up, w_down, g_final)


def _alibi_slopes(n):
    return 2.0 ** (-8.0 * jnp.arange(1, n + 1, dtype=F32) / n)


def _layer(x, g_mix, w_in, wq_t, wv_t, lam_qk, g_diff, g_dil, w_out, g_mlp, w_up, w_down,
           g_final, lam_init):
    qt, k, vt, pb1, pb4, pb16, nrm = _in_proj(x, g_mix, w_in, wq_t, wv_t, tm=512)
    oa = _diff_attn(qt, k, vt, nrm, _alibi_slopes(N_DIFF_HEADS), lam_qk, g_diff, lam_init,
                    bq=1024)
    assert DIL_PATTERNS == ((2 * DIL_RADIUS, 1), (8 * DIL_RADIUS, 4), (32 * DIL_RADIUS, 16))
    ob = _dil_attn(pb1, pb4, pb16, nrm, _alibi_slopes(N_DIL_HEADS))
    return _out_mlp(x, oa, ob, g_dil, w_out, g_mlp, w_up, w_down, g_final,
                    tm=1024, ff_chunk=1024)


def kernel(x_prompt, x_sample, g_mix, w_in, lam_qk, g_diff, g_dil, w_out, g_mlp, w_up,
           w_down, g_final):
    assert g_mix.shape[0] == 1
    lam_init = 0.8 - 0.6 * math.exp(-0.3 * 0)
    w_in_b = w_in[0].astype(BF16)
    weights = (
        g_mix[0][None, :], w_in_b, w_in_b[:, 0:DIFF_QW].T, w_in_b[:, 2 * DIFF_QW:DIFF_W].T,
        lam_qk[0],
        g_diff[0], g_dil[0][None, :], w_out[0].astype(BF16), g_mlp[0][None, :],
        w_up[0].astype(BF16), w_down[0].astype(BF16), g_final[None, :],
    )
    return tuple(_layer(x, *weights, lam_init) for x in (x_prompt, x_sample))
```

```python
import functools
import math

import jax
import jax.numpy as jnp
from jax import lax
from jax.experimental import pallas as pl
from jax.experimental.pallas import tpu as pltpu

D_MODEL = 1024
HEAD_DIM = 64
N_DIFF_HEADS = 4
N_DIL_HEADS = 8
N_DIL_PAIRS = N_DIL_HEADS // 2
DIL_PATTERNS = ((128, 1), (512, 4), (2048, 16))
DIL_RADIUS = 64
D_FF = 4 * D_MODEL
EPS = 1e-5
DIFF_QW = N_DIFF_HEADS * 2 * HEAD_DIM
DIFF_W = 3 * DIFF_QW
DIL_W = N_DIL_HEADS * HEAD_DIM
IN_W = DIFF_W + 3 * DIL_W
LANES = 128
PAIR_W = 3 * LANES
NEG = -1e30
LOG2E = math.log2(math.e)
ONES_ROWS = 16
DIRECT_EXP_MAX_LOGIT = 60.0
BIAS_ROWS = 16
DIFF_MAX_TILES = 2
DIFF_GROUP_LANES = 8 * LANES
DIL_MAX_BLOCKS = 16
NRM_DIFF_K = 4 * N_DIL_PAIRS
NRM_DIFF_Q = NRM_DIFF_K + N_DIFF_HEADS
VMEM_LIMIT = 56 * 1024 * 1024

BF16 = jnp.bfloat16
F32 = jnp.float32


def _rms(x, g):
    return x * lax.rsqrt(jnp.mean(x * x, axis=-1, keepdims=True) + EPS) * g


def _params(sem):
    return pltpu.CompilerParams(dimension_semantics=sem, vmem_limit_bytes=VMEM_LIMIT)


def _in_proj_kernel(x_ref, g_ref, w_ref, wqt_ref, wvt_ref, qt_ref, k_ref, vt_ref,
                    pb1_ref, pb4_ref, pb16_ref, nrm_ref, slab_ref, slab4_ref):
    scale = HEAD_DIM ** -0.5
    nt = (((1,), (1,)), ((), ()))
    tm = x_ref.shape[1]
    lane = lax.broadcasted_iota(jnp.int32, (1, LANES), 1)
    nrm = jnp.zeros((1, LANES), F32)
    h = _rms(x_ref[0], g_ref[...]).astype(BF16)
    for c in range(3):
        p = jnp.dot(h, w_ref[:, DIFF_W + c * DIL_W:DIFF_W + (c + 1) * DIL_W],
                    preferred_element_type=F32)
        if c == 0:
            p = p * (scale * LOG2E)
        cols = slice(c * LANES, (c + 1) * LANES)
        for hp in range(N_DIL_PAIRS):
            ph = p[:, hp * LANES:(hp + 1) * LANES]
            if c < 2:
                sq = ph * ph
                for e in range(2):
                    mine = (lane >= e * HEAD_DIM) & (lane < (e + 1) * HEAD_DIM)
                    big = jnp.max(jnp.sum(jnp.where(mine, sq, 0.0), axis=1, keepdims=True),
                                  axis=0, keepdims=True)
                    nrm = jnp.where(lane == 4 * hp + 2 * e + c, big, nrm)
            pb1_ref[0, hp, :, cols] = ph.astype(BF16)
            s = c * N_DIL_PAIRS + hp
            slab_ref[s] = ph
            for g in range(4):
                v4 = slab_ref[s, pl.ds(g, tm // 4, stride=4), :]
                pb4_ref[0, hp, g, :, cols] = v4.astype(BF16)
                slab4_ref[s, g] = v4
            for g in range(4):
                for c2 in range(4):
                    v16 = slab4_ref[s, g, pl.ds(c2, tm // 16, stride=4), :]
                    pb16_ref[0, hp, 4 * c2 + g, :, cols] = v16.astype(BF16)
    kd = jnp.dot(h, w_ref[:, DIFF_QW:2 * DIFF_QW], preferred_element_type=F32)
    k_ref[0] = kd.astype(BF16)
    for hh in range(N_DIFF_HEADS):
        kh = kd[:, hh * LANES:(hh + 1) * LANES]
        big = jnp.max(jnp.sum(kh * kh, axis=1, keepdims=True), axis=0, keepdims=True)
        nrm = jnp.where(lane == NRM_DIFF_K + hh, big, nrm)
    qt = lax.dot_general(wqt_ref[...], h, nt, preferred_element_type=F32) * (scale * LOG2E)
    qt_ref[0] = qt.astype(BF16)
    for hm in range(2 * N_DIFF_HEADS):
        qm = qt[hm * HEAD_DIM:(hm + 1) * HEAD_DIM, :]
        big = jnp.max(jnp.sum(qm * qm, axis=0, keepdims=True), axis=1, keepdims=True)
        nrm = jnp.where(lane == NRM_DIFF_Q + hm, big, nrm)
    nrm_ref[0, 0] = nrm
    vt = lax.dot_general(wvt_ref[...], h, nt, preferred_element_type=F32).astype(BF16)
    for hh in range(N_DIFF_HEADS):
        vt_ref[0, hh, 0] = vt[hh * LANES:(hh + 1) * LANES, :]


def _in_proj(x, g_mix, w_in, wq_t, wv_t, tm):
    B, S, D = x.shape
    H = N_DIFF_HEADS
    P = N_DIL_PAIRS
    return pl.pallas_call(
        _in_proj_kernel,
        grid=(B, S // tm),
        in_specs=[
            pl.BlockSpec((1, tm, D), lambda b, i: (b, i, 0)),
            pl.BlockSpec((1, D), lambda b, i: (0, 0)),
            pl.BlockSpec((D, IN_W), lambda b, i: (0, 0)),
            pl.BlockSpec((DIFF_QW, D), lambda b, i: (0, 0)),
            pl.BlockSpec((DIFF_QW, D), lambda b, i: (0, 0)),
        ],
        out_specs=[
            pl.BlockSpec((1, DIFF_QW, tm), lambda b, i: (b, 0, i)),
            pl.BlockSpec((1, tm, DIFF_QW), lambda b, i: (b, i, 0)),
            pl.BlockSpec((1, H, 1, LANES, tm), lambda b, i: (b, 0, i, 0, 0)),
            pl.BlockSpec((1, P, tm, PAIR_W), lambda b, i: (b, 0, i, 0)),
            pl.BlockSpec((1, P, 4, tm // 4, PAIR_W), lambda b, i: (b, 0, 0, i, 0)),
            pl.BlockSpec((1, P, 16, tm // 16, PAIR_W), lambda b, i: (b, 0, 0, i, 0)),
            pl.BlockSpec((1, 1, 1, LANES), lambda b, i: (b, i, 0, 0)),
        ],
        out_shape=[
            jax.ShapeDtypeStruct((B, DIFF_QW, S), BF16),
            jax.ShapeDtypeStruct((B, S, DIFF_QW), BF16),
            jax.ShapeDtypeStruct((B, H, S // tm, LANES, tm), BF16),
            jax.ShapeDtypeStruct((B, P, S, PAIR_W), BF16),
            jax.ShapeDtypeStruct((B, P, 4, S // 4, PAIR_W), BF16),
            jax.ShapeDtypeStruct((B, P, 16, S // 16, PAIR_W), BF16),
            jax.ShapeDtypeStruct((B, S // tm, 1, LANES), F32),
        ],
        scratch_shapes=[
            pltpu.VMEM((3 * P, tm, LANES), F32),
            pltpu.VMEM((3 * P, 4, tm // 4, LANES), F32),
        ],
        compiler_params=_params(("parallel", "parallel")),
        name="in_proj",
    )(x, g_mix, w_in, wq_t, wv_t)


def _diff_attn_kernel(slopes_ref, lam_ref, g_ref, nrm_ref, qt_ref, k_ref, vt_ref, o_ref,
                      q2t_ref, m_ref, acc_ref, feat_ref, *, bq, bk, lam_init):
    G = min(DIFF_GROUP_LANES, 2 * bq)
    h = pl.program_id(1)
    qi = pl.program_id(2)
    nkv = k_ref.shape[1] // bk
    slope2 = slopes_ref[h] * LOG2E

    qt = qt_ref[0]
    sub = lax.broadcasted_iota(jnp.int32, qt.shape, 0)
    zero = jnp.zeros_like(qt)
    q2t_ref[0:LANES, 0:bq] = jnp.where(sub < HEAD_DIM, qt, zero)
    q2t_ref[0:LANES, bq:2 * bq] = jnp.where(sub >= HEAD_DIM, qt, zero)
    max_tiles = feat_ref.shape[0]
    assert max_tiles * BIAS_ROWS <= LANES

    @pl.when(qi == 0)
    def _():
        q2t_ref[LANES:2 * LANES, :] = jnp.zeros((LANES, 2 * bq), BF16)
        f_row = lax.broadcasted_iota(jnp.int32, (bk, LANES), 0)
        f_lane = lax.broadcasted_iota(jnp.int32, (bk, LANES), 1)
        f_slot = f_lane % BIAS_ROWS
        feat = jnp.where(f_slot < 2, f_row % 256,
                         jnp.where(f_slot < 4, f_row // 256,
                                   jnp.where(f_slot < 7, 1, 0)))
        for tt in range(max_tiles):
            mine = f_lane // BIAS_ROWS == tt
            feat_ref[tt] = jnp.where(mine, feat, 0).astype(F32).astype(BF16)

    acc_ref[...] = jnp.zeros(acc_ref.shape, F32)
    key_off = lax.broadcasted_iota(jnp.int32, (bk, G), 0).astype(F32)
    lane = lax.broadcasted_iota(jnp.int32, (1, 2 * bq), 1)
    pos_q = (qi * bq + jnp.where(lane >= bq, lane - bq, lane)).astype(F32)
    ones = jnp.ones((ONES_ROWS, bk), BF16)

    q_rows = nrm_ref.shape[1] * bq // k_ref.shape[1]
    k_big = jnp.max(nrm_ref[0, :, 0, :], axis=0, keepdims=True)
    q_big = jnp.max(nrm_ref[0, pl.ds(qi * q_rows, q_rows), 0, :], axis=0, keepdims=True)
    nlane = lax.broadcasted_iota(jnp.int32, k_big.shape, 1)
    k_max = jnp.max(jnp.where(nlane == NRM_DIFF_K + h, k_big, 0.0))
    q_mine = (nlane == NRM_DIFF_Q + 2 * h) | (nlane == NRM_DIFF_Q + 2 * h + 1)
    q_max = jnp.max(jnp.where(q_mine, q_big, 0.0))
    direct = q_max * k_max <= DIRECT_EXP_MAX_LOGIT ** 2

    def online_tile(j, carry):
        j = jnp.asarray(j, jnp.int32)
        start = pl.multiple_of(j * bk, bk)
        k_t = k_ref[0, pl.ds(start, bk), :]
        vt1 = jnp.concatenate([vt_ref[0, 0, j], ones], axis=0)
        j0 = (j * bk).astype(F32)
        groups = [slice(gi * G, (gi + 1) * G) for gi in range(2 * bq // G)]
        scores = [jnp.dot(k_t, q2t_ref[0:LANES, sl], preferred_element_type=F32)
                  for sl in groups]
        for sl, s in zip(groups, scores):
            u = s - slope2 * jnp.abs(pos_q[:, sl] - (j0 + key_off))
            m_old = m_ref[:, sl]
            m_new = jnp.maximum(m_old, jnp.max(u, axis=0, keepdims=True))
            alpha = jnp.exp2(m_old - m_new)
            p = jnp.exp2(u - m_new).astype(BF16)
            acc_ref[:, sl] = alpha * acc_ref[:, sl] + jnp.dot(
                vt1, p, preferred_element_type=F32)
            m_ref[:, sl] = m_new
        return carry

    def direct_tiles(j, n, side):
        groups = [slice(gi * G, (gi + 1) * G) for gi in range(2 * bq // G)]

        def bias_rows(j0):
            def bf16_part(x):
                return x.astype(BF16).astype(F32)

            sgn = -float(side)
            sv = jnp.zeros_like(pos_q) + sgn * slope2
            s_hi = bf16_part(sv)
            s_lo = bf16_part(sv - s_hi)
            a = sv * (j0 - pos_q)
            a1 = bf16_part(a)
            a2 = bf16_part(a - a1)
            a3 = bf16_part(a - a1 - a2)
            pieces = [s_hi, s_lo, s_hi * 256.0, s_lo * 256.0, a1, a2, a3]
            row = lax.broadcasted_iota(jnp.int32, (BIAS_ROWS, 2 * bq), 0)
            w = jnp.zeros((BIAS_ROWS, 2 * bq), F32)
            for i, piece in enumerate(pieces):
                w = jnp.where(row == i, piece, w)
            return w.astype(BF16)

        def operands(tt):
            jj = j + tt
            k_t = k_ref[0, pl.ds(pl.multiple_of(jj * bk, bk), bk), :]
            j0 = (jj * bk).astype(F32)
            if side != 0:
                q2t_ref[LANES + tt * BIAS_ROWS:LANES + (tt + 1) * BIAS_ROWS, :] = bias_rows(j0)
                k_t = jnp.concatenate([k_t, feat_ref[tt]], axis=1)
            return k_t, vt_ref[0, 0, jj], j0

        def score(k_t, sl):
            rows = slice(0, k_t.shape[1])
            return jnp.dot(k_t, q2t_ref[rows, sl], preferred_element_type=F32)

        def pv(vt, j0, sl, s):
            if side == 0:
                s = s - slope2 * jnp.abs(pos_q[:, sl] - (j0 + key_off))
            p = jnp.exp2(s)
            return (jnp.dot(vt, p.astype(BF16), preferred_element_type=F32),
                    jnp.sum(p, axis=0, keepdims=True))

        tiles = [operands(tt) for tt in range(n)]
        scores = [score(tiles[0][0], sl) for sl in groups]
        sums = [None] * len(groups)
        for tt in range(n):
            _, vt, j0 = tiles[tt]
            nxt = []
            for gi, sl in enumerate(groups):
                x, lx = pv(vt, j0, sl, scores[gi])
                sums[gi] = (x, lx) if tt == 0 else (sums[gi][0] + x, sums[gi][1] + lx)
                if tt + 1 < n:
                    nxt.append(score(tiles[tt + 1][0], sl))
            scores = nxt
        for sl, (x, lx) in zip(groups, sums):
            acc_ref[0:LANES, sl] += x
            acc_ref[LANES:LANES + 1, sl] += lx

    def all_tiles(online):
        ndiag = bq // bk
        if online:
            m_ref[...] = jnp.full(m_ref.shape, -jnp.inf, F32)
            lax.fori_loop(0, nkv, online_tile, 0)
            return

        assert max_tiles % 2 == 0 and ndiag % 2 == 0 and nkv % 2 == 0

        def run(first, count, side):
            def body(t, carry):
                direct_tiles(first + t * max_tiles, max_tiles, side)
                return carry

            full = count // max_tiles
            lax.fori_loop(0, full, body, 0)
            for extra in range(2, max_tiles, 2):
                @pl.when(count - full * max_tiles == extra)
                def _():
                    direct_tiles(first + full * max_tiles, extra, side)

        run(0, qi * ndiag, -1)
        for d in range(0, ndiag, 2):
            direct_tiles(qi * ndiag + d, 2, 0)
        run((qi + 1) * ndiag, nkv - (qi + 1) * ndiag, 1)

    @pl.when(direct)
    def _():
        all_tiles(False)

    @pl.when(jnp.logical_not(direct))
    def _():
        all_tiles(True)

    lq = lam_ref[...]
    lam = (jnp.exp(jnp.sum(lq[0:1] * lq[1:2], axis=1, keepdims=True))
           - jnp.exp(jnp.sum(lq[2:3] * lq[3:4], axis=1, keepdims=True)) + lam_init)
    acc = acc_ref[...]
    ot = acc[0:LANES] * (1.0 / acc[LANES:LANES + 1])
    od = (ot[:, 0:bq] - lam * ot[:, bq:2 * bq]).T
    od = od * lax.rsqrt(jnp.mean(od * od, axis=-1, keepdims=True) + EPS)
    od = od * g_ref[pl.ds(h, 1), :] * (1.0 - lam_init)
    o_ref[0] = od.astype(BF16)


def _diff_attn(qt, k, vt, nrm, slopes, lam_qk, g_diff, lam_init, bq):
    B, S, _ = k.shape
    H = N_DIFF_HEADS
    nkv, bk = vt.shape[2], vt.shape[4]
    assert bq % bk == 0
    kern = functools.partial(_diff_attn_kernel, bq=bq, bk=bk, lam_init=lam_init)
    return pl.pallas_call(
        kern,
        grid=(B, H, S // bq),
        in_specs=[
            pl.BlockSpec(memory_space=pltpu.SMEM),
            pl.BlockSpec((4, HEAD_DIM), lambda b, h, i: (0, 0)),
            pl.BlockSpec((H, LANES), lambda b, h, i: (0, 0)),
            pl.BlockSpec((1, nrm.shape[1], 1, LANES), lambda b, h, i: (b, 0, 0, 0)),
            pl.BlockSpec((1, LANES, bq), lambda b, h, i: (b, h, i)),
            pl.BlockSpec((1, S, LANES), lambda b, h, i: (b, 0, h)),
            pl.BlockSpec((1, 1, nkv, LANES, bk), lambda b, h, i: (b, h, 0, 0, 0)),
        ],
        out_specs=pl.BlockSpec((1, bq, LANES), lambda b, h, i: (b, i, h)),
        out_shape=jax.ShapeDtypeStruct((B, S, H * LANES), BF16),
        scratch_shapes=[
            pltpu.VMEM((2 * LANES, 2 * bq), BF16),
            pltpu.VMEM((1, 2 * bq), F32),
            pltpu.VMEM((LANES + ONES_ROWS, 2 * bq), F32),
            pltpu.VMEM((DIFF_MAX_TILES, bk, LANES), BF16),
        ],
        compiler_params=_params(("parallel", "parallel", "arbitrary")),
        name="diff_attn",
    )(slopes, lam_qk, g_diff, nrm, qt, k, vt)


def _dil_bias(dil, slope_a, slope_b):
    QB, R = LANES, DIL_RADIUS
    KW = QB + 2 * R
    r_i = lax.broadcasted_iota(jnp.int32, (QB, KW), 0)
    c_i = lax.broadcasted_iota(jnp.int32, (QB, KW), 1)
    rel = c_i - r_i - R
    band = (rel >= -R) & (rel <= R)
    dist = (jnp.abs(rel) * dil).astype(F32)
    return jnp.concatenate([jnp.where(band, -(slope_a * LOG2E) * dist, NEG),
                            jnp.where(band, -(slope_b * LOG2E) * dist, NEG)], axis=0)


def _dil_segment(q_at, k_at, v_at, nres, L, dil, tok0, bias_at, first, last, online,
                 sa_ref, sb_ref, m_ref, o_ref, kpad, vpad_a, vpad_b):
    QB, R = LANES, DIL_RADIUS
    KW = QB + 2 * R
    pitch = L + 2 * R
    per_class = L // QB
    lane_q = lax.broadcasted_iota(jnp.int32, (QB, LANES), 1)
    head_a = lane_q < HEAD_DIM
    c2 = lax.broadcasted_iota(jnp.int32, (2 * QB, KW), 1)

    zpad = jnp.zeros((R, LANES), BF16)
    own_a = lax.broadcasted_iota(jnp.int32, (L, LANES), 1) < HEAD_DIM
    one = jnp.ones((L, LANES), BF16)
    for c in range(nres):
        base = c * pitch
        for pad in (kpad, vpad_a, vpad_b):
            pad[base:base + R, :] = zpad
            pad[base + R + L:base + pitch, :] = zpad
        kpad[base + R:base + R + L, :] = k_at(c)
        v_src = v_at(c)
        vpad_a[base + R:base + R + L, :] = jnp.where(own_a, v_src, one)
        vpad_b[base + R:base + R + L, :] = jnp.where(own_a, one, v_src)

    def rows(c, i0):
        if dil == 1:
            return pl.ds(i0, QB)
        return pl.ds(tok0 + c + dil * i0, QB, stride=dil)

    def scores(c, i0):
        q = q_at(c, i0)
        zero = jnp.zeros_like(q)
        q2 = jnp.concatenate([jnp.where(head_a, q, zero), jnp.where(head_a, zero, q)], axis=0)
        s = lax.dot_general(q2, kpad[pl.ds(c * pitch + i0, KW), :], (((1,), (1,)), ((), ())),
                            preferred_element_type=F32)
        return s + bias_at[...]

    def finish(c, i0, s):
        if online:
            key = i0 - R + c2
            s = jnp.where((key >= 0) & (key < L), s, NEG)
            m_c = jnp.max(s, axis=1, keepdims=True)
            s = s - m_c
        p = jnp.exp2(s).astype(BF16)
        win = pl.ds(c * pitch + i0, KW)
        sa = jnp.dot(p[0:QB], vpad_a[win, :], preferred_element_type=F32)
        sb = jnp.dot(p[QB:2 * QB], vpad_b[win, :], preferred_element_type=F32)
        r = rows(c, i0)
        if online:
            ma = jnp.broadcast_to(m_c[0:QB], (QB, LANES))
            mb = jnp.broadcast_to(m_c[QB:2 * QB], (QB, LANES))
            if not first:
                m_p = m_ref[r, :]
                mpa = jnp.max(jnp.where(head_a, m_p, -jnp.inf), axis=1, keepdims=True)
                mpb = jnp.max(jnp.where(head_a, -jnp.inf, m_p), axis=1, keepdims=True)
                mna, mnb = jnp.maximum(mpa, ma), jnp.maximum(mpb, mb)
                sa = sa_ref[r, :] * jnp.exp2(mpa - mna) + sa * jnp.exp2(ma - mna)
                sb = sb_ref[r, :] * jnp.exp2(mpb - mnb) + sb * jnp.exp2(mb - mnb)
                ma, mb = mna, mnb
            if not last:
                m_ref[r, :] = jnp.where(head_a, ma, mb)
        elif not first:
            sa = sa_ref[r, :] + sa
            sb = sb_ref[r, :] + sb
        if last:
            half = LANES // 2
            o_ref[0, 0, r, :] = jnp.where(head_a, sa / pltpu.roll(sa, half, 1),
                                          sb / pltpu.roll(sb, half, 1))
        else:
            sa_ref[r, :] = sa
            sb_ref[r, :] = sb

    def blocks(todo):
        ss = [scores(c, i0) for c, i0 in todo]
        for (c, i0), s in zip(todo, ss):
            finish(c, i0, s)

    if online:
        for c in range(nres):
            def body(n, carry, c=c):
                n = jnp.asarray(n, jnp.int32)
                blocks([(c, pl.multiple_of(n * QB, QB))])
                return carry
            lax.fori_loop(0, per_class, body, 0)
    elif per_class <= DIL_MAX_BLOCKS:
        todo = [(c, u * QB) for c in range(nres) for u in range(per_class)]
        for at in range(0, len(todo), DIL_MAX_BLOCKS):
            blocks(todo[at:at + DIL_MAX_BLOCKS])
    else:
        assert nres == 1 and per_class % DIL_MAX_BLOCKS == 0

        def body(n, carry):
            n = jnp.asarray(n, jnp.int32)
            blocks([(0, pl.multiple_of((n * DIL_MAX_BLOCKS + u) * QB, QB))
                    for u in range(DIL_MAX_BLOCKS)])
            return carry

        lax.fori_loop(0, per_class // DIL_MAX_BLOCKS, body, 0)


def _dil_kernel(slopes_ref, nrm_ref, pb1_ref, pb4_ref, pb16_ref, o_ref,
                sa_ref, sb_ref, m_ref, kpad, vpad_a, vpad_b, bias_ref, *, group16, group4):
    hp = pl.program_id(1)
    t = pl.program_id(2)
    S = pb1_ref.shape[2]
    n16, n4 = 16 // group16, 4 // group4
    dils = (16, 4, 1)

    @pl.when(t == 0)
    def _():
        for i, dil in enumerate(dils):
            bias_ref[i] = _dil_bias(dil, slopes_ref[2 * hp], slopes_ref[2 * hp + 1])

    big = jnp.max(nrm_ref[0, :, 0, :], axis=0, keepdims=True)
    lane = lax.broadcasted_iota(jnp.int32, big.shape, 1)

    def pick(idx):
        return jnp.max(jnp.where(lane == idx, big, 0.0))

    direct = ((pick(4 * hp) * pick(4 * hp + 1) <= DIRECT_EXP_MAX_LOGIT ** 2)
              & (pick(4 * hp + 2) * pick(4 * hp + 3) <= DIRECT_EXP_MAX_LOGIT ** 2))

    def pattern(online):
        args = (sa_ref, sb_ref, m_ref, o_ref, kpad, vpad_a, vpad_b)

        @pl.when(t < n16)
        def _():
            _dil_segment(lambda c, i0: pb16_ref[0, 0, c, pl.ds(i0, LANES), 0:LANES],
                         lambda c: pb16_ref[0, 0, c, :, LANES:2 * LANES],
                         lambda c: pb16_ref[0, 0, c, :, 2 * LANES:3 * LANES],
                         group16, S // 16, 16, t * group16, bias_ref.at[0],
                         True, False, online, *args)

        @pl.when((t >= n16) & (t < n16 + n4))
        def _():
            _dil_segment(lambda c, i0: pb4_ref[0, 0, c, pl.ds(i0, LANES), 0:LANES],
                         lambda c: pb4_ref[0, 0, c, :, LANES:2 * LANES],
                         lambda c: pb4_ref[0, 0, c, :, 2 * LANES:3 * LANES],
                         group4, S // 4, 4, (t - n16) * group4, bias_ref.at[1],
                         False, False, online, *args)

        @pl.when(t == n16 + n4)
        def _():
            _dil_segment(lambda c, i0: pb1_ref[0, 0, pl.ds(i0, LANES), 0:LANES],
                         lambda c: pb1_ref[0, 0, :, LANES:2 * LANES],
                         lambda c: pb1_ref[0, 0, :, 2 * LANES:3 * LANES],
                         1, S, 1, 0, bias_ref.at[2],
                         False, True, online, *args)

    @pl.when(direct)
    def _():
        pattern(False)

    @pl.when(jnp.logical_not(direct))
    def _():
        pattern(True)


def _dil_attn(pb1, pb4, pb16, nrm, slopes, group16=8, group4=2):
    B, P, S, _ = pb1.shape
    n16, n4 = 16 // group16, 4 // group4
    nt = nrm.shape[1]
    pad_rows = max(group16 * (S // 16 + 2 * DIL_RADIUS), group4 * (S // 4 + 2 * DIL_RADIUS),
                   S + 2 * DIL_RADIUS)
    return pl.pallas_call(
        functools.partial(_dil_kernel, group16=group16, group4=group4),
        grid=(B, P, n16 + n4 + 1),
        in_specs=[
            pl.BlockSpec(memory_space=pltpu.SMEM),
            pl.BlockSpec((1, nt, 1, LANES), lambda b, p, t: (b, 0, 0, 0)),
            pl.BlockSpec((1, 1, S, PAIR_W), lambda b, p, t: (b, p, 0, 0)),
            pl.BlockSpec((1, 1, group4, S // 4, PAIR_W),
                         lambda b, p, t: (b, p, jnp.clip(t - n16, 0, n4 - 1), 0, 0)),
            pl.BlockSpec((1, 1, group16, S // 16, PAIR_W),
                         lambda b, p, t: (b, p, jnp.minimum(t, n16 - 1), 0, 0)),
        ],
        out_specs=pl.BlockSpec((1, 1, S, LANES), lambda b, p, t: (b, p, 0, 0)),
        out_shape=jax.ShapeDtypeStruct((B, P, S, LANES), F32),
        scratch_shapes=[
            pltpu.VMEM((S, LANES), F32),
            pltpu.VMEM((S, LANES), F32),
            pltpu.VMEM((S, LANES), F32),
            pltpu.VMEM((pad_rows, LANES), BF16),
            pltpu.VMEM((pad_rows, LANES), BF16),
            pltpu.VMEM((pad_rows, LANES), BF16),
            pltpu.VMEM((3, 2 * LANES, LANES + 2 * DIL_RADIUS), F32),
        ],
        compiler_params=_params(("parallel", "parallel", "arbitrary")),
        name="dil_attn",
    )(slopes, nrm, pb1, pb4, pb16)


def _out_mlp_kernel(x_ref, oa_ref, ob_ref, gdil_ref, wout_ref, gmlp_ref, wup_ref,
                    wdown_ref, gfin_ref, y_ref, *, ff_chunk):
    ob = jnp.concatenate([ob_ref[0, p] for p in range(N_DIL_PAIRS)], axis=-1)
    ob = _rms(ob, gdil_ref[...])
    mix = jnp.concatenate([oa_ref[0], ob.astype(BF16)], axis=-1)
    x1 = x_ref[0] + jnp.dot(mix, wout_ref[...], preferred_element_type=F32)
    h = _rms(x1, gmlp_ref[...]).astype(BF16)
    y = x1
    for c in range(D_FF // ff_chunk):
        u = jnp.dot(h, wup_ref[:, c * ff_chunk:(c + 1) * ff_chunk], preferred_element_type=F32)
        u = jnp.square(jnp.maximum(u, 0.0)).astype(BF16)
        y = y + jnp.dot(u, wdown_ref[c * ff_chunk:(c + 1) * ff_chunk, :],
                        preferred_element_type=F32)
    y_ref[0] = _rms(y, gfin_ref[...])


def _out_mlp(x, oa, ob, g_dil, w_out, g_mlp, w_up, w_down, g_final, tm, ff_chunk):
    B, S, D = x.shape
    const = lambda shape: pl.BlockSpec(shape, lambda b, i: (0,) * len(shape),
                                       pipeline_mode=pl.Buffered(1))
    return pl.pallas_call(
        functools.partial(_out_mlp_kernel, ff_chunk=ff_chunk),
        grid=(B, S // tm),
        in_specs=[
            pl.BlockSpec((1, tm, D), lambda b, i: (b, i, 0)),
            pl.BlockSpec((1, tm, N_DIFF_HEADS * LANES), lambda b, i: (b, i, 0)),
            pl.BlockSpec((1, N_DIL_PAIRS, tm, LANES), lambda b, i: (b, 0, i, 0)),
            const((1, DIL_W)),
            const((2 * DIL_W, D)),
            const((1, D)),
            const((D, D_FF)),
            const((D_FF, D)),
            const((1, D)),
        ],
        out_specs=pl.BlockSpec((1, tm, D), lambda b, i: (b, i, 0)),
        out_shape=jax.ShapeDtypeStruct((B, S, D), F32),
        compiler_params=_params(("parallel", "parallel")),
        name="out_mlp",
    )(x, oa, ob, g_dil, w_out, g_mlp, w_up, w_down, g_final)


def _alibi_slopes(n):
    return 2.0 ** (-8.0 * jnp.arange(1, n + 1, dtype=F32) / n)


def _layer(x, g_mix, w_in, wq_t, wv_t, lam_qk, g_diff, g_dil, w_out, g_mlp, w_up, w_down,
           g_final, lam_init):
    qt, k, vt, pb1, pb4, pb16, nrm = _in_proj(x, g_mix, w_in, wq_t, wv_t, tm=512)
    oa = _diff_attn(qt, k, vt, nrm, _alibi_slopes(N_DIFF_HEADS), lam_qk, g_diff, lam_init,
                    bq=1024)
    assert DIL_PATTERNS == ((2 * DIL_RADIUS, 1), (8 * DIL_RADIUS, 4), (32 * DIL_RADIUS, 16))
    ob = _dil_attn(pb1, pb4, pb16, nrm, _alibi_slopes(N_DIL_HEADS))
    return _out_mlp(x, oa, ob, g_dil, w_out, g_mlp, w_up, w_down, g_final,
                    tm=1024, ff_chunk=1024)


def kernel(x_prompt, x_sample, g_mix, w_in, lam_qk, g_diff, g_dil, w_out, g_mlp, w_up,
           w_down, g_final):
    assert g_mix.shape[0] == 1
    lam_init = 0.8 - 0.6 * math.exp(-0.3 * 0)
    w_in_b = w_in[0].astype(BF16)
    weights = (
        g_mix[0][None, :], w_in_b, w_in_b[:, 0:DIFF_QW].T, w_in_b[:, 2 * DIFF_QW:DIFF_W].T,
        lam_qk[0],
        g_diff[0], g_dil[0][None, :], w_out[0].astype(BF16), g_mlp[0][None, :],
        w_up[0].astype(BF16), w_down[0].astype(BF16), g_final[None, :],
    )
    return tuple(_layer(x, *weights, lam_init) for x in (x_prompt, x_sample))
```

```python
import functools
import math

import jax
import jax.numpy as jnp
from jax import lax
from jax.experimental import pallas as pl
from jax.experimental.pallas import tpu as pltpu

D_MODEL = 1024
HEAD_DIM = 64
N_DIFF_HEADS = 4
N_DIL_HEADS = 8
N_DIL_PAIRS = N_DIL_HEADS // 2
DIL_PATTERNS = ((128, 1), (512, 4), (2048, 16))
DIL_RADIUS = 64
D_FF = 4 * D_MODEL
EPS = 1e-5
DIFF_QW = N_DIFF_HEADS * 2 * HEAD_DIM
DIFF_W = 3 * DIFF_QW
DIL_W = N_DIL_HEADS * HEAD_DIM
IN_W = DIFF_W + 3 * DIL_W
LANES = 128
PAIR_W = 3 * LANES
NEG = -1e30
LOG2E = math.log2(math.e)
ONES_ROWS = 16
DIRECT_EXP_MAX_LOGIT = 60.0
BIAS_ROWS = 16
DIFF_MAX_TILES = 2
DIFF_GROUP_LANES = 8 * LANES
DIL_MAX_BLOCKS = 16
NRM_DIFF_K = 4 * N_DIL_PAIRS
NRM_DIFF_Q = NRM_DIFF_K + N_DIFF_HEADS
VMEM_LIMIT = 56 * 1024 * 1024

BF16 = jnp.bfloat16
F32 = jnp.float32


def _rms(x, g):
    return x * lax.rsqrt(jnp.mean(x * x, axis=-1, keepdims=True) + EPS) * g


def _params(sem):
    return pltpu.CompilerParams(dimension_semantics=sem, vmem_limit_bytes=VMEM_LIMIT)


def _in_proj_kernel(x_ref, g_ref, w_ref, wqt_ref, wvt_ref, qt_ref, k_ref, vt_ref,
                    pb1_ref, pb4_ref, pb16_ref, nrm_ref, slab_ref, slab4_ref):
    scale = HEAD_DIM ** -0.5
    nt = (((1,), (1,)), ((), ()))
    tm = x_ref.shape[1]
    lane = lax.broadcasted_iota(jnp.int32, (1, LANES), 1)
    nrm = jnp.zeros((1, LANES), F32)
    h = _rms(x_ref[0], g_ref[...]).astype(BF16)
    for c in range(3):
        p = jnp.dot(h, w_ref[:, DIFF_W + c * DIL_W:DIFF_W + (c + 1) * DIL_W],
                    preferred_element_type=F32)
        if c == 0:
            p = p * (scale * LOG2E)
        cols = slice(c * LANES, (c + 1) * LANES)
        for hp in range(N_DIL_PAIRS):
            ph = p[:, hp * LANES:(hp + 1) * LANES]
            if c < 2:
                sq = ph * ph
                for e in range(2):
                    mine = (lane >= e * HEAD_DIM) & (lane < (e + 1) * HEAD_DIM)
                    big = jnp.max(jnp.sum(jnp.where(mine, sq, 0.0), axis=1, keepdims=True),
                                  axis=0, keepdims=True)
                    nrm = jnp.where(lane == 4 * hp + 2 * e + c, big, nrm)
            pb1_ref[0, hp, :, cols] = ph.astype(BF16)
            s = c * N_DIL_PAIRS + hp
            slab_ref[s] = ph
            for g in range(4):
                v4 = slab_ref[s, pl.ds(g, tm // 4, stride=4), :]
                pb4_ref[0, hp, g, :, cols] = v4.astype(BF16)
                slab4_ref[s, g] = v4
            for g in range(4):
                for c2 in range(4):
                    v16 = slab4_ref[s, g, pl.ds(c2, tm // 16, stride=4), :]
                    pb16_ref[0, hp, 4 * c2 + g, :, cols] = v16.astype(BF16)
    kd = jnp.dot(h, w_ref[:, DIFF_QW:2 * DIFF_QW], preferred_element_type=F32)
    k_ref[0] = kd.astype(BF16)
    for hh in range(N_DIFF_HEADS):
        kh = kd[:, hh * LANES:(hh + 1) * LANES]
        big = jnp.max(jnp.sum(kh * kh, axis=1, keepdims=True), axis=0, keepdims=True)
        nrm = jnp.where(lane == NRM_DIFF_K + hh, big, nrm)
    qt = lax.dot_general(wqt_ref[...], h, nt, preferred_element_type=F32) * (scale * LOG2E)
    qt_ref[0] = qt.astype(BF16)
    for hm in range(2 * N_DIFF_HEADS):
        qm = qt[hm * HEAD_DIM:(hm + 1) * HEAD_DIM, :]
        big = jnp.max(jnp.sum(qm * qm, axis=0, keepdims=True), axis=1, keepdims=True)
        nrm = jnp.where(lane == NRM_DIFF_Q + hm, big, nrm)
    nrm_ref[0, 0] = nrm
    vt = lax.dot_general(wvt_ref[...], h, nt, preferred_element_type=F32).astype(BF16)
    for hh in range(N_DIFF_HEADS):
        vt_ref[0, hh, 0] = vt[hh * LANES:(hh + 1) * LANES, :]


def _in_proj(x, g_mix, w_in, wq_t, wv_t, tm):
    B, S, D = x.shape
    H = N_DIFF_HEADS
    P = N_DIL_PAIRS
    return pl.pallas_call(
        _in_proj_kernel,
        grid=(B, S // tm),
        in_specs=[
            pl.BlockSpec((1, tm, D), lambda b, i: (b, i, 0)),
            pl.BlockSpec((1, D), lambda b, i: (0, 0)),
            pl.BlockSpec((D, IN_W), lambda b, i: (0, 0)),
            pl.BlockSpec((DIFF_QW, D), lambda b, i: (0, 0)),
            pl.BlockSpec((DIFF_QW, D), lambda b, i: (0, 0)),
        ],
        out_specs=[
            pl.BlockSpec((1, DIFF_QW, tm), lambda b, i: (b, 0, i)),
            pl.BlockSpec((1, tm, DIFF_QW), lambda b, i: (b, i, 0)),
            pl.BlockSpec((1, H, 1, LANES, tm), lambda b, i: (b, 0, i, 0, 0)),
            pl.BlockSpec((1, P, tm, PAIR_W), lambda b, i: (b, 0, i, 0)),
            pl.BlockSpec((1, P, 4, tm // 4, PAIR_W), lambda b, i: (b, 0, 0, i, 0)),
            pl.BlockSpec((1, P, 16, tm // 16, PAIR_W), lambda b, i: (b, 0, 0, i, 0)),
            pl.BlockSpec((1, 1, 1, LANES), lambda b, i: (b, i, 0, 0)),
        ],
        out_shape=[
            jax.ShapeDtypeStruct((B, DIFF_QW, S), BF16),
            jax.ShapeDtypeStruct((B, S, DIFF_QW), BF16),
            jax.ShapeDtypeStruct((B, H, S // tm, LANES, tm), BF16),
            jax.ShapeDtypeStruct((B, P, S, PAIR_W), BF16),
            jax.ShapeDtypeStruct((B, P, 4, S // 4, PAIR_W), BF16),
            jax.ShapeDtypeStruct((B, P, 16, S // 16, PAIR_W), BF16),
            jax.ShapeDtypeStruct((B, S // tm, 1, LANES), F32),
        ],
        scratch_shapes=[
            pltpu.VMEM((3 * P, tm, LANES), F32),
            pltpu.VMEM((3 * P, 4, tm // 4, LANES), F32),
        ],
        compiler_params=_params(("parallel", "parallel")),
        name="in_proj",
    )(x, g_mix, w_in, wq_t, wv_t)


def _diff_attn_kernel(slopes_ref, lam_ref, g_ref, nrm_ref, qt_ref, k_ref, vt_ref, o_ref,
                      q2t_ref, m_ref, acc_ref, feat_ref, *, bq, bk, lam_init):
    G = min(DIFF_GROUP_LANES, 2 * bq)
    h = pl.program_id(1)
    qi = pl.program_id(2)
    nkv = k_ref.shape[1] // bk
    slope2 = slopes_ref[h] * LOG2E

    qt = qt_ref[0]
    sub = lax.broadcasted_iota(jnp.int32, qt.shape, 0)
    zero = jnp.zeros_like(qt)
    q2t_ref[0:LANES, 0:bq] = jnp.where(sub < HEAD_DIM, qt, zero)
    q2t_ref[0:LANES, bq:2 * bq] = jnp.where(sub >= HEAD_DIM, qt, zero)
    max_tiles = feat_ref.shape[0]
    assert max_tiles * BIAS_ROWS <= LANES

    @pl.when(qi == 0)
    def _():
        q2t_ref[LANES:2 * LANES, :] = jnp.zeros((LANES, 2 * bq), BF16)
        f_row = lax.broadcasted_iota(jnp.int32, (bk, LANES), 0)
        f_lane = lax.broadcasted_iota(jnp.int32, (bk, LANES), 1)
        f_slot = f_lane % BIAS_ROWS
        feat = jnp.where(f_slot < 2, f_row % 256,
                         jnp.where(f_slot < 4, f_row // 256,
                                   jnp.where(f_slot < 7, 1, 0)))
        for tt in range(max_tiles):
            mine = f_lane // BIAS_ROWS == tt
            feat_ref[tt] = jnp.where(mine, feat, 0).astype(F32).astype(BF16)

    acc_ref[...] = jnp.zeros(acc_ref.shape, F32)
    key_off = lax.broadcasted_iota(jnp.int32, (bk, G), 0).astype(F32)
    lane = lax.broadcasted_iota(jnp.int32, (1, 2 * bq), 1)
    pos_q = (qi * bq + jnp.where(lane >= bq, lane - bq, lane)).astype(F32)
    ones = jnp.ones((ONES_ROWS, bk), BF16)

    q_rows = nrm_ref.shape[1] * bq // k_ref.shape[1]
    k_big = jnp.max(nrm_ref[0, :, 0, :], axis=0, keepdims=True)
    q_big = jnp.max(nrm_ref[0, pl.ds(qi * q_rows, q_rows), 0, :], axis=0, keepdims=True)
    nlane = lax.broadcasted_iota(jnp.int32, k_big.shape, 1)
    k_max = jnp.max(jnp.where(nlane == NRM_DIFF_K + h, k_big, 0.0))
    q_mine = (nlane == NRM_DIFF_Q + 2 * h) | (nlane == NRM_DIFF_Q + 2 * h + 1)
    q_max = jnp.max(jnp.where(q_mine, q_big, 0.0))
    direct = q_max * k_max <= DIRECT_EXP_MAX_LOGIT ** 2

    def online_tile(j, carry):
        j = jnp.asarray(j, jnp.int32)
        start = pl.multiple_of(j * bk, bk)
        k_t = k_ref[0, pl.ds(start, bk), :]
        vt1 = jnp.concatenate([vt_ref[0, 0, j], ones], axis=0)
        j0 = (j * bk).astype(F32)
        groups = [slice(gi * G, (gi + 1) * G) for gi in range(2 * bq // G)]
        scores = [jnp.dot(k_t, q2t_ref[0:LANES, sl], preferred_element_type=F32)
                  for sl in groups]
        for sl, s in zip(groups, scores):
            u = s - slope2 * jnp.abs(pos_q[:, sl] - (j0 + key_off))
            m_old = m_ref[:, sl]
            m_new = jnp.maximum(m_old, jnp.max(u, axis=0, keepdims=True))
            alpha = jnp.exp2(m_old - m_new)
            p = jnp.exp2(u - m_new).astype(BF16)
            acc_ref[:, sl] = alpha * acc_ref[:, sl] + jnp.dot(
                vt1, p, preferred_element_type=F32)
            m_ref[:, sl] = m_new
        return carry

    def direct_tiles(j, n, side):
        groups = [slice(gi * G, (gi + 1) * G) for gi in range(2 * bq // G)]

        def bias_rows(j0):
            def bf16_part(x):
                return x.astype(BF16).astype(F32)

            sgn = -float(side)
            sv = jnp.zeros_like(pos_q) + sgn * slope2
            s_hi = bf16_part(sv)
            s_lo = bf16_part(sv - s_hi)
            a = sv * (j0 - pos_q)
            a1 = bf16_part(a)
            a2 = bf16_part(a - a1)
            a3 = bf16_part(a - a1 - a2)
            pieces = [s_hi, s_lo, s_hi * 256.0, s_lo * 256.0, a1, a2, a3]
            row = lax.broadcasted_iota(jnp.int32, (BIAS_ROWS, 2 * bq), 0)
            w = jnp.zeros((BIAS_ROWS, 2 * bq), F32)
            for i, piece in enumerate(pieces):
                w = jnp.where(row == i, piece, w)
            return w.astype(BF16)

        def operands(tt):
            jj = j + tt
            k_t = k_ref[0, pl.ds(pl.multiple_of(jj * bk, bk), bk), :]
            j0 = (jj * bk).astype(F32)
            if side != 0:
                q2t_ref[LANES + tt * BIAS_ROWS:LANES + (tt + 1) * BIAS_ROWS, :] = bias_rows(j0)
                k_t = jnp.concatenate([k_t, feat_ref[tt]], axis=1)
            return k_t, vt_ref[0, 0, jj], j0

        def score(k_t, sl):
            rows = slice(0, k_t.shape[1])
            return jnp.dot(k_t, q2t_ref[rows, sl], preferred_element_type=F32)

        def pv(vt, j0, sl, s):
            if side == 0:
                s = s - slope2 * jnp.abs(pos_q[:, sl] - (j0 + key_off))
            p = jnp.exp2(s)
            return (jnp.dot(vt, p.astype(BF16), preferred_element_type=F32),
                    jnp.sum(p, axis=0, keepdims=True))

        tiles = [operands(tt) for tt in range(n)]
        scores = [score(tiles[0][0], sl) for sl in groups]
        sums = [None] * len(groups)
        for tt in range(n):
            _, vt, j0 = tiles[tt]
            nxt = []
            for gi, sl in enumerate(groups):
                x, lx = pv(vt, j0, sl, scores[gi])
                sums[gi] = (x, lx) if tt == 0 else (sums[gi][0] + x, sums[gi][1] + lx)
                if tt + 1 < n:
                    nxt.append(score(tiles[tt + 1][0], sl))
            scores = nxt
        for sl, (x, lx) in zip(groups, sums):
            acc_ref[0:LANES, sl] += x
            acc_ref[LANES:LANES + 1, sl] += lx

    def all_tiles(online):
        ndiag = bq // bk
        if online:
            m_ref[...] = jnp.full(m_ref.shape, -jnp.inf, F32)
            lax.fori_loop(0, nkv, online_tile, 0)
            return

        assert max_tiles % 2 == 0 and ndiag % 2 == 0 and nkv % 2 == 0

        def run(first, count, side):
            def body(t, carry):
                direct_tiles(first + t * max_tiles, max_tiles, side)
                return carry

            full = count // max_tiles
            lax.fori_loop(0, full, body, 0)
            for extra in range(2, max_tiles, 2):
                @pl.when(count - full * max_tiles == extra)
                def _():
                    direct_tiles(first + full * max_tiles, extra, side)

        run(0, qi * ndiag, -1)
        for d in range(0, ndiag, 2):
            direct_tiles(qi * ndiag + d, 2, 0)
        run((qi + 1) * ndiag, nkv - (qi + 1) * ndiag, 1)

    @pl.when(direct)
    def _():
        all_tiles(False)

    @pl.when(jnp.logical_not(direct))
    def _():
        all_tiles(True)

    lq = lam_ref[...]
    lam = (jnp.exp(jnp.sum(lq[0:1] * lq[1:2], axis=1, keepdims=True))
           - jnp.exp(jnp.sum(lq[2:3] * lq[3:4], axis=1, keepdims=True)) + lam_init)
    acc = acc_ref[...]
    ot = acc[0:LANES] * (1.0 / acc[LANES:LANES + 1])
    odt = ot[:, 0:bq] - lam * ot[:, bq:2 * bq]
    odt = odt * lax.rsqrt(jnp.mean(odt * odt, axis=0, keepdims=True) + EPS)
    od = odt.T * g_ref[pl.ds(h, 1), :] * (1.0 - lam_init)
    o_ref[0] = od.astype(BF16)


def _diff_attn(qt, k, vt, nrm, slopes, lam_qk, g_diff, lam_init, bq):
    B, S, _ = k.shape
    H = N_DIFF_HEADS
    nkv, bk = vt.shape[2], vt.shape[4]
    assert bq % bk == 0
    kern = functools.partial(_diff_attn_kernel, bq=bq, bk=bk, lam_init=lam_init)
    return pl.pallas_call(
        kern,
        grid=(B, H, S // bq),
        in_specs=[
            pl.BlockSpec(memory_space=pltpu.SMEM),
            pl.BlockSpec((4, HEAD_DIM), lambda b, h, i: (0, 0)),
            pl.BlockSpec((H, LANES), lambda b, h, i: (0, 0)),
            pl.BlockSpec((1, nrm.shape[1], 1, LANES), lambda b, h, i: (b, 0, 0, 0)),
            pl.BlockSpec((1, LANES, bq), lambda b, h, i: (b, h, i)),
            pl.BlockSpec((1, S, LANES), lambda b, h, i: (b, 0, h)),
            pl.BlockSpec((1, 1, nkv, LANES, bk), lambda b, h, i: (b, h, 0, 0, 0)),
        ],
        out_specs=pl.BlockSpec((1, bq, LANES), lambda b, h, i: (b, i, h)),
        out_shape=jax.ShapeDtypeStruct((B, S, H * LANES), BF16),
        scratch_shapes=[
            pltpu.VMEM((2 * LANES, 2 * bq), BF16),
            pltpu.VMEM((1, 2 * bq), F32),
            pltpu.VMEM((LANES + ONES_ROWS, 2 * bq), F32),
            pltpu.VMEM((DIFF_MAX_TILES, bk, LANES), BF16),
        ],
        compiler_params=_params(("parallel", "parallel", "arbitrary")),
        name="diff_attn",
    )(slopes, lam_qk, g_diff, nrm, qt, k, vt)


def _dil_bias(dil, slope_a, slope_b):
    QB, R = LANES, DIL_RADIUS
    KW = QB + 2 * R
    r_i = lax.broadcasted_iota(jnp.int32, (QB, KW), 0)
    c_i = lax.broadcasted_iota(jnp.int32, (QB, KW), 1)
    rel = c_i - r_i - R
    band = (rel >= -R) & (rel <= R)
    dist = (jnp.abs(rel) * dil).astype(F32)
    return jnp.concatenate([jnp.where(band, -(slope_a * LOG2E) * dist, NEG),
                            jnp.where(band, -(slope_b * LOG2E) * dist, NEG)], axis=0)


def _dil_segment(q_at, k_at, v_at, nres, L, dil, tok0, bias_at, first, last, online,
                 sa_ref, sb_ref, m_ref, o_ref, kpad, vpad_a, vpad_b):
    QB, R = LANES, DIL_RADIUS
    KW = QB + 2 * R
    pitch = L + 2 * R
    per_class = L // QB
    lane_q = lax.broadcasted_iota(jnp.int32, (QB, LANES), 1)
    head_a = lane_q < HEAD_DIM
    c2 = lax.broadcasted_iota(jnp.int32, (2 * QB, KW), 1)

    zpad = jnp.zeros((R, LANES), BF16)
    own_a = lax.broadcasted_iota(jnp.int32, (L, LANES), 1) < HEAD_DIM
    one = jnp.ones((L, LANES), BF16)
    for c in range(nres):
        base = c * pitch
        for pad in (kpad, vpad_a, vpad_b):
            pad[base:base + R, :] = zpad
            pad[base + R + L:base + pitch, :] = zpad
        kpad[base + R:base + R + L, :] = k_at(c)
        v_src = v_at(c)
        vpad_a[base + R:base + R + L, :] = jnp.where(own_a, v_src, one)
        vpad_b[base + R:base + R + L, :] = jnp.where(own_a, one, v_src)

    def rows(c, i0):
        if dil == 1:
            return pl.ds(i0, QB)
        return pl.ds(tok0 + c + dil * i0, QB, stride=dil)

    def scores(c, i0):
        q = q_at(c, i0)
        zero = jnp.zeros_like(q)
        q2 = jnp.concatenate([jnp.where(head_a, q, zero), jnp.where(head_a, zero, q)], axis=0)
        s = lax.dot_general(q2, kpad[pl.ds(c * pitch + i0, KW), :], (((1,), (1,)), ((), ())),
                            preferred_element_type=F32)
        return s + bias_at[...]

    def finish(c, i0, s):
        if online:
            key = i0 - R + c2
            s = jnp.where((key >= 0) & (key < L), s, NEG)
            m_c = jnp.max(s, axis=1, keepdims=True)
            s = s - m_c
        p = jnp.exp2(s).astype(BF16)
        win = pl.ds(c * pitch + i0, KW)
        sa = jnp.dot(p[0:QB], vpad_a[win, :], preferred_element_type=F32)
        sb = jnp.dot(p[QB:2 * QB], vpad_b[win, :], preferred_element_type=F32)
        r = rows(c, i0)
        if online:
            ma = jnp.broadcast_to(m_c[0:QB], (QB, LANES))
            mb = jnp.broadcast_to(m_c[QB:2 * QB], (QB, LANES))
            if not first:
                m_p = m_ref[r, :]
                mpa = jnp.max(jnp.where(head_a, m_p, -jnp.inf), axis=1, keepdims=True)
                mpb = jnp.max(jnp.where(head_a, -jnp.inf, m_p), axis=1, keepdims=True)
                mna, mnb = jnp.maximum(mpa, ma), jnp.maximum(mpb, mb)
                sa = sa_ref[r, :] * jnp.exp2(mpa - mna) + sa * jnp.exp2(ma - mna)
                sb = sb_ref[r, :] * jnp.exp2(mpb - mnb) + sb * jnp.exp2(mb - mnb)
                ma, mb = mna, mnb
            if not last:
                m_ref[r, :] = jnp.where(head_a, ma, mb)
        elif not first:
            sa = sa_ref[r, :] + sa
            sb = sb_ref[r, :] + sb
        if last:
            half = LANES // 2
            o_ref[0, 0, r, :] = jnp.where(head_a, sa / pltpu.roll(sa, half, 1),
                                          sb / pltpu.roll(sb, half, 1))
        else:
            sa_ref[r, :] = sa
            sb_ref[r, :] = sb

    def blocks(todo):
        ss = [scores(c, i0) for c, i0 in todo]
        for (c, i0), s in zip(todo, ss):
            finish(c, i0, s)

    if online:
        for c in range(nres):
            def body(n, carry, c=c):
                n = jnp.asarray(n, jnp.int32)
                blocks([(c, pl.multiple_of(n * QB, QB))])
                return carry
            lax.fori_loop(0, per_class, body, 0)
    elif per_class <= DIL_MAX_BLOCKS:
        todo = [(c, u * QB) for c in range(nres) for u in range(per_class)]
        for at in range(0, len(todo), DIL_MAX_BLOCKS):
            blocks(todo[at:at + DIL_MAX_BLOCKS])
    else:
        assert nres == 1 and per_class % DIL_MAX_BLOCKS == 0

        def body(n, carry):
            n = jnp.asarray(n, jnp.int32)
            blocks([(0, pl.multiple_of((n * DIL_MAX_BLOCKS + u) * QB, QB))
                    for u in range(DIL_MAX_BLOCKS)])
            return carry

        lax.fori_loop(0, per_class // DIL_MAX_BLOCKS, body, 0)


def _dil_kernel(slopes_ref, nrm_ref, pb1_ref, pb4_ref, pb16_ref, o_ref,
                sa_ref, sb_ref, m_ref, kpad, vpad_a, vpad_b, bias_ref, *, group16, group4):
    hp = pl.program_id(1)
    t = pl.program_id(2)
    S = pb1_ref.shape[2]
    n16, n4 = 16 // group16, 4 // group4
    dils = (16, 4, 1)

    @pl.when(t == 0)
    def _():
        for i, dil in enumerate(dils):
            bias_ref[i] = _dil_bias(dil, slopes_ref[2 * hp], slopes_ref[2 * hp + 1])

    big = jnp.max(nrm_ref[0, :, 0, :], axis=0, keepdims=True)
    lane = lax.broadcasted_iota(jnp.int32, big.shape, 1)

    def pick(idx):
        return jnp.max(jnp.where(lane == idx, big, 0.0))

    direct = ((pick(4 * hp) * pick(4 * hp + 1) <= DIRECT_EXP_MAX_LOGIT ** 2)
              & (pick(4 * hp + 2) * pick(4 * hp + 3) <= DIRECT_EXP_MAX_LOGIT ** 2))

    def pattern(online):
        args = (sa_ref, sb_ref, m_ref, o_ref, kpad, vpad_a, vpad_b)

        @pl.when(t < n16)
        def _():
            _dil_segment(lambda c, i0: pb16_ref[0, 0, c, pl.ds(i0, LANES), 0:LANES],
                         lambda c: pb16_ref[0, 0, c, :, LANES:2 * LANES],
                         lambda c: pb16_ref[0, 0, c, :, 2 * LANES:3 * LANES],
                         group16, S // 16, 16, t * group16, bias_ref.at[0],
                         True, False, online, *args)

        @pl.when((t >= n16) & (t < n16 + n4))
        def _():
            _dil_segment(lambda c, i0: pb4_ref[0, 0, c, pl.ds(i0, LANES), 0:LANES],
                         lambda c: pb4_ref[0, 0, c, :, LANES:2 * LANES],
                         lambda c: pb4_ref[0, 0, c, :, 2 * LANES:3 * LANES],
                         group4, S // 4, 4, (t - n16) * group4, bias_ref.at[1],
                         False, False, online, *args)

        @pl.when(t == n16 + n4)
        def _():
            _dil_segment(lambda c, i0: pb1_ref[0, 0, pl.ds(i0, LANES), 0:LANES],
                         lambda c: pb1_ref[0, 0, :, LANES:2 * LANES],
                         lambda c: pb1_ref[0, 0, :, 2 * LANES:3 * LANES],
                         1, S, 1, 0, bias_ref.at[2],
                         False, True, online, *args)

    @pl.when(direct)
    def _():
        pattern(False)

    @pl.when(jnp.logical_not(direct))
    def _():
        pattern(True)


def _dil_attn(pb1, pb4, pb16, nrm, slopes, group16=8, group4=2):
    B, P, S, _ = pb1.shape
    n16, n4 = 16 // group16, 4 // group4
    nt = nrm.shape[1]
    pad_rows = max(group16 * (S // 16 + 2 * DIL_RADIUS), group4 * (S // 4 + 2 * DIL_RADIUS),
                   S + 2 * DIL_RADIUS)
    return pl.pallas_call(
        functools.partial(_dil_kernel, group16=group16, group4=group4),
        grid=(B, P, n16 + n4 + 1),
        in_specs=[
            pl.BlockSpec(memory_space=pltpu.SMEM),
            pl.BlockSpec((1, nt, 1, LANES), lambda b, p, t: (b, 0, 0, 0)),
            pl.BlockSpec((1, 1, S, PAIR_W), lambda b, p, t: (b, p, 0, 0)),
            pl.BlockSpec((1, 1, group4, S // 4, PAIR_W),
                         lambda b, p, t: (b, p, jnp.clip(t - n16, 0, n4 - 1), 0, 0)),
            pl.BlockSpec((1, 1, group16, S // 16, PAIR_W),
                         lambda b, p, t: (b, p, jnp.minimum(t, n16 - 1), 0, 0)),
        ],
        out_specs=pl.BlockSpec((1, 1, S, LANES), lambda b, p, t: (b, p, 0, 0)),
        out_shape=jax.ShapeDtypeStruct((B, P, S, LANES), F32),
        scratch_shapes=[
            pltpu.VMEM((S, LANES), F32),
            pltpu.VMEM((S, LANES), F32),
            pltpu.VMEM((S, LANES), F32),
            pltpu.VMEM((pad_rows, LANES), BF16),
            pltpu.VMEM((pad_rows, LANES), BF16),
            pltpu.VMEM((pad_rows, LANES), BF16),
            pltpu.VMEM((3, 2 * LANES, LANES + 2 * DIL_RADIUS), F32),
        ],
        compiler_params=_params(("parallel", "parallel", "arbitrary")),
        name="dil_attn",
    )(slopes, nrm, pb1, pb4, pb16)


def _out_mlp_kernel(x_ref, oa_ref, ob_ref, gdil_ref, wout_ref, gmlp_ref, wup_ref,
                    wdown_ref, gfin_ref, y_ref, *, ff_chunk):
    ob = jnp.concatenate([ob_ref[0, p] for p in range(N_DIL_PAIRS)], axis=-1)
    ob = _rms(ob, gdil_ref[...])
    mix = jnp.concatenate([oa_ref[0], ob.astype(BF16)], axis=-1)
    x1 = x_ref[0] + jnp.dot(mix, wout_ref[...], preferred_element_type=F32)
    h = _rms(x1, gmlp_ref[...]).astype(BF16)
    y = x1
    for c in range(D_FF // ff_chunk):
        u = jnp.dot(h, wup_ref[:, c * ff_chunk:(c + 1) * ff_chunk], preferred_element_type=F32)
        u = jnp.square(jnp.maximum(u, 0.0)).astype(BF16)
        y = y + jnp.dot(u, wdown_ref[c * ff_chunk:(c + 1) * ff_chunk, :],
                        preferred_element_type=F32)
    y_ref[0] = _rms(y, gfin_ref[...])


def _out_mlp(x, oa, ob, g_dil, w_out, g_mlp, w_up, w_down, g_final, tm, ff_chunk):
    B, S, D = x.shape
    const = lambda shape: pl.BlockSpec(shape, lambda b, i: (0,) * len(shape),
                                       pipeline_mode=pl.Buffered(1))
    return pl.pallas_call(
        functools.partial(_out_mlp_kernel, ff_chunk=ff_chunk),
        grid=(B, S // tm),
        in_specs=[
            pl.BlockSpec((1, tm, D), lambda b, i: (b, i, 0)),
            pl.BlockSpec((1, tm, N_DIFF_HEADS * LANES), lambda b, i: (b, i, 0)),
            pl.BlockSpec((1, N_DIL_PAIRS, tm, LANES), lambda b, i: (b, 0, i, 0)),
            const((1, DIL_W)),
            const((2 * DIL_W, D)),
            const((1, D)),
            const((D, D_FF)),
            const((D_FF, D)),
            const((1, D)),
        ],
        out_specs=pl.BlockSpec((1, tm, D), lambda b, i: (b, i, 0)),
        out_shape=jax.ShapeDtypeStruct((B, S, D), F32),
        compiler_params=_params(("parallel", "parallel")),
        name="out_mlp",
    )(x, oa, ob, g_dil, w_out, g_mlp, w_up, w_down, g_final)


def _alibi_slopes(n):
    return 2.0 ** (-8.0 * jnp.arange(1, n + 1, dtype=F32) / n)


def _layer(x, g_mix, w_in, wq_t, wv_t, lam_qk, g_diff, g_dil, w_out, g_mlp, w_up, w_down,
           g_final, lam_init):
    qt, k, vt, pb1, pb4, pb16, nrm = _in_proj(x, g_mix, w_in, wq_t, wv_t, tm=512)
    oa = _diff_attn(qt, k, vt, nrm, _alibi_slopes(N_DIFF_HEADS), lam_qk, g_diff, lam_init,
                    bq=1024)
    assert DIL_PATTERNS == ((2 * DIL_RADIUS, 1), (8 * DIL_RADIUS, 4), (32 * DIL_RADIUS, 16))
    ob = _dil_attn(pb1, pb4, pb16, nrm, _alibi_slopes(N_DIL_HEADS))
    return _out_mlp(x, oa, ob, g_dil, w_out, g_mlp, w_up, w_down, g_final,
                    tm=1024, ff_chunk=1024)


def kernel(x_prompt, x_sample, g_mix, w_in, lam_qk, g_diff, g_dil, w_out, g_mlp, w_up,
           w_down, g_final):
    assert g_mix.shape[0] == 1
    lam_init = 0.8 - 0.6 * math.exp(-0.3 * 0)
    w_in_b = w_in[0].astype(BF16)
    weights = (
        g_mix[0][None, :], w_in_b, w_in_b[:, 0:DIFF_QW].T, w_in_b[:, 2 * DIFF_QW:DIFF_W].T,
        lam_qk[0],
        g_diff[0], g_dil[0][None, :], w_out[0].astype(BF16), g_mlp[0][None, :],
        w_up[0].astype(BF16), w_down[0].astype(BF16), g_final[None, :],
    )
    return tuple(_layer(x, *weights, lam_init) for x in (x_prompt, x_sample))
```
